```python
import jax, jax.numpy as jnp
from jax import lax
import numpy as np

D_MODEL = 2048
BATCH = 8
SEQ = 4096
DEPTH = 2
DEC_BATCH = 32
DEC_SEQ = 16
PAST_LEN = 2048

CHUNK = 64
QBLOCK = 128
HEAD_DIM = 128
N_HEADS_A = 8
N_HEADS_B = 8
IDX_HEADS = 16
IDX_DIM = 64
TOPK_MAX = 256
ROPE_THETA = 10000.0
D_FF = 5632
C_WIDTH = 2048
C_GROUPS = 16
C_GROUP_DIM = C_WIDTH // C_GROUPS
C_CHUNK = 128
RMS_EPS = 1e-6
N_AB_LAYERS = (DEPTH + 1) // 2
N_C_LAYERS = DEPTH // 2
AB_WIDTHS = (N_HEADS_A * HEAD_DIM, N_HEADS_A * HEAD_DIM, N_HEADS_A * HEAD_DIM,
             IDX_HEADS * IDX_DIM, IDX_DIM, IDX_HEADS,
             N_HEADS_B * HEAD_DIM, N_HEADS_B * HEAD_DIM, N_HEADS_B * HEAD_DIM)
AB_IN = 3 * N_HEADS_A * HEAD_DIM + IDX_HEADS * IDX_DIM + IDX_DIM + IDX_HEADS + 3 * N_HEADS_B * HEAD_DIM
AB_OUT = (N_HEADS_A + N_HEADS_B) * HEAD_DIM

kernel_name = "dsa_stickbreak_gmlp_macaron_stream_step"


def rms_norm(x, g):
    xf = x.astype(jnp.float32)
    y = xf * lax.rsqrt(jnp.mean(xf * xf, axis=-1, keepdims=True) + RMS_EPS)
    return (y * g.astype(jnp.float32)).astype(x.dtype)


def half_ffn(x, g, w1, w3, w2):
    h = rms_norm(x, g)
    return x + 0.5 * ((jax.nn.silu(h @ w1) * (h @ w3)) @ w2)


def rope(x, pos):
    half = x.shape[-1] // 2
    inv = ROPE_THETA ** (-jnp.arange(half, dtype=jnp.float32) / half)
    ang = pos.astype(jnp.float32)[:, None] * inv[None, :]
    cos = jnp.cos(ang)[:, None, :]
    sin = jnp.sin(ang)[:, None, :]
    xf = x.astype(jnp.float32)
    x1, x2 = xf[..., :half], xf[..., half:]
    return jnp.concatenate([x1 * cos - x2 * sin, x2 * cos + x1 * sin], axis=-1).astype(x.dtype)


def split_cols(z, widths):
    out, start = [], 0
    for w in widths:
        out.append(z[..., start:start + w])
        start += w
    return out


def ab_project(h, pos, w_in):
    b, t = h.shape[0], h.shape[1]
    qa, ka, va, iq, ik, iw, qb, kb, vb = split_cols(h @ w_in, AB_WIDTHS)
    qa = rope(qa.reshape(b, t, N_HEADS_A, HEAD_DIM), pos)
    ka = rope(ka.reshape(b, t, N_HEADS_A, HEAD_DIM), pos)
    va = va.reshape(b, t, N_HEADS_A, HEAD_DIM)
    iq = rope(iq.reshape(b, t, IDX_HEADS, IDX_DIM), pos) * (IDX_DIM ** -0.5)
    ik = rope(ik[:, :, None, :], pos)[:, :, 0, :]
    iw = iw * (IDX_HEADS ** -0.5)
    qb = qb.reshape(b, t, N_HEADS_B, HEAD_DIM)
    kb = kb.reshape(b, t, N_HEADS_B, HEAD_DIM)
    vb = vb.reshape(b, t, N_HEADS_B, HEAD_DIM)
    return qa, ka, va, iq, ik, iw, qb, kb, vb


def dsa_attend(q, iq, iw, k, v, ik, q_pos, n_top):
    L = k.shape[0]
    rel = jax.nn.relu(jnp.einsum('thd,sd->ths', iq, ik).astype(jnp.float32))
    score = jnp.einsum('th,ths->ts', iw.astype(jnp.float32), rel)
    limit = (q_pos // CHUNK + 1) * CHUNK
    score = jnp.where(jnp.arange(L)[None, :] < limit[:, None], score, -jnp.inf)
    _, idx = lax.top_k(score, n_top)
    valid = idx < limit[:, None]
    kg = jnp.take(k, idx, axis=0)
    vg = jnp.take(v, idx, axis=0)
    logits = jnp.einsum('thd,tnhd->thn', q, kg).astype(jnp.float32) * (HEAD_DIM ** -0.5)
    logits = jnp.where(valid[:, None, :], logits, -jnp.inf)
    p = jax.nn.softmax(logits, axis=-1)
    return jnp.einsum('thn,tnhd->thd', p.astype(v.dtype), vg)


def stick_breaking(q, k, v, q_pos):
    L = k.shape[1]
    z = jnp.einsum('bthd,bshd->bhts', q, k).astype(jnp.float32) * (HEAD_DIM ** -0.5)
    causal = jnp.arange(L)[None, :] < q_pos[:, None]
    log_keep = jnp.where(causal, jax.nn.log_sigmoid(-z), 0.0)
    log_after = lax.cumsum(log_keep, axis=3, reverse=True) - log_keep
    a = jnp.where(causal, jnp.exp(jax.nn.log_sigmoid(z) + log_after), 0.0)
    return jnp.einsum('bhts,bshd->bthd', a.astype(v.dtype), v)


def ab_prompt(h, w_in, w_out):
    b, s = h.shape[0], h.shape[1]
    pos = jnp.arange(s)
    qa, ka, va, iq, ik, iw, qb, kb, vb = ab_project(h, pos, w_in)
    nb = s // QBLOCK
    n_top = min(TOPK_MAX, s // 4)
    pos_blk = pos.reshape(nb, QBLOCK)

    def dsa_seq(args):
        q1, iq1, iw1, k1, v1, ik1 = args
        def blk(bargs):
            qblk, iqblk, iwblk, pblk = bargs
            return dsa_attend(qblk, iqblk, iwblk, k1, v1, ik1, pblk, n_top)
        o = lax.map(blk, (q1.reshape(nb, QBLOCK, N_HEADS_A, HEAD_DIM),
                          iq1.reshape(nb, QBLOCK, IDX_HEADS, IDX_DIM),
                          iw1.reshape(nb, QBLOCK, IDX_HEADS), pos_blk))
        return o.reshape(s, N_HEADS_A, HEAD_DIM)

    o_a = lax.map(dsa_seq, (qa, iq, iw, ka, va, ik))
    qb_blk = qb.reshape(b, nb, QBLOCK, N_HEADS_B, HEAD_DIM).swapaxes(0, 1)
    o_b = lax.map(lambda a: stick_breaking(a[0], kb, vb, a[1]), (qb_blk, pos_blk))
    o_b = o_b.swapaxes(0, 1).reshape(b, s, N_HEADS_B * HEAD_DIM)
    o = jnp.concatenate([o_a.reshape(b, s, N_HEADS_A * HEAD_DIM), o_b], axis=-1)
    return o @ w_out, (ka, va, ik, kb, vb)


def ab_sample(h, c_ak, c_av, c_aik, c_bk, c_bv, w_in, w_out):
    b, t = h.shape[0], h.shape[1]
    past = c_ak.shape[1]
    pos = past + jnp.arange(t)
    qa, ka, va, iq, ik, iw, qb, kb, vb = ab_project(h, pos, w_in)
    ka_all = jnp.concatenate([c_ak, ka], axis=1)
    va_all = jnp.concatenate([c_av, va], axis=1)
    ik_all = jnp.concatenate([c_aik, ik], axis=1)
    kb_all = jnp.concatenate([c_bk, kb], axis=1)
    vb_all = jnp.concatenate([c_bv, vb], axis=1)
    n_top = min(TOPK_MAX, (past + t) // 4)
    o_a = lax.map(lambda a: dsa_attend(a[0], a[1], a[2], a[3], a[4], a[5], pos, n_top),
                  (qa, iq, iw, ka_all, va_all, ik_all))
    o_b = stick_breaking(qb, kb_all, vb_all, pos)
    o = jnp.concatenate([o_a.reshape(b, t, N_HEADS_A * HEAD_DIM),
                         o_b.reshape(b, t, N_HEADS_B * HEAD_DIM)], axis=-1)
    return o @ w_out, (ka, va, ik, kb, vb)


def c_project(h, w_in, v_norm):
    z = jax.nn.gelu(h @ w_in)
    u, v = z[..., :C_WIDTH], z[..., C_WIDTH:]
    return u, rms_norm(v, v_norm)


def c_masked_weights(w_s):
    i = jnp.arange(C_CHUNK)
    mask = (i[None, :] // CHUNK) <= (i[:, None] // CHUNK)
    return jnp.where(mask[None], w_s, 0.0).astype(w_s.dtype)


def c_prompt(h, w_in, v_norm, w_s, b_s, w_out):
    b, s = h.shape[0], h.shape[1]
    u, v = c_project(h, w_in, v_norm)
    nc = s // C_CHUNK
    vg = v.reshape(b, nc, C_CHUNK, C_GROUPS, C_GROUP_DIM)
    mix = jnp.einsum('gij,bnjgc->bnigc', c_masked_weights(w_s), vg) + b_s.T[None, None, :, :, None]
    return (u * mix.reshape(b, s, C_WIDTH)) @ w_out


def c_sample(h, w_in, v_norm, w_s, b_s, w_out):
    b, t = h.shape[0], h.shape[1]
    u, v = c_project(h, w_in, v_norm)
    vg = v.reshape(b, t, C_GROUPS, C_GROUP_DIM)
    w_m = c_masked_weights(w_s)[:, :t, :t]
    mix = jnp.einsum('gij,bjgc->bigc', w_m, vg) + b_s[:, :t].T[None, :, :, None]
    return (u * mix.reshape(b, t, C_WIDTH)) @ w_out, v


def setup_inputs(seed: int = 0) -> dict:
    key = jax.random.key(seed)
    ks = jax.random.split(key, 32)

    def nrm(k, shape, scale):
        return jax.random.normal(k, shape, jnp.float32) * scale

    def gain(k, shape):
        return 1.0 + 0.05 * jax.random.normal(k, shape, jnp.float32)

    return {
        "x_prompt": nrm(ks[0], (BATCH, SEQ, D_MODEL), 1.0),
        "x_sample": nrm(ks[1], (DEC_BATCH, DEC_SEQ, D_MODEL), 1.0),
        "cache_a_k": nrm(ks[2], (N_AB_LAYERS, DEC_BATCH, PAST_LEN, N_HEADS_A, HEAD_DIM), 1.0),
        "cache_a_v": nrm(ks[3], (N_AB_LAYERS, DEC_BATCH, PAST_LEN, N_HEADS_A, HEAD_DIM), 1.0),
        "cache_a_ik": nrm(ks[4], (N_AB_LAYERS, DEC_BATCH, PAST_LEN, IDX_DIM), 1.0),
        "cache_b_k": nrm(ks[5], (N_AB_LAYERS, DEC_BATCH, PAST_LEN, N_HEADS_B, HEAD_DIM), 1.0),
        "cache_b_v": nrm(ks[6], (N_AB_LAYERS, DEC_BATCH, PAST_LEN, N_HEADS_B, HEAD_DIM), 1.0),
        "norm_ff1": gain(ks[7], (DEPTH, D_MODEL)),
        "ff1_w1": nrm(ks[8], (DEPTH, D_MODEL, D_FF), D_MODEL ** -0.5),
        "ff1_w3": nrm(ks[9], (DEPTH, D_MODEL, D_FF), D_MODEL ** -0.5),
        "ff1_w2": nrm(ks[10], (DEPTH, D_FF, D_MODEL), D_FF ** -0.5),
        "norm_mix": gain(ks[11], (DEPTH, D_MODEL)),
        "norm_ff2": gain(ks[12], (DEPTH, D_MODEL)),
        "ff2_w1": nrm(ks[13], (DEPTH, D_MODEL, D_FF), D_MODEL ** -0.5),
        "ff2_w3": nrm(ks[14], (DEPTH, D_MODEL, D_FF), D_MODEL ** -0.5),
        "ff2_w2": nrm(ks[15], (DEPTH, D_FF, D_MODEL), D_FF ** -0.5),
        "ab_w_in": nrm(ks[16], (N_AB_LAYERS, D_MODEL, AB_IN), D_MODEL ** -0.5),
        "ab_w_out": nrm(ks[17], (N_AB_LAYERS, AB_OUT, D_MODEL), AB_OUT ** -0.5),
        "c_w_in": nrm(ks[18], (N_C_LAYERS, D_MODEL, 2 * C_WIDTH), D_MODEL ** -0.5),
        "c_v_norm": gain(ks[19], (N_C_LAYERS, C_WIDTH)),
        "c_w_s": nrm(ks[20], (N_C_LAYERS, C_GROUPS, C_CHUNK, C_CHUNK), C_CHUNK ** -0.5),
        "c_b_s": gain(ks[21], (N_C_LAYERS, C_GROUPS, C_CHUNK)),
        "c_w_out": nrm(ks[22], (N_C_LAYERS, C_WIDTH, D_MODEL), C_WIDTH ** -0.5),
        "final_norm": gain(ks[23], (D_MODEL,)),
    }


def reference(x_prompt, x_sample, cache_a_k, cache_a_v, cache_a_ik, cache_b_k, cache_b_v,
              norm_ff1, ff1_w1, ff1_w3, ff1_w2, norm_mix, norm_ff2, ff2_w1, ff2_w3, ff2_w2,
              ab_w_in, ab_w_out, c_w_in, c_v_norm, c_w_s, c_b_s, c_w_out, final_norm):
    yp, ys = x_prompt, x_sample
    p_ak, p_av, p_aik, p_bk, p_bv = [], [], [], [], []
    s_ak, s_av, s_aik, s_bk, s_bv, s_cv = [], [], [], [], [], []
    for layer in range(DEPTH):
        j = layer // 2
        yp = half_ffn(yp, norm_ff1[layer], ff1_w1[layer], ff1_w3[layer], ff1_w2[layer])
        ys = half_ffn(ys, norm_ff1[layer], ff1_w1[layer], ff1_w3[layer], ff1_w2[layer])
        hp = rms_norm(yp, norm_mix[layer])
        hs = rms_norm(ys, norm_mix[layer])
        if layer % 2 == 0:
            mp, (ka, va, ik, kb, vb) = ab_prompt(hp, ab_w_in[j], ab_w_out[j])
            p_ak.append(ka); p_av.append(va); p_aik.append(ik); p_bk.append(kb); p_bv.append(vb)
            ms, (ka, va, ik, kb, vb) = ab_sample(hs, cache_a_k[j], cache_a_v[j], cache_a_ik[j],
                                                 cache_b_k[j], cache_b_v[j], ab_w_in[j], ab_w_out[j])
            s_ak.append(ka); s_av.append(va); s_aik.append(ik); s_bk.append(kb); s_bv.append(vb)
        else:
            mp = c_prompt(hp, c_w_in[j], c_v_norm[j], c_w_s[j], c_b_s[j], c_w_out[j])
            ms, vrows = c_sample(hs, c_w_in[j], c_v_norm[j], c_w_s[j], c_b_s[j], c_w_out[j])
            s_cv.append(vrows)
        yp = yp + mp
        ys = ys + ms
        yp = half_ffn(yp, norm_ff2[layer], ff2_w1[layer], ff2_w3[layer], ff2_w2[layer])
        ys = half_ffn(ys, norm_ff2[layer], ff2_w1[layer], ff2_w3[layer], ff2_w2[layer])
    y_prompt = rms_norm(yp, final_norm)
    y_sample = rms_norm(ys, final_norm)
    new_p_a_k = jnp.stack(p_ak)
    new_p_a_v = jnp.stack(p_av)
    new_p_a_ik = jnp.stack(p_aik)
    new_p_b_k = jnp.stack(p_bk)
    new_p_b_v = jnp.stack(p_bv)
    new_s_a_k = jnp.stack(s_ak)
    new_s_a_v = jnp.stack(s_av)
    new_s_a_ik = jnp.stack(s_aik)
    new_s_b_k = jnp.stack(s_bk)
    new_s_b_v = jnp.stack(s_bv)
    new_s_c_v = jnp.stack(s_cv)
    return (y_prompt, y_sample, new_p_a_k, new_p_a_v, new_p_a_ik, new_p_b_k, new_p_b_v,
            new_s_a_k, new_s_a_v, new_s_a_ik, new_s_b_k, new_s_b_v, new_s_c_v)
```

```python
import functools
import math

import jax
import jax.numpy as jnp
from jax import lax
from jax.experimental import pallas as pl
from jax.experimental.pallas import tpu as pltpu

F32 = jnp.float32
BF16 = jnp.bfloat16

RMS_EPS = 1e-6
CHUNK = 64
TOPK_MAX = 256
ROPE_THETA = 10000.0
C_CHUNK = 128
LANES = 128
HEAD_DIM = 128
NEG = -1e30
INT_MIN = -2147483648
VMEM_LIMIT_BYTES = 56 * 1024 * 1024

NT_DIMS = (((1,), (1,)), ((), ()))


def _params(*sem):
    return pltpu.CompilerParams(dimension_semantics=sem, vmem_limit_bytes=VMEM_LIMIT_BYTES)


def _rms(x, g):
    return x * lax.rsqrt(jnp.mean(x * x, axis=-1, keepdims=True) + RMS_EPS) * g


def _split_bf16(x):
    hi = x.astype(BF16)
    lo = (x - hi.astype(F32)).astype(BF16)
    return hi, lo


def _ffn_kernel(x_ref, g_ref, w1_ref, w3_ref, w2_ref, *rest, nf, final_norm):
    if final_norm:
        gf_ref, o_ref, h_scr, acc_scr = rest
    else:
        o_ref, h_scr, acc_scr = rest
    f = pl.program_id(1)

    @pl.when(f == 0)
    def _():
        h_scr[...] = _rms(x_ref[...], g_ref[...]).astype(BF16)
        acc_scr[...] = jnp.zeros_like(acc_scr)

    h = h_scr[...]
    a = jnp.dot(h, w1_ref[...], preferred_element_type=F32)
    b = jnp.dot(h, w3_ref[...], preferred_element_type=F32)
    p = (a * jax.nn.sigmoid(a) * b).astype(BF16)
    acc_scr[...] += jnp.dot(p, w2_ref[...], preferred_element_type=F32)

    @pl.when(f == nf - 1)
    def _():
        y = x_ref[...] + 0.5 * acc_scr[...]
        if final_norm:
            y = _rms(y, gf_ref[...])
        o_ref[...] = y


def _half_ffn(x, g, w1, w3, w2, final_g=None, tf=512):
    m, d = x.shape
    dff = w1.shape[1]
    tm = min(512, m)
    nf = dff // tf
    in_specs = [
        pl.BlockSpec((tm, d), lambda i, f: (i, 0)),
        pl.BlockSpec((1, d), lambda i, f: (0, 0)),
        pl.BlockSpec((d, tf), lambda i, f: (0, f)),
        pl.BlockSpec((d, tf), lambda i, f: (0, f)),
        pl.BlockSpec((tf, d), lambda i, f: (f, 0)),
    ]
    args = [x, g.reshape(1, d), w1, w3, w2]
    if final_g is not None:
        in_specs.append(pl.BlockSpec((1, d), lambda i, f: (0, 0)))
        args.append(final_g.reshape(1, d))
    return pl.pallas_call(
        functools.partial(_ffn_kernel, nf=nf, final_norm=final_g is not None),
        grid=(m // tm, nf),
        in_specs=in_specs,
        out_specs=pl.BlockSpec((tm, d), lambda i, f: (i, 0)),
        out_shape=jax.ShapeDtypeStruct((m, d), F32),
        scratch_shapes=[pltpu.VMEM((tm, d), BF16), pltpu.VMEM((tm, d), F32)],
        compiler_params=_params("parallel", "arbitrary"),
        name="half_ffn",
    )(*args)


def _rope_heads(z, cos, sin):
    outs = []
    for h in range(z.shape[1] // LANES):
        zh = z[:, h * LANES:(h + 1) * LANES]
        outs.append(zh * cos + pltpu.roll(zh, LANES // 2, axis=1) * sin)
    return outs


def _rope_pairs(z, cos, sin):
    lane = lax.broadcasted_iota(jnp.int32, (z.shape[0], LANES), 1)
    low = (lane % 64) < 32
    outs = []
    for h in range(z.shape[1] // LANES):
        zh = z[:, h * LANES:(h + 1) * LANES]
        partner = jnp.where(low, pltpu.roll(zh, LANES - 32, axis=1), pltpu.roll(zh, 32, axis=1))
        outs.append(zh * cos + partner * sin)
    return outs


def _proj_a_kernel(x_ref, g_ref, w_ref, c128_ref, s128_ref, c64_ref, s64_ref,
                   qa_ref, kaf_ref, kab_ref, vaf_ref, vab_ref, iq_ref, ikw_ref, h_scr,
                   *, n_idx_heads, idx_dim):
    j = pl.program_id(1)

    @pl.when(j == 0)
    def _():
        h_scr[...] = _rms(x_ref[...], g_ref[...]).astype(BF16)

    z = jnp.dot(h_scr[...], w_ref[...], preferred_element_type=F32)
    wa = qa_ref.shape[1]

    @pl.when(j == 0)
    def _():
        scale = HEAD_DIM ** -0.5
        for h, r in enumerate(_rope_heads(z[:, :wa], c128_ref[...], s128_ref[...])):
            qa_ref[:, h * LANES:(h + 1) * LANES] = (r * scale).astype(BF16)

    @pl.when(j == 1)
    def _():
        for h, r in enumerate(_rope_heads(z[:, :wa], c128_ref[...], s128_ref[...])):
            kaf_ref[:, h * LANES:(h + 1) * LANES] = r
            kab_ref[:, h * LANES:(h + 1) * LANES] = r.astype(BF16)

    @pl.when(j == 2)
    def _():
        kv = z[:, :wa]
        vaf_ref[...] = kv
        vab_ref[...] = kv.astype(BF16)

    @pl.when(j == 3)
    def _():
        wi = iq_ref.shape[1]
        for h, r in enumerate(_rope_pairs(z[:, :wi], c64_ref[...], s64_ref[...])):
            iq_ref[:, h * LANES:(h + 1) * LANES] = r * (idx_dim ** -0.5)

    @pl.when(j == 4)
    def _():
        zz = z[:, :LANES]
        r = _rope_pairs(zz, c64_ref[...], s64_ref[...])[0]
        lane = lax.broadcasted_iota(jnp.int32, zz.shape, 1)
        ikw_ref[...] = jnp.where(lane < idx_dim, r, zz * (n_idx_heads ** -0.5))


def _proj_a(x, g, w, tabs, n_pos_rows, ha, n_idx_heads, idx_dim):
    m, d = x.shape
    tm = min(512, m)
    wa = ha * HEAD_DIM
    wi = n_idx_heads * idx_dim
    tn = w.shape[1] // 5
    npb = n_pos_rows // tm
    row = lambda i, j: (i, 0)
    tab = lambda i, j: (i % npb, 0)
    tab_spec = pl.BlockSpec((tm, LANES), tab)
    return pl.pallas_call(
        functools.partial(_proj_a_kernel, n_idx_heads=n_idx_heads, idx_dim=idx_dim),
        grid=(m // tm, 5),
        in_specs=[
            pl.BlockSpec((tm, d), row),
            pl.BlockSpec((1, d), lambda i, j: (0, 0)),
            pl.BlockSpec((d, tn), lambda i, j: (0, j)),
            tab_spec, tab_spec, tab_spec, tab_spec,
        ],
        out_specs=[
            pl.BlockSpec((tm, wa), row), pl.BlockSpec((tm, wa), row), pl.BlockSpec((tm, wa), row),
            pl.BlockSpec((tm, wa), row), pl.BlockSpec((tm, wa), row),
            pl.BlockSpec((tm, wi), row), pl.BlockSpec((tm, LANES), row),
        ],
        out_shape=[
            jax.ShapeDtypeStruct((m, wa), BF16), jax.ShapeDtypeStruct((m, wa), F32),
            jax.ShapeDtypeStruct((m, wa), BF16), jax.ShapeDtypeStruct((m, wa), F32),
            jax.ShapeDtypeStruct((m, wa), BF16),
            jax.ShapeDtypeStruct((m, wi), F32), jax.ShapeDtypeStruct((m, LANES), F32),
        ],
        scratch_shapes=[pltpu.VMEM((tm, d), BF16)],
        compiler_params=_params("parallel", "arbitrary"),
        name="proj_a",
    )(x, g.reshape(1, d), w, *tabs)


def _proj_b_kernel(x_ref, g_ref, w_ref, qb_ref, kbf_ref, kbb_ref, vbf_ref, vbb_ref, h_scr):
    j = pl.program_id(1)

    @pl.when(j == 0)
    def _():
        h_scr[...] = _rms(x_ref[...], g_ref[...]).astype(BF16)

    z = jnp.dot(h_scr[...], w_ref[...], preferred_element_type=F32)

    @pl.when(j == 0)
    def _():
        qb_ref[...] = (z * (HEAD_DIM ** -0.5)).astype(BF16)

    @pl.when(j == 1)
    def _():
        kbf_ref[...] = z
        kbb_ref[...] = z.astype(BF16)

    @pl.when(j == 2)
    def _():
        vbf_ref[...] = z
        vbb_ref[...] = z.astype(BF16)


def _proj_b(x, g, w):
    m, d = x.shape
    tm = min(512, m)
    wb = w.shape[1] // 3
    row = lambda i, j: (i, 0)
    return pl.pallas_call(
        _proj_b_kernel,
        grid=(m // tm, 3),
        in_specs=[
            pl.BlockSpec((tm, d), row),
            pl.BlockSpec((1, d), lambda i, j: (0, 0)),
            pl.BlockSpec((d, wb), lambda i, j: (0, j)),
        ],
        out_specs=[pl.BlockSpec((tm, wb), row)] * 5,
        out_shape=[
            jax.ShapeDtypeStruct((m, wb), BF16), jax.ShapeDtypeStruct((m, wb), F32),
            jax.ShapeDtypeStruct((m, wb), BF16), jax.ShapeDtypeStruct((m, wb), F32),
            jax.ShapeDtypeStruct((m, wb), BF16),
        ],
        scratch_shapes=[pltpu.VMEM((tm, d), BF16)],
        compiler_params=_params("parallel", "arbitrary"),
        name="proj_b",
    )(x, g.reshape(1, d), w)


def _key_to_f32(key):
    bits = jnp.where(key >= 0, key, key ^ jnp.int32(0x7FFFFFFF))
    return lax.bitcast_convert_type(bits, F32)


def _dsa_core(q_ref, iq, iw, ik3_ref, k_ref, v_ref, tri_ref, s_scr, o_ref,
              limit, n_chunks, ck, n_top, n_heads, n_idx_heads, idx_dim):
    tq = iq.shape[0]

    iq3 = []
    for h in range(n_idx_heads):
        x = iq[:, h * idx_dim:(h + 1) * idx_dim]
        hi = x.astype(BF16).astype(F32)
        iq3.append(jnp.concatenate([hi, x - hi, hi], axis=1).astype(BF16))

    col0 = lax.broadcasted_iota(jnp.int32, (tq, ck), 1)

    def score_chunk(c, _):
        start = pl.multiple_of(c * ck, ck)
        ikc = ik3_ref[pl.ds(start, ck), :]
        acc = jnp.zeros((tq, ck), F32)
        for h in range(n_idx_heads):
            rel = lax.dot_general(iq3[h], ikc, NT_DIMS, preferred_element_type=F32)
            acc = acc + iw[:, h:h + 1] * jnp.maximum(rel, 0.0)
        s_scr[c] = jnp.where(col0 + c * ck < limit, acc, -jnp.inf)
        return 0

    lax.fori_loop(0, n_chunks, score_chunk, 0)

    def count(pred_fn):
        def body(c, acc):
            m = jnp.where(pred_fn(s_scr[c]), 1.0, 0.0)
            part = m[:, :LANES]
            for t in range(1, ck // LANES):
                part = part + m[:, t * LANES:(t + 1) * LANES]
            return acc + part
        acc = lax.fori_loop(0, n_chunks, body, jnp.zeros((tq, LANES), F32))
        return jnp.sum(acc, axis=1, keepdims=True)

    kf = jnp.float32(n_top)
    cnt = count(lambda s: s >= 0.0)
    key = jnp.where(cnt >= kf, jnp.int32(0), jnp.int32(INT_MIN))

    def bit_step(b, key):
        cand = key | jnp.left_shift(jnp.int32(1), 30 - b)
        cand_f = _key_to_f32(cand)
        cnt = count(lambda s: s >= cand_f)
        return jnp.where(cnt >= kf, cand, key)

    key = lax.fori_loop(0, 31, bit_step, key)
    thr = _key_to_f32(key)
    need = kf - count(lambda s: s > thr)
    take_all = limit <= n_top

    def bias_chunk(c, carry):
        s = s_scr[c]
        eq = s == thr
        pre = jnp.dot(jnp.where(eq, 1.0, 0.0).astype(BF16), tri_ref[...], preferred_element_type=F32)
        tied = jnp.where((carry + pre) <= need, 0.0, NEG)
        bias = jnp.where(eq, tied, jnp.where(s > thr, 0.0, NEG))
        s_scr[c] = jnp.where(take_all, jnp.where(s > -jnp.inf, 0.0, NEG), bias)
        return carry + pre[:, ck - 1:ck]

    lax.fori_loop(0, n_chunks, bias_chunk, jnp.zeros((tq, 1), F32))

    for h in range(n_heads):
        hs = slice(h * HEAD_DIM, (h + 1) * HEAD_DIM)
        q_h = q_ref[:, hs]

        def att_chunk(c, state):
            m, l, acc = state
            start = pl.multiple_of(c * ck, ck)
            kc = k_ref[pl.ds(start, ck), hs]
            vc = v_ref[pl.ds(start, ck), hs]
            lg = lax.dot_general(q_h, kc, NT_DIMS, preferred_element_type=F32) + s_scr[c]
            m_new = jnp.maximum(m, jnp.max(lg, axis=1, keepdims=True))
            alpha = jnp.exp(m - m_new)
            p = jnp.exp(lg - m_new)
            l = alpha * l + jnp.sum(p, axis=1, keepdims=True)
            acc = alpha * acc + jnp.dot(p.astype(BF16), vc, preferred_element_type=F32)
            return m_new, l, acc

        init = (jnp.full((tq, 1), NEG, F32), jnp.zeros((tq, 1), F32), jnp.zeros((tq, HEAD_DIM), F32))
        m, l, acc = lax.fori_loop(0, n_chunks, att_chunk, init)
        o_ref[:, hs] = (acc / l).astype(o_ref.dtype)


def _dsa_prompt_kernel(q_ref, iq_ref, ikw_ref, ik3_ref, k_ref, v_ref, tri_ref, o_ref, s_scr,
                       *, tq, ck, n_top, n_heads, n_idx_heads, idx_dim):
    i = pl.program_id(1)
    pos = i * tq + lax.broadcasted_iota(jnp.int32, (tq, 1), 0)
    limit = (pos // CHUNK + 1) * CHUNK
    n_chunks = ((i + 1) * tq + ck - 1) // ck
    iw = ikw_ref[:, idx_dim:idx_dim + n_idx_heads]
    _dsa_core(q_ref, iq_ref[...], iw, ik3_ref, k_ref, v_ref, tri_ref, s_scr, o_ref,
              limit, n_chunks, ck, n_top, n_heads, n_idx_heads, idx_dim)


def _tri_incl(n):
    r = jnp.arange(n)
    return (r[:, None] <= r[None, :]).astype(BF16)


def _dsa_prompt(qa, iq, ikw, ik3, ka, va, n_heads, n_idx_heads, idx_dim):
    b, s, wa = qa.shape
    tq = 128
    ck = min(512, s)
    n_top = min(TOPK_MAX, s // 4)
    qblk = lambda bb, i: (bb, i, 0)
    full = lambda bb, i: (bb, 0, 0)
    return pl.pallas_call(
        functools.partial(_dsa_prompt_kernel, tq=tq, ck=ck, n_top=n_top, n_heads=n_heads,
                          n_idx_heads=n_idx_heads, idx_dim=idx_dim),
        grid=(b, s // tq),
        in_specs=[
            pl.BlockSpec((None, tq, wa), qblk),
            pl.BlockSpec((None, tq, iq.shape[2]), qblk),
            pl.BlockSpec((None, tq, LANES), qblk),
            pl.BlockSpec((None, s, ik3.shape[2]), full),
            pl.BlockSpec((None, s, wa), full),
            pl.BlockSpec((None, s, wa), full),
            pl.BlockSpec((ck, ck), lambda bb, i: (0, 0)),
        ],
        out_specs=pl.BlockSpec((None, tq, wa), qblk),
        out_shape=jax.ShapeDtypeStruct((b, s, wa), BF16),
        scratch_shapes=[pltpu.VMEM((s // ck, tq, ck), F32)],
        compiler_params=_params("parallel", "arbitrary"),
        name="dsa_prompt",
    )(qa, iq, ikw, ik3, ka, va, _tri_incl(ck))


def _sb_core(q_ref, k_ref, v_ref, m2_ref, o_ref, diag, n_heads, tk):
    tq = q_ref.shape[0]
    row = lax.broadcasted_iota(jnp.int32, (tq, tk), 0)
    col = lax.broadcasted_iota(jnp.int32, (tq, tk), 1)
    for h in range(n_heads):
        hs = slice(h * HEAD_DIM, (h + 1) * HEAD_DIM)
        q_h = q_ref[:, hs]

        def tile(step, state):
            carry, acc = state
            start = pl.multiple_of((diag - step) * tk, tk)
            kt = k_ref[pl.ds(start, tk), hs]
            vt = v_ref[pl.ds(start, tk), hs]
            z = lax.dot_general(q_h, kt, NT_DIMS, preferred_element_type=F32)
            sp = jnp.maximum(z, 0.0) + jnp.log1p(jnp.exp(-jnp.abs(z)))
            vis = col < row + jnp.where(step > 0, tk, 0)
            lk = jnp.where(vis, -sp, 0.0)
            hi, lo = _split_bf16(lk)
            after = jnp.dot(jnp.concatenate([hi, lo], axis=1), m2_ref[...], preferred_element_type=F32)
            a = jnp.where(vis, jnp.exp(z - sp + after + carry), 0.0)
            acc = acc + jnp.dot(a.astype(BF16), vt, preferred_element_type=F32)
            carry = carry + jnp.sum(lk, axis=1, keepdims=True)
            return carry, acc

        init = (jnp.zeros((tq, 1), F32), jnp.zeros((tq, HEAD_DIM), F32))
        _, acc = lax.fori_loop(0, diag + 1, tile, init)
        o_ref[:, hs] = acc.astype(o_ref.dtype)


def _tri_after(n):
    r = jnp.arange(n)
    m = (r[:, None] > r[None, :]).astype(BF16)
    return jnp.concatenate([m, m], axis=0)


def _sb_prompt_kernel(q_ref, k_ref, v_ref, m2_ref, o_ref, *, n_heads, tk):
    _sb_core(q_ref, k_ref, v_ref, m2_ref, o_ref, pl.program_id(1), n_heads, tk)


def _sb_prompt(qb, kb, vb, n_heads):
    b, s, wb = qb.shape
    tq = tk = 128
    qblk = lambda bb, i: (bb, i, 0)
    full = lambda bb, i: (bb, 0, 0)
    return pl.pallas_call(
        functools.partial(_sb_prompt_kernel, n_heads=n_heads, tk=tk),
        grid=(b, s // tq),
        in_specs=[
            pl.BlockSpec((None, tq, wb), qblk),
            pl.BlockSpec((None, s, wb), full),
            pl.BlockSpec((None, s, wb), full),
            pl.BlockSpec((2 * tk, tk), lambda bb, i: (0, 0)),
        ],
        out_specs=pl.BlockSpec((None, tq, wb), qblk),
        out_shape=jax.ShapeDtypeStruct((b, s, wb), BF16),
        compiler_params=_params("parallel", "arbitrary"),
        name="sb_prompt",
    )(qb, kb, vb, _tri_after(tk))


def _stage_kv(cache_ref, new_ref, buf, past, t, n_heads):
    for h in range(n_heads):
        buf[0:past, h * HEAD_DIM:(h + 1) * HEAD_DIM] = cache_ref[:, h, :].astype(BF16)
    buf[past:past + t, :] = new_ref[...]
    buf[past + t:, :] = jnp.zeros((buf.shape[0] - past - t, buf.shape[1]), BF16)


def _dsa_sample_kernel(q_ref, iq_ref, ikw_ref, kn_ref, vn_ref, ck_ref, cv_ref, cik_ref, tri_ref,
                       o_ref, kbuf, vbuf, ikbuf, s_scr,
                       *, past, t, ck, n_top, n_heads, n_idx_heads, idx_dim):
    _stage_kv(ck_ref, kn_ref, kbuf, past, t, n_heads)
    _stage_kv(cv_ref, vn_ref, vbuf, past, t, n_heads)
    hi, lo = _split_bf16(cik_ref[...])
    ikbuf[0:past, :] = jnp.concatenate([hi, hi, lo], axis=1)
    hi, lo = _split_bf16(ikw_ref[:, 0:idx_dim])
    ikbuf[past:past + t, :] = jnp.concatenate([hi, hi, lo], axis=1)
    ikbuf[past + t:, :] = jnp.zeros((ikbuf.shape[0] - past - t, ikbuf.shape[1]), BF16)
    limit = jnp.full((t, 1), past + t, jnp.int32)
    iw = ikw_ref[:, idx_dim:idx_dim + n_idx_heads]
    _dsa_core(q_ref, iq_ref[...], iw, ikbuf, kbuf, vbuf, tri_ref, s_scr, o_ref,
              limit, kbuf.shape[0] // ck, ck, n_top, n_heads, n_idx_heads, idx_dim)


def _dsa_sample(qa, iq, ikw, ka, va, cache_k, cache_v, cache_ik, n_idx_heads):
    b, t, wa = qa.shape
    past, n_heads = cache_k.shape[1], cache_k.shape[2]
    idx_dim = cache_ik.shape[2]
    assert (past // CHUNK + 1) * CHUNK >= past + t, "new frames must sit in one open chunk"
    ck = 128
    lp = past + ck
    n_top = min(TOPK_MAX, (past + t) // 4)
    row = lambda bb: (bb, 0, 0)
    cache = lambda bb: (bb, 0, 0, 0)
    return pl.pallas_call(
        functools.partial(_dsa_sample_kernel, past=past, t=t, ck=ck, n_top=n_top, n_heads=n_heads,
                          n_idx_heads=n_idx_heads, idx_dim=idx_dim),
        grid=(b,),
        in_specs=[
            pl.BlockSpec((None, t, wa), row),
            pl.BlockSpec((None, t, iq.shape[2]), row),
            pl.BlockSpec((None, t, LANES), row),
            pl.BlockSpec((None, t, wa), row),
            pl.BlockSpec((None, t, wa), row),
            pl.BlockSpec((None, past, n_heads, HEAD_DIM), cache),
            pl.BlockSpec((None, past, n_heads, HEAD_DIM), cache),
            pl.BlockSpec((None, past, idx_dim), row),
            pl.BlockSpec((ck, ck), lambda bb: (0, 0)),
        ],
        out_specs=pl.BlockSpec((None, t, wa), row),
        out_shape=jax.ShapeDtypeStruct((b, t, wa), BF16),
        scratch_shapes=[
            pltpu.VMEM((lp, wa), BF16), pltpu.VMEM((lp, wa), BF16),
            pltpu.VMEM((lp, 3 * idx_dim), BF16), pltpu.VMEM((lp // ck, t, ck), F32),
        ],
        compiler_params=_params("parallel"),
        name="dsa_sample",
    )(qa, iq, ikw, ka, va, cache_k, cache_v, cache_ik, _tri_incl(ck))


def _sb_sample_kernel(q_ref, kn_ref, vn_ref, ck_ref, cv_ref, m2_ref, o_ref, kbuf, vbuf,
                      *, past, t, tk, n_heads):
    _stage_kv(ck_ref, kn_ref, kbuf, past, t, n_heads)
    _stage_kv(cv_ref, vn_ref, vbuf, past, t, n_heads)
    _sb_core(q_ref, kbuf, vbuf, m2_ref, o_ref, past // tk, n_heads, tk)


def _sb_sample(qb, kb, vb, cache_k, cache_v):
    b, t, wb = qb.shape
    past, n_heads = cache_k.shape[1], cache_k.shape[2]
    tk = 128
    lp = past + tk
    row = lambda bb: (bb, 0, 0)
    cache = lambda bb: (bb, 0, 0, 0)
    return pl.pallas_call(
        functools.partial(_sb_sample_kernel, past=past, t=t, tk=tk, n_heads=n_heads),
        grid=(b,),
        in_specs=[
            pl.BlockSpec((None, t, wb), row),
            pl.BlockSpec((None, t, wb), row),
            pl.BlockSpec((None, t, wb), row),
            pl.BlockSpec((None, past, n_heads, HEAD_DIM), cache),
            pl.BlockSpec((None, past, n_heads, HEAD_DIM), cache),
            pl.BlockSpec((2 * tk, tk), lambda bb: (0, 0)),
        ],
        out_specs=pl.BlockSpec((None, t, wb), row),
        out_shape=jax.ShapeDtypeStruct((b, t, wb), BF16),
        scratch_shapes=[pltpu.VMEM((lp, wb), BF16), pltpu.VMEM((lp, wb), BF16)],
        compiler_params=_params("parallel"),
        name="sb_sample",
    )(qb, kb, vb, cache_k, cache_v, _tri_after(tk))


def _out_proj_kernel(y_ref, oa_ref, ob_ref, wa_ref, wb_ref, o_ref):
    acc = jnp.dot(oa_ref[...], wa_ref[...], preferred_element_type=F32)
    acc = acc + jnp.dot(ob_ref[...], wb_ref[...], preferred_element_type=F32)
    o_ref[...] = y_ref[...] + acc


def _out_proj(y, oa, ob, w_a, w_b):
    m, d = y.shape
    tm = min(512, m)
    row = lambda i: (i, 0)
    const = lambda i: (0, 0)
    return pl.pallas_call(
        _out_proj_kernel,
        grid=(m // tm,),
        in_specs=[
            pl.BlockSpec((tm, d), row),
            pl.BlockSpec((tm, oa.shape[1]), row),
            pl.BlockSpec((tm, ob.shape[1]), row),
            pl.BlockSpec(w_a.shape, const),
            pl.BlockSpec(w_b.shape, const),
        ],
        out_specs=pl.BlockSpec((tm, d), row),
        out_shape=jax.ShapeDtypeStruct((m, d), F32),
        compiler_params=_params("parallel"),
        name="out_proj",
    )(y, oa, ob, w_a, w_b)


def _gelu(x):
    return 0.5 * x * (1.0 + jnp.tanh(math.sqrt(2.0 / math.pi) * (x + 0.044715 * (x * x * x))))


def _c_v_kernel(x_ref, g_ref, w_ref, vg_ref, *outs):
    h = _rms(x_ref[...], g_ref[...]).astype(BF16)
    v = _gelu(jnp.dot(h, w_ref[...], preferred_element_type=F32))
    vn = _rms(v, vg_ref[...])
    outs[0][...] = vn.astype(BF16)
    if len(outs) > 1:
        outs[1][...] = vn


def _c_v(x, g, w_v, v_gain, want_f32):
    m, d = x.shape
    cw = w_v.shape[1]
    tm = min(512, m)
    row = lambda i: (i, 0)
    const = lambda i: (0, 0)
    out_specs = [pl.BlockSpec((tm, cw), row)]
    out_shape = [jax.ShapeDtypeStruct((m, cw), BF16)]
    if want_f32:
        out_specs.append(pl.BlockSpec((tm, cw), row))
        out_shape.append(jax.ShapeDtypeStruct((m, cw), F32))
    return pl.pallas_call(
        _c_v_kernel,
        grid=(m // tm,),
        in_specs=[
            pl.BlockSpec((tm, d), row),
            pl.BlockSpec((1, d), const),
            pl.BlockSpec((d, cw), const),
            pl.BlockSpec((1, cw), const),
        ],
        out_specs=out_specs,
        out_shape=out_shape,
        compiler_params=_params("parallel"),
        name="c_v",
    )(x, g.reshape(1, d), w_v, v_gain.reshape(1, cw))


def _c_mix_kernel(x_ref, g_ref, wu_ref, vn_ref, wm_ref, bias_ref, wo_ref, o_ref, h_scr, acc_scr,
                  *, nj, gs):
    j = pl.program_id(1)

    @pl.when(j == 0)
    def _():
        h_scr[...] = _rms(x_ref[...], g_ref[...]).astype(BF16)
        acc_scr[...] = jnp.zeros_like(acc_scr)

    u = _gelu(jnp.dot(h_scr[...], wu_ref[...], preferred_element_type=F32))
    tm = u.shape[0]
    for gg in range(gs):
        cs = slice(gg * LANES, (gg + 1) * LANES)
        wm = wm_ref[j * gs + gg]
        for r in range(tm // C_CHUNK):
            rs = slice(r * C_CHUNK, (r + 1) * C_CHUNK)
            mix = jnp.dot(wm, vn_ref[rs, cs], preferred_element_type=F32) + bias_ref[:, cs]
            p = (u[rs, cs] * mix).astype(BF16)
            acc_scr[rs, :] += jnp.dot(p, wo_ref[cs, :], preferred_element_type=F32)

    @pl.when(j == nj - 1)
    def _():
        o_ref[...] = x_ref[...] + acc_scr[...]


def _c_mix(x, g, w_u, vn, w_m, bias, w_o, gs=4):
    m, d = x.shape
    cw = w_u.shape[1]
    ng = cw // LANES
    gs = min(gs, ng)
    nj = ng // gs
    tm = min(512, m)
    row = lambda i, j: (i, 0)
    return pl.pallas_call(
        functools.partial(_c_mix_kernel, nj=nj, gs=gs),
        grid=(m // tm, nj),
        in_specs=[
            pl.BlockSpec((tm, d), row),
            pl.BlockSpec((1, d), lambda i, j: (0, 0)),
            pl.BlockSpec((d, gs * LANES), lambda i, j: (0, j)),
            pl.BlockSpec((tm, gs * LANES), lambda i, j: (i, j)),
            pl.BlockSpec(w_m.shape, lambda i, j: (0, 0, 0)),
            pl.BlockSpec((C_CHUNK, gs * LANES), lambda i, j: (0, j)),
            pl.BlockSpec((gs * LANES, d), lambda i, j: (j, 0)),
        ],
        out_specs=pl.BlockSpec((tm, d), row),
        out_shape=jax.ShapeDtypeStruct((m, d), F32),
        scratch_shapes=[pltpu.VMEM((tm, d), BF16), pltpu.VMEM((tm, d), F32)],
        compiler_params=_params("parallel", "arbitrary"),
        name="c_mix",
    )(x, g.reshape(1, d), w_u, vn, w_m, bias, w_o)


def _rope_tables(pos, reps):
    pos = pos.astype(F32)[:, None]

    def tab(half, copies):
        inv = ROPE_THETA ** (-jnp.arange(half, dtype=F32) / half)
        ang = pos * inv[None, :]
        c, s = jnp.cos(ang), jnp.sin(ang)
        return jnp.tile(jnp.concatenate([c, c], axis=1), (reps, copies)), \
            jnp.tile(jnp.concatenate([-s, s], axis=1), (reps, copies))

    c128, s128 = tab(HEAD_DIM // 2, 1)
    c64, s64 = tab(32, 2)
    return c128, s128, c64, s64


def kernel(x_prompt, x_sample, cache_a_k, cache_a_v, cache_a_ik, cache_b_k, cache_b_v, norm_ff1, ff1_w1, ff1_w3, ff1_w2, norm_mix, norm_ff2, ff2_w1, ff2_w3, ff2_w2, ab_w_in, ab_w_out, c_w_in, c_v_norm, c_w_s, c_b_s, c_w_out, final_norm):
    bp, seq, d = x_prompt.shape
    bs, t, _ = x_sample.shape
    past, ha = cache_a_k.shape[2], cache_a_k.shape[3]
    hb = cache_b_k.shape[3]
    idx_dim = cache_a_ik.shape[3]
    wa, wb = ha * HEAD_DIM, hb * HEAD_DIM
    n_idx_heads = (ab_w_in.shape[2] - 3 * wa - 3 * wb - idx_dim) // (idx_dim + 1)
    wi = n_idx_heads * idx_dim
    assert idx_dim == 64 and wi % LANES == 0 and wi <= wa and n_idx_heads <= LANES - idx_dim
    depth = norm_ff1.shape[0]
    mp, ms = bp * seq, bs * t

    yp = x_prompt.reshape(mp, d)
    ys = x_sample.reshape(ms, d)
    bf = lambda w: w.astype(BF16)

    tabs_p = _rope_tables(jnp.arange(seq), 1)
    tabs_s = _rope_tables(past + jnp.arange(t), min(512, ms) // t)

    outs_p, outs_s, s_cv = [], [], []
    for layer in range(depth):
        j = layer // 2
        w1, w3, w2 = bf(ff1_w1[layer]), bf(ff1_w3[layer]), bf(ff1_w2[layer])
        yp = _half_ffn(yp, norm_ff1[layer], w1, w3, w2)
        ys = _half_ffn(ys, norm_ff1[layer], w1, w3, w2)
        if layer % 2 == 0:
            w_in = ab_w_in[j]
            o = 0
            cols = []
            for width in (wa, wa, wa, wi, idx_dim, n_idx_heads, wb, wb, wb):
                cols.append(w_in[:, o:o + width])
                o += width
            w_qa, w_ka, w_va, w_iq, w_ik, w_iw, w_qb, w_kb, w_vb = cols
            pad = lambda w: jnp.pad(w, ((0, 0), (0, wa - w.shape[1])))
            w_pa = bf(jnp.concatenate(
                [w_qa, w_ka, w_va, pad(w_iq), pad(jnp.concatenate([w_ik, w_iw], axis=1))], axis=1))
            w_pb = bf(jnp.concatenate([w_qb, w_kb, w_vb], axis=1))
            w_oa, w_ob = bf(ab_w_out[j][:wa]), bf(ab_w_out[j][wa:])

            def mixer(y, tabs, n_pos_rows):
                qa, kaf, kab, vaf, vab, iq, ikw = _proj_a(
                    y, norm_mix[layer], w_pa, tabs, n_pos_rows, ha, n_idx_heads, idx_dim)
                qb, kbf, kbb, vbf, vbb = _proj_b(y, norm_mix[layer], w_pb)
                return (qa, kaf, kab, vaf, vab, iq, ikw, qb, kbf, kbb, vbf, vbb)

            qa, kaf, kab, vaf, vab, iq, ikw, qb, kbf, kbb, vbf, vbb = mixer(yp, tabs_p, seq)
            r3 = lambda a: a.reshape(bp, seq, a.shape[-1])
            ik = ikw[:, :idx_dim]
            ik_hi, ik_lo = _split_bf16(ik)
            ik3 = jnp.concatenate([ik_hi, ik_hi, ik_lo], axis=1)
            o_a = _dsa_prompt(r3(qa), r3(iq), r3(ikw), r3(ik3), r3(kab), r3(vab), ha, n_idx_heads, idx_dim)
            o_b = _sb_prompt(r3(qb), r3(kbb), r3(vbb), hb)
            yp = _out_proj(yp, o_a.reshape(mp, wa), o_b.reshape(mp, wb), w_oa, w_ob)
            outs_p.append((kaf.reshape(bp, seq, ha, HEAD_DIM), vaf.reshape(bp, seq, ha, HEAD_DIM),
                           ik.reshape(bp, seq, idx_dim),
                           kbf.reshape(bp, seq, hb, HEAD_DIM), vbf.reshape(bp, seq, hb, HEAD_DIM)))

            qa, kaf, kab, vaf, vab, iq, ikw, qb, kbf, kbb, vbf, vbb = mixer(ys, tabs_s, min(512, ms))
            r3 = lambda a: a.reshape(bs, t, a.shape[-1])
            o_a = _dsa_sample(r3(qa), r3(iq), r3(ikw), r3(kab), r3(vab),
                              cache_a_k[j], cache_a_v[j], cache_a_ik[j], n_idx_heads)
            o_b = _sb_sample(r3(qb), r3(kbb), r3(vbb), cache_b_k[j], cache_b_v[j])
            ys = _out_proj(ys, o_a.reshape(ms, wa), o_b.reshape(ms, wb), w_oa, w_ob)
            outs_s.append((kaf.reshape(bs, t, ha, HEAD_DIM), vaf.reshape(bs, t, ha, HEAD_DIM),
                           ikw[:, :idx_dim].reshape(bs, t, idx_dim),
                           kbf.reshape(bs, t, hb, HEAD_DIM), vbf.reshape(bs, t, hb, HEAD_DIM)))
        else:
            cw = c_w_in.shape[2] // 2
            ng = c_w_s.shape[1]
            w_u, w_v = bf(c_w_in[j][:, :cw]), bf(c_w_in[j][:, cw:])
            w_o = bf(c_w_out[j])
            i = jnp.arange(C_CHUNK)
            mask = (i[None, :] // CHUNK) <= (i[:, None] // CHUNK)
            w_m = jnp.where(mask[None], c_w_s[j], 0.0)
            bias_p = jnp.repeat(c_b_s[j].T, cw // ng, axis=1)
            per = C_CHUNK // t
            w_ms = jnp.einsum('ab,gij->gaibj', jnp.eye(per, dtype=F32), w_m[:, :t, :t]).reshape(ng, C_CHUNK, C_CHUNK)
            bias_s = jnp.tile(bias_p[:t], (per, 1))

            vn = _c_v(yp, norm_mix[layer], w_v, c_v_norm[j], False)[0]
            yp = _c_mix(yp, norm_mix[layer], w_u, vn, bf(w_m), bias_p, w_o)
            vn, vn_f32 = _c_v(ys, norm_mix[layer], w_v, c_v_norm[j], True)
            ys = _c_mix(ys, norm_mix[layer], w_u, vn, bf(w_ms), bias_s, w_o)
            s_cv.append(vn_f32.reshape(bs, t, cw))
        last = layer == depth - 1
        w1, w3, w2 = bf(ff2_w1[layer]), bf(ff2_w3[layer]), bf(ff2_w2[layer])
        fg = final_norm if last else None
        yp = _half_ffn(yp, norm_ff2[layer], w1, w3, w2, fg)
        ys = _half_ffn(ys, norm_ff2[layer], w1, w3, w2, fg)

    stack = lambda outs, k: jnp.stack([o[k] for o in outs])
    return (yp.reshape(bp, seq, d), ys.reshape(bs, t, d),
            stack(outs_p, 0), stack(outs_p, 1), stack(outs_p, 2), stack(outs_p, 3), stack(outs_p, 4),
            stack(outs_s, 0), stack(outs_s, 1), stack(outs_s, 2), stack(outs_s, 3), stack(outs_s, 4),
            jnp.stack(s_cv))
```

```python
import functools
import math

import jax
import jax.numpy as jnp
from jax import lax
from jax.experimental import pallas as pl
from jax.experimental.pallas import tpu as pltpu

F32 = jnp.float32
BF16 = jnp.bfloat16

RMS_EPS = 1e-6
CHUNK = 64
TOPK_MAX = 256
ROPE_THETA = 10000.0
C_CHUNK = 128
LANES = 128
HEAD_DIM = 128
NEG = -1e30
INT_MIN = -2147483648
SB_DEAD = -105.0
DSA_CHUNK = 256
VMEM_LIMIT_BYTES = 56 * 1024 * 1024

NT_DIMS = (((1,), (1,)), ((), ()))


def _params(*sem):
    return pltpu.CompilerParams(dimension_semantics=sem, vmem_limit_bytes=VMEM_LIMIT_BYTES)


def _rms(x, g):
    return x * lax.rsqrt(jnp.mean(x * x, axis=-1, keepdims=True) + RMS_EPS) * g


def _split_bf16(x):
    hi = x.astype(BF16)
    lo = (x - hi.astype(F32)).astype(BF16)
    return hi, lo


def _ffn_kernel(x_ref, g_ref, w1_ref, w3_ref, w2_ref, *rest, nf, final_norm):
    if final_norm:
        gf_ref, o_ref, h_scr, acc_scr = rest
    else:
        o_ref, h_scr, acc_scr = rest
    f = pl.program_id(1)

    @pl.when(f == 0)
    def _():
        h_scr[...] = _rms(x_ref[...], g_ref[...]).astype(BF16)
        acc_scr[...] = jnp.zeros_like(acc_scr)

    h = h_scr[...]
    a = jnp.dot(h, w1_ref[...], preferred_element_type=F32)
    b = jnp.dot(h, w3_ref[...], preferred_element_type=F32)
    p = (a * jax.nn.sigmoid(a) * b).astype(BF16)
    acc_scr[...] += jnp.dot(p, w2_ref[...], preferred_element_type=F32)

    @pl.when(f == nf - 1)
    def _():
        y = x_ref[...] + 0.5 * acc_scr[...]
        if final_norm:
            y = _rms(y, gf_ref[...])
        o_ref[...] = y


def _half_ffn(x, g, w1, w3, w2, final_g=None, tf=512):
    m, d = x.shape
    dff = w1.shape[1]
    tm = min(512, m)
    nf = dff // tf
    in_specs = [
        pl.BlockSpec((tm, d), lambda i, f: (i, 0)),
        pl.BlockSpec((1, d), lambda i, f: (0, 0)),
        pl.BlockSpec((d, tf), lambda i, f: (0, f)),
        pl.BlockSpec((d, tf), lambda i, f: (0, f)),
        pl.BlockSpec((tf, d), lambda i, f: (f, 0)),
    ]
    args = [x, g.reshape(1, d), w1, w3, w2]
    if final_g is not None:
        in_specs.append(pl.BlockSpec((1, d), lambda i, f: (0, 0)))
        args.append(final_g.reshape(1, d))
    return pl.pallas_call(
        functools.partial(_ffn_kernel, nf=nf, final_norm=final_g is not None),
        grid=(m // tm, nf),
        in_specs=in_specs,
        out_specs=pl.BlockSpec((tm, d), lambda i, f: (i, 0)),
        out_shape=jax.ShapeDtypeStruct((m, d), F32),
        scratch_shapes=[pltpu.VMEM((tm, d), BF16), pltpu.VMEM((tm, d), F32)],
        compiler_params=_params("parallel", "arbitrary"),
        name="half_ffn",
    )(*args)


def _rope_heads(z, cos, sin):
    outs = []
    for h in range(z.shape[1] // LANES):
        zh = z[:, h * LANES:(h + 1) * LANES]
        outs.append(zh * cos + pltpu.roll(zh, LANES // 2, axis=1) * sin)
    return outs


def _rope_pairs(z, cos, sin):
    lane = lax.broadcasted_iota(jnp.int32, (z.shape[0], LANES), 1)
    low = (lane % 64) < 32
    outs = []
    for h in range(z.shape[1] // LANES):
        zh = z[:, h * LANES:(h + 1) * LANES]
        partner = jnp.where(low, pltpu.roll(zh, LANES - 32, axis=1), pltpu.roll(zh, 32, axis=1))
        outs.append(zh * cos + partner * sin)
    return outs


def _proj_a_kernel(x_ref, g_ref, w_ref, wkw_ref, c128_ref, s128_ref, c64_ref, s64_ref,
                   qa_ref, kaf_ref, kab_ref, vaf_ref, vab_ref, iq_ref, ikw_ref, h_scr,
                   *, n_idx_heads, idx_dim):
    j = pl.program_id(1)

    @pl.when(j == 0)
    def _():
        h_scr[...] = _rms(x_ref[...], g_ref[...]).astype(BF16)

    z = jnp.dot(h_scr[...], w_ref[...], preferred_element_type=F32)
    wa = qa_ref.shape[1]

    @pl.when(j == 0)
    def _():
        scale = HEAD_DIM ** -0.5
        for h, r in enumerate(_rope_heads(z[:, :wa], c128_ref[...], s128_ref[...])):
            qa_ref[:, h * LANES:(h + 1) * LANES] = (r * scale).astype(BF16)

    @pl.when(j == 1)
    def _():
        for h, r in enumerate(_rope_heads(z[:, :wa], c128_ref[...], s128_ref[...])):
            kaf_ref[:, h * LANES:(h + 1) * LANES] = r
            kab_ref[:, h * LANES:(h + 1) * LANES] = r.astype(BF16)

    @pl.when(j == 2)
    def _():
        kv = z[:, :wa]
        vaf_ref[...] = kv
        vab_ref[...] = kv.astype(BF16)

    @pl.when(j == 3)
    def _():
        wi = iq_ref.shape[1]
        for h, r in enumerate(_rope_pairs(z[:, :wi], c64_ref[...], s64_ref[...])):
            iq_ref[:, h * LANES:(h + 1) * LANES] = r * (idx_dim ** -0.5)
        zz = jnp.dot(h_scr[...], wkw_ref[...], preferred_element_type=F32)
        r = _rope_pairs(zz, c64_ref[...], s64_ref[...])[0]
        lane = lax.broadcasted_iota(jnp.int32, zz.shape, 1)
        ikw_ref[...] = jnp.where(lane < idx_dim, r, zz * (n_idx_heads ** -0.5))


def _proj_a(x, g, w, w_kw, tabs, n_pos_rows, ha, n_idx_heads, idx_dim):
    m, d = x.shape
    tm = min(512, m)
    wa = ha * HEAD_DIM
    wi = n_idx_heads * idx_dim
    tn = w.shape[1] // 4
    npb = n_pos_rows // tm
    row = lambda i, j: (i, 0)
    tab = lambda i, j: (i % npb, 0)
    tab_spec = pl.BlockSpec((tm, LANES), tab)
    return pl.pallas_call(
        functools.partial(_proj_a_kernel, n_idx_heads=n_idx_heads, idx_dim=idx_dim),
        grid=(m // tm, 4),
        in_specs=[
            pl.BlockSpec((tm, d), row),
            pl.BlockSpec((1, d), lambda i, j: (0, 0)),
            pl.BlockSpec((d, tn), lambda i, j: (0, j)),
            pl.BlockSpec((d, LANES), lambda i, j: (0, 0)),
            tab_spec, tab_spec, tab_spec, tab_spec,
        ],
        out_specs=[
            pl.BlockSpec((tm, wa), row), pl.BlockSpec((tm, wa), row), pl.BlockSpec((tm, wa), row),
            pl.BlockSpec((tm, wa), row), pl.BlockSpec((tm, wa), row),
            pl.BlockSpec((tm, wi), row), pl.BlockSpec((tm, LANES), row),
        ],
        out_shape=[
            jax.ShapeDtypeStruct((m, wa), BF16), jax.ShapeDtypeStruct((m, wa), F32),
            jax.ShapeDtypeStruct((m, wa), BF16), jax.ShapeDtypeStruct((m, wa), F32),
            jax.ShapeDtypeStruct((m, wa), BF16),
            jax.ShapeDtypeStruct((m, wi), F32), jax.ShapeDtypeStruct((m, LANES), F32),
        ],
        scratch_shapes=[pltpu.VMEM((tm, d), BF16)],
        compiler_params=_params("parallel", "arbitrary"),
        name="proj_a",
    )(x, g.reshape(1, d), w, w_kw, *tabs)


def _proj_b_kernel(x_ref, g_ref, w_ref, qb_ref, kbf_ref, kbb_ref, vbf_ref, vbb_ref, h_scr):
    j = pl.program_id(1)

    @pl.when(j == 0)
    def _():
        h_scr[...] = _rms(x_ref[...], g_ref[...]).astype(BF16)

    z = jnp.dot(h_scr[...], w_ref[...], preferred_element_type=F32)

    @pl.when(j == 0)
    def _():
        qb_ref[...] = (z * (HEAD_DIM ** -0.5)).astype(BF16)

    @pl.when(j == 1)
    def _():
        kbf_ref[...] = z
        kbb_ref[...] = z.astype(BF16)

    @pl.when(j == 2)
    def _():
        vbf_ref[...] = z
        vbb_ref[...] = z.astype(BF16)


def _proj_b(x, g, w):
    m, d = x.shape
    tm = min(512, m)
    wb = w.shape[1] // 3
    row = lambda i, j: (i, 0)
    return pl.pallas_call(
        _proj_b_kernel,
        grid=(m // tm, 3),
        in_specs=[
            pl.BlockSpec((tm, d), row),
            pl.BlockSpec((1, d), lambda i, j: (0, 0)),
            pl.BlockSpec((d, wb), lambda i, j: (0, j)),
        ],
        out_specs=[pl.BlockSpec((tm, wb), row)] * 5,
        out_shape=[
            jax.ShapeDtypeStruct((m, wb), BF16), jax.ShapeDtypeStruct((m, wb), F32),
            jax.ShapeDtypeStruct((m, wb), BF16), jax.ShapeDtypeStruct((m, wb), F32),
            jax.ShapeDtypeStruct((m, wb), BF16),
        ],
        scratch_shapes=[pltpu.VMEM((tm, d), BF16)],
        compiler_params=_params("parallel", "arbitrary"),
        name="proj_b",
    )(x, g.reshape(1, d), w)


def _key_to_f32(key):
    bits = jnp.where(key >= 0, key, key ^ jnp.int32(0x7FFFFFFF))
    return lax.bitcast_convert_type(bits, F32)


def _fold_lanes(x, op):
    out = x[:, :LANES]
    for t in range(1, x.shape[1] // LANES):
        out = op(out, x[:, t * LANES:(t + 1) * LANES])
    return out


def _dsa_core(q_ref, iq, iw, main, tail, tri_ref, s_scr, iq3_scr, acc_scr, l_scr, o_ref,
              limit, n_main, ck, n_top, n_heads, n_idx_heads, idx_dim):
    tq = iq.shape[0]
    n_chunks = n_main + (0 if tail is None else 1)

    for h in range(n_idx_heads):
        x = iq[:, h * idx_dim:(h + 1) * idx_dim]
        hi = x.astype(BF16).astype(F32)
        iq3_scr[h * tq:(h + 1) * tq, :] = jnp.concatenate([hi, x - hi, hi], axis=1).astype(BF16)

    col0 = lax.broadcasted_iota(jnp.int32, (tq, ck), 1)

    def score_chunk(c, ikc):
        rel = lax.dot_general(iq3_scr[...], ikc, NT_DIMS, preferred_element_type=F32)
        acc = iw[:, 0:1] * jnp.maximum(rel[0:tq], 0.0)
        for h in range(1, n_idx_heads):
            acc = acc + iw[:, h:h + 1] * jnp.maximum(rel[h * tq:(h + 1) * tq], 0.0)
        s_scr[c] = jnp.where(col0 + c * ck < limit, acc, -jnp.inf)

    def score_main(c, _):
        score_chunk(c, main[0](c))
        return 0

    lax.fori_loop(0, n_main, score_main, 0)
    if tail is not None:
        score_chunk(n_main, tail[0]())

    def count(pred_fn):
        def body(c, acc):
            return acc + _fold_lanes(jnp.where(pred_fn(s_scr[c]), 1.0, 0.0), jnp.add)
        acc = lax.fori_loop(0, n_chunks, body, jnp.zeros((tq, LANES), F32))
        return jnp.sum(acc, axis=1, keepdims=True)

    kf = jnp.float32(n_top)
    cnt = count(lambda s: s >= 0.0)
    key = jnp.where(cnt >= kf, jnp.int32(0), jnp.int32(INT_MIN))

    def bit_step(b, key):
        cand = key | jnp.left_shift(jnp.int32(1), 30 - b)
        cand_f = _key_to_f32(cand)
        cnt = count(lambda s: s >= cand_f)
        return jnp.where(cnt >= kf, cand, key)

    key = lax.fori_loop(0, 31, bit_step, key)
    thr = _key_to_f32(key)
    need = kf - count(lambda s: s > thr)
    take_all = limit <= n_top

    def bias_chunk(c, carry):
        s = s_scr[c]
        eq = s == thr
        pre = jnp.dot(jnp.where(eq, 1.0, 0.0).astype(BF16), tri_ref[...], preferred_element_type=F32)
        tied = jnp.where((carry + pre) <= need, 0.0, NEG)
        bias = jnp.where(eq, tied, jnp.where(s > thr, 0.0, NEG))
        s_scr[c] = jnp.where(take_all, jnp.where(s > -jnp.inf, 0.0, NEG), bias)
        return carry + pre[:, ck - 1:ck]

    lax.fori_loop(0, n_chunks, bias_chunk, jnp.zeros((tq, 1), F32))

    heads = [slice(h * HEAD_DIM, (h + 1) * HEAD_DIM) for h in range(n_heads)]
    acc_scr[...] = jnp.zeros_like(acc_scr)
    l_scr[...] = jnp.zeros_like(l_scr)

    def att_chunk(c, kv, ms):
        bias = s_scr[c]
        new_m = []
        for h, hs in enumerate(heads):
            kc, vc = kv(h)
            lg = lax.dot_general(q_ref[:, hs], kc, NT_DIMS, preferred_element_type=F32) + bias
            m_new = jnp.maximum(ms[h], jnp.max(_fold_lanes(lg, jnp.maximum), axis=1, keepdims=True))
            alpha = jnp.exp(ms[h] - m_new)
            p = jnp.exp(lg - m_new)
            l_scr[:, hs] = alpha * l_scr[:, hs] + _fold_lanes(p, jnp.add)
            acc_scr[:, hs] = alpha * acc_scr[:, hs] + jnp.dot(p.astype(BF16), vc, preferred_element_type=F32)
            new_m.append(m_new)
        return tuple(new_m)

    ms = tuple(jnp.full((tq, 1), NEG, F32) for _ in heads)
    ms = lax.fori_loop(0, n_main, lambda c, ms: att_chunk(c, functools.partial(main[1], c), ms), ms)
    if tail is not None:
        att_chunk(n_main, tail[1], ms)
    for hs in heads:
        o_ref[:, hs] = (acc_scr[:, hs] / jnp.sum(l_scr[:, hs], axis=1, keepdims=True)).astype(o_ref.dtype)


def _dsa_prompt_kernel(q_ref, iq_ref, ikw_ref, ik3_ref, k_ref, v_ref, tri_ref, o_ref,
                       s_scr, iq3_scr, acc_scr, l_scr, *, tq, ck, n_top, n_heads, n_idx_heads, idx_dim):
    i = pl.program_id(1)
    pos = i * tq + lax.broadcasted_iota(jnp.int32, (tq, 1), 0)
    limit = (pos // CHUNK + 1) * CHUNK
    n_chunks = ((i + 1) * tq + ck - 1) // ck
    iw = ikw_ref[:, idx_dim:idx_dim + n_idx_heads]

    def rows(c):
        return pl.ds(pl.multiple_of(c * ck, ck), ck)

    def kv(c, h):
        cols = slice(h * HEAD_DIM, (h + 1) * HEAD_DIM)
        return k_ref[rows(c), cols], v_ref[rows(c), cols]

    main = (lambda c: ik3_ref[rows(c), :], kv)
    _dsa_core(q_ref, iq_ref[...], iw, main, None, tri_ref, s_scr, iq3_scr, acc_scr, l_scr, o_ref,
              limit, n_chunks, ck, n_top, n_heads, n_idx_heads, idx_dim)


def _tri_incl(n):
    r = jnp.arange(n)
    return (r[:, None] <= r[None, :]).astype(BF16)


def _dsa_prompt(qa, iq, ikw, ik3, ka, va, n_heads, n_idx_heads, idx_dim):
    b, s, wa = qa.shape
    tq = 128
    ck = DSA_CHUNK
    n_top = min(TOPK_MAX, s // 4)
    qblk = lambda bb, i: (bb, i, 0)
    full = lambda bb, i: (bb, 0, 0)
    return pl.pallas_call(
        functools.partial(_dsa_prompt_kernel, tq=tq, ck=ck, n_top=n_top, n_heads=n_heads,
                          n_idx_heads=n_idx_heads, idx_dim=idx_dim),
        grid=(b, s // tq),
        in_specs=[
            pl.BlockSpec((None, tq, wa), qblk),
            pl.BlockSpec((None, tq, iq.shape[2]), qblk),
            pl.BlockSpec((None, tq, LANES), qblk),
            pl.BlockSpec((None, s, ik3.shape[2]), full),
            pl.BlockSpec((None, s, wa), full),
            pl.BlockSpec((None, s, wa), full),
            pl.BlockSpec((ck, ck), lambda bb, i: (0, 0)),
        ],
        out_specs=pl.BlockSpec((None, tq, wa), qblk),
        out_shape=jax.ShapeDtypeStruct((b, s, wa), BF16),
        scratch_shapes=[
            pltpu.VMEM((s // ck, tq, ck), F32), pltpu.VMEM((n_idx_heads * tq, 3 * idx_dim), BF16),
            pltpu.VMEM((tq, wa), F32), pltpu.VMEM((tq, wa), F32),
        ],
        compiler_params=_params("parallel", "arbitrary"),
        name="dsa_prompt",
    )(qa, iq, ikw, ik3, ka, va, _tri_incl(ck))


def _sb_tile(q_h, kt, vt, m2, carry, vis):
    z = lax.dot_general(q_h, kt, NT_DIMS, preferred_element_type=F32)
    sp = jnp.maximum(z, 0.0) + jnp.log1p(jnp.exp(-jnp.abs(z)))
    lk = -sp if vis is None else jnp.where(vis, -sp, 0.0)
    hi, lo = _split_bf16(lk)
    after = jnp.dot(jnp.concatenate([hi, lo], axis=1), m2, preferred_element_type=F32)
    a = jnp.exp(z - sp + after + carry)
    if vis is not None:
        a = jnp.where(vis, a, 0.0)
    return jnp.sum(lk, axis=1, keepdims=True), jnp.dot(a.astype(BF16), vt, preferred_element_type=F32)


def _sb_core(q_ref, diag_kv, past_kv, m2_ref, o_ref, acc_scr, n_past, n_heads, tk):
    tq = q_ref.shape[0]
    vis = lax.broadcasted_iota(jnp.int32, (tq, tk), 1) < lax.broadcasted_iota(jnp.int32, (tq, tk), 0)
    m2 = m2_ref[...]
    heads = [slice(h * HEAD_DIM, (h + 1) * HEAD_DIM) for h in range(n_heads)]

    carries = []
    for h, hs in enumerate(heads):
        kt, vt = diag_kv(h)
        dc, contrib = _sb_tile(q_ref[:, hs], kt, vt, m2, jnp.zeros((tq, 1), F32), vis)
        acc_scr[:, hs] = contrib
        carries.append(dc)

    def alive(cs):
        m = cs[0]
        for c in cs[1:]:
            m = jnp.maximum(m, c)
        return jnp.max(m)

    def cond(state):
        step, top, _ = state
        return jnp.logical_and(step < n_past, top > SB_DEAD)

    def body(state):
        step, _, cs = state
        j = n_past - 1 - step
        new = []
        for h, hs in enumerate(heads):
            kt, vt = past_kv(j, h)
            dc, contrib = _sb_tile(q_ref[:, hs], kt, vt, m2, cs[h], None)
            acc_scr[:, hs] += contrib
            new.append(cs[h] + dc)
        return step + 1, alive(new), tuple(new)

    lax.while_loop(cond, body, (jnp.int32(0), alive(carries), tuple(carries)))
    o_ref[...] = acc_scr[...].astype(o_ref.dtype)


def _tri_after(n):
    r = jnp.arange(n)
    m = (r[:, None] > r[None, :]).astype(BF16)
    return jnp.concatenate([m, m], axis=0)


def _sb_prompt_kernel(q_ref, k_ref, v_ref, m2_ref, o_ref, acc_scr, *, n_heads, tk):
    i = pl.program_id(1)

    def tile(j, h):
        rows = pl.ds(pl.multiple_of(j * tk, tk), tk)
        cols = slice(h * HEAD_DIM, (h + 1) * HEAD_DIM)
        return k_ref[rows, cols], v_ref[rows, cols]

    _sb_core(q_ref, lambda h: tile(i, h), tile, m2_ref, o_ref, acc_scr, i, n_heads, tk)


def _sb_prompt(qb, kb, vb, n_heads):
    b, s, wb = qb.shape
    tq = tk = 128
    qblk = lambda bb, i: (bb, i, 0)
    full = lambda bb, i: (bb, 0, 0)
    return pl.pallas_call(
        functools.partial(_sb_prompt_kernel, n_heads=n_heads, tk=tk),
        grid=(b, s // tq),
        in_specs=[
            pl.BlockSpec((None, tq, wb), qblk),
            pl.BlockSpec((None, s, wb), full),
            pl.BlockSpec((None, s, wb), full),
            pl.BlockSpec((2 * tk, tk), lambda bb, i: (0, 0)),
        ],
        out_specs=pl.BlockSpec((None, tq, wb), qblk),
        out_shape=jax.ShapeDtypeStruct((b, s, wb), BF16),
        scratch_shapes=[pltpu.VMEM((tq, wb), F32)],
        compiler_params=_params("parallel", "arbitrary"),
        name="sb_prompt",
    )(qb, kb, vb, _tri_after(tk))


def _pad_rows(new_ref, buf):
    t = new_ref.shape[0]
    buf[0:t, :] = new_ref[...].astype(buf.dtype)
    buf[t:, :] = jnp.zeros((buf.shape[0] - t, buf.shape[1]), buf.dtype)


def _ik3(x):
    hi, lo = _split_bf16(x)
    return jnp.concatenate([hi, hi, lo], axis=1)


def _dsa_sample_kernel(q_ref, iq_ref, ikw_ref, kn_ref, vn_ref, ck_ref, cv_ref, cik_ref, tri_ref,
                       o_ref, knew, vnew, iknew, s_scr, iq3_scr, acc_scr, l_scr,
                       *, past, t, ck, n_top, n_heads, n_idx_heads, idx_dim):
    _pad_rows(kn_ref, knew)
    _pad_rows(vn_ref, vnew)
    iknew[0:t, :] = ikw_ref[:, 0:idx_dim]
    iknew[t:, :] = jnp.zeros((ck - t, idx_dim), F32)
    limit = jnp.full((t, 1), past + t, jnp.int32)
    iw = ikw_ref[:, idx_dim:idx_dim + n_idx_heads]

    def rows(c):
        return pl.ds(pl.multiple_of(c * ck, ck), ck)

    def cache_kv(c, h):
        cols = slice(h * HEAD_DIM, (h + 1) * HEAD_DIM)
        return ck_ref[rows(c), cols], cv_ref[rows(c), cols]

    def new_kv(h):
        cols = slice(h * HEAD_DIM, (h + 1) * HEAD_DIM)
        return knew[:, cols], vnew[:, cols]

    main = (lambda c: _ik3(cik_ref[rows(c), :]), cache_kv)
    tail = (lambda: _ik3(iknew[...]), new_kv)
    _dsa_core(q_ref, iq_ref[...], iw, main, tail, tri_ref, s_scr, iq3_scr, acc_scr, l_scr, o_ref,
              limit, past // ck, ck, n_top, n_heads, n_idx_heads, idx_dim)


def _dsa_sample(qa, iq, ikw, ka, va, cache_k, cache_v, cache_ik, n_idx_heads):
    b, t, wa = qa.shape
    past, n_heads = cache_k.shape[1], wa // HEAD_DIM
    idx_dim = cache_ik.shape[2]
    assert (past // CHUNK + 1) * CHUNK >= past + t, "new frames must sit in one open chunk"
    ck = DSA_CHUNK
    assert past % ck == 0 and t <= ck
    n_top = min(TOPK_MAX, (past + t) // 4)
    row = lambda bb: (bb, 0, 0)
    return pl.pallas_call(
        functools.partial(_dsa_sample_kernel, past=past, t=t, ck=ck, n_top=n_top, n_heads=n_heads,
                          n_idx_heads=n_idx_heads, idx_dim=idx_dim),
        grid=(b,),
        in_specs=[
            pl.BlockSpec((None, t, wa), row),
            pl.BlockSpec((None, t, iq.shape[2]), row),
            pl.BlockSpec((None, t, LANES), row),
            pl.BlockSpec((None, t, wa), row),
            pl.BlockSpec((None, t, wa), row),
            pl.BlockSpec((None, past, wa), row),
            pl.BlockSpec((None, past, wa), row),
            pl.BlockSpec((None, past, idx_dim), row),
            pl.BlockSpec((ck, ck), lambda bb: (0, 0)),
        ],
        out_specs=pl.BlockSpec((None, t, wa), row),
        out_shape=jax.ShapeDtypeStruct((b, t, wa), BF16),
        scratch_shapes=[
            pltpu.VMEM((ck, wa), BF16), pltpu.VMEM((ck, wa), BF16), pltpu.VMEM((ck, idx_dim), F32),
            pltpu.VMEM((past // ck + 1, t, ck), F32), pltpu.VMEM((n_idx_heads * t, 3 * idx_dim), BF16),
            pltpu.VMEM((t, wa), F32), pltpu.VMEM((t, wa), F32),
        ],
        compiler_params=_params("parallel"),
        name="dsa_sample",
    )(qa, iq, ikw, ka, va, cache_k, cache_v, cache_ik, _tri_incl(ck))


def _sb_sample_kernel(q_ref, kn_ref, vn_ref, ck_ref, cv_ref, m2_ref, o_ref, knew, vnew, acc_scr,
                      *, past, t, tk, n_heads):
    for new_ref, buf in ((kn_ref, knew), (vn_ref, vnew)):
        buf[0:t, :] = new_ref[...]
        buf[t:, :] = jnp.zeros((tk - t, buf.shape[1]), BF16)

    def new_kv(h):
        cols = slice(h * HEAD_DIM, (h + 1) * HEAD_DIM)
        return knew[:, cols], vnew[:, cols]

    def cache_kv(j, h):
        rows = pl.ds(pl.multiple_of(j * tk, tk), tk)
        return ck_ref[rows, h, :].astype(BF16), cv_ref[rows, h, :].astype(BF16)

    _sb_core(q_ref, new_kv, cache_kv, m2_ref, o_ref, acc_scr, past // tk, n_heads, tk)


def _sb_sample(qb, kb, vb, cache_k, cache_v):
    b, t, wb = qb.shape
    past, n_heads = cache_k.shape[1], cache_k.shape[2]
    tk = 128
    assert past % tk == 0 and t <= tk
    row = lambda bb: (bb, 0, 0)
    cache = lambda bb: (bb, 0, 0, 0)
    return pl.pallas_call(
        functools.partial(_sb_sample_kernel, past=past, t=t, tk=tk, n_heads=n_heads),
        grid=(b,),
        in_specs=[
            pl.BlockSpec((None, t, wb), row),
            pl.BlockSpec((None, t, wb), row),
            pl.BlockSpec((None, t, wb), row),
            pl.BlockSpec((None, past, n_heads, HEAD_DIM), cache),
            pl.BlockSpec((None, past, n_heads, HEAD_DIM), cache),
            pl.BlockSpec((2 * tk, tk), lambda bb: (0, 0)),
        ],
        out_specs=pl.BlockSpec((None, t, wb), row),
        out_shape=jax.ShapeDtypeStruct((b, t, wb), BF16),
        scratch_shapes=[pltpu.VMEM((tk, wb), BF16), pltpu.VMEM((tk, wb), BF16), pltpu.VMEM((t, wb), F32)],
        compiler_params=_params("parallel"),
        name="sb_sample",
    )(qb, kb, vb, cache_k, cache_v, _tri_after(tk))


def _out_proj_kernel(y_ref, oa_ref, ob_ref, wa_ref, wb_ref, o_ref):
    acc = jnp.dot(oa_ref[...], wa_ref[...], preferred_element_type=F32)
    acc = acc + jnp.dot(ob_ref[...], wb_ref[...], preferred_element_type=F32)
    o_ref[...] = y_ref[...] + acc


def _out_proj(y, oa, ob, w_a, w_b):
    m, d = y.shape
    tm = min(512, m)
    row = lambda i: (i, 0)
    const = lambda i: (0, 0)
    return pl.pallas_call(
        _out_proj_kernel,
        grid=(m // tm,),
        in_specs=[
            pl.BlockSpec((tm, d), row),
            pl.BlockSpec((tm, oa.shape[1]), row),
            pl.BlockSpec((tm, ob.shape[1]), row),
            pl.BlockSpec(w_a.shape, const),
            pl.BlockSpec(w_b.shape, const),
        ],
        out_specs=pl.BlockSpec((tm, d), row),
        out_shape=jax.ShapeDtypeStruct((m, d), F32),
        compiler_params=_params("parallel"),
        name="out_proj",
    )(y, oa, ob, w_a, w_b)


def _gelu(x):
    return 0.5 * x * (1.0 + jnp.tanh(math.sqrt(2.0 / math.pi) * (x + 0.044715 * (x * x * x))))


def _c_v_kernel(x_ref, g_ref, w_ref, vg_ref, *outs):
    h = _rms(x_ref[...], g_ref[...]).astype(BF16)
    v = _gelu(jnp.dot(h, w_ref[...], preferred_element_type=F32))
    vn = _rms(v, vg_ref[...])
    outs[0][...] = vn.astype(BF16)
    if len(outs) > 1:
        outs[1][...] = vn


def _c_v(x, g, w_v, v_gain, want_f32):
    m, d = x.shape
    cw = w_v.shape[1]
    tm = min(512, m)
    row = lambda i: (i, 0)
    const = lambda i: (0, 0)
    out_specs = [pl.BlockSpec((tm, cw), row)]
    out_shape = [jax.ShapeDtypeStruct((m, cw), BF16)]
    if want_f32:
        out_specs.append(pl.BlockSpec((tm, cw), row))
        out_shape.append(jax.ShapeDtypeStruct((m, cw), F32))
    return pl.pallas_call(
        _c_v_kernel,
        grid=(m // tm,),
        in_specs=[
            pl.BlockSpec((tm, d), row),
            pl.BlockSpec((1, d), const),
            pl.BlockSpec((d, cw), const),
            pl.BlockSpec((1, cw), const),
        ],
        out_specs=out_specs,
        out_shape=out_shape,
        compiler_params=_params("parallel"),
        name="c_v",
    )(x, g.reshape(1, d), w_v, v_gain.reshape(1, cw))


def _c_mix_kernel(x_ref, g_ref, wu_ref, vn_ref, wm_ref, bias_ref, wo_ref, o_ref, h_scr, acc_scr, p_scr,
                  *, nj, gs):
    j = pl.program_id(1)

    @pl.when(j == 0)
    def _():
        h_scr[...] = _rms(x_ref[...], g_ref[...]).astype(BF16)
        acc_scr[...] = jnp.zeros_like(acc_scr)

    u = _gelu(jnp.dot(h_scr[...], wu_ref[...], preferred_element_type=F32))
    tm = u.shape[0]
    for gg in range(gs):
        cs = slice(gg * LANES, (gg + 1) * LANES)
        wm = wm_ref[j * gs + gg]
        for r in range(tm // C_CHUNK):
            rs = slice(r * C_CHUNK, (r + 1) * C_CHUNK)
            mix = jnp.dot(wm, vn_ref[rs, cs], preferred_element_type=F32) + bias_ref[:, cs]
            p_scr[rs, cs] = (u[rs, cs] * mix).astype(BF16)
    acc_scr[...] += jnp.dot(p_scr[...], wo_ref[...], preferred_element_type=F32)

    @pl.when(j == nj - 1)
    def _():
        o_ref[...] = x_ref[...] + acc_scr[...]


def _c_mix(x, g, w_u, vn, w_m, bias, w_o, gs=4):
    m, d = x.shape
    cw = w_u.shape[1]
    ng = cw // LANES
    gs = min(gs, ng)
    nj = ng // gs
    tm = min(512, m)
    row = lambda i, j: (i, 0)
    return pl.pallas_call(
        functools.partial(_c_mix_kernel, nj=nj, gs=gs),
        grid=(m // tm, nj),
        in_specs=[
            pl.BlockSpec((tm, d), row),
            pl.BlockSpec((1, d), lambda i, j: (0, 0)),
            pl.BlockSpec((d, gs * LANES), lambda i, j: (0, j)),
            pl.BlockSpec((tm, gs * LANES), lambda i, j: (i, j)),
            pl.BlockSpec(w_m.shape, lambda i, j: (0, 0, 0)),
            pl.BlockSpec((C_CHUNK, gs * LANES), lambda i, j: (0, j)),
            pl.BlockSpec((gs * LANES, d), lambda i, j: (j, 0)),
        ],
        out_specs=pl.BlockSpec((tm, d), row),
        out_shape=jax.ShapeDtypeStruct((m, d), F32),
        scratch_shapes=[pltpu.VMEM((tm, d), BF16), pltpu.VMEM((tm, d), F32),
                        pltpu.VMEM((tm, gs * LANES), BF16)],
        compiler_params=_params("parallel", "arbitrary"),
        name="c_mix",
    )(x, g.reshape(1, d), w_u, vn, w_m, bias, w_o)


def _rope_tables(pos, reps):
    pos = pos.astype(F32)[:, None]

    def tab(half, copies):
        inv = ROPE_THETA ** (-jnp.arange(half, dtype=F32) / half)
        ang = pos * inv[None, :]
        c, s = jnp.cos(ang), jnp.sin(ang)
        return jnp.tile(jnp.concatenate([c, c], axis=1), (reps, copies)), \
            jnp.tile(jnp.concatenate([-s, s], axis=1), (reps, copies))

    c128, s128 = tab(HEAD_DIM // 2, 1)
    c64, s64 = tab(32, 2)
    return c128, s128, c64, s64


def kernel(x_prompt, x_sample, cache_a_k, cache_a_v, cache_a_ik, cache_b_k, cache_b_v, norm_ff1, ff1_w1, ff1_w3, ff1_w2, norm_mix, norm_ff2, ff2_w1, ff2_w3, ff2_w2, ab_w_in, ab_w_out, c_w_in, c_v_norm, c_w_s, c_b_s, c_w_out, final_norm):
    bp, seq, d = x_prompt.shape
    bs, t, _ = x_sample.shape
    past, ha = cache_a_k.shape[2], cache_a_k.shape[3]
    hb = cache_b_k.shape[3]
    idx_dim = cache_a_ik.shape[3]
    wa, wb = ha * HEAD_DIM, hb * HEAD_DIM
    n_idx_heads = (ab_w_in.shape[2] - 3 * wa - 3 * wb - idx_dim) // (idx_dim + 1)
    wi = n_idx_heads * idx_dim
    assert idx_dim == 64 and wi % LANES == 0 and wi <= wa and n_idx_heads <= LANES - idx_dim
    depth = norm_ff1.shape[0]
    mp, ms = bp * seq, bs * t

    yp = x_prompt.reshape(mp, d)
    ys = x_sample.reshape(ms, d)
    bf = lambda w: w.astype(BF16)

    tabs_p = _rope_tables(jnp.arange(seq), 1)
    tabs_s = _rope_tables(past + jnp.arange(t), min(512, ms) // t)

    outs_p, outs_s, s_cv = [], [], []
    for layer in range(depth):
        j = layer // 2
        w1, w3, w2 = bf(ff1_w1[layer]), bf(ff1_w3[layer]), bf(ff1_w2[layer])
        yp = _half_ffn(yp, norm_ff1[layer], w1, w3, w2)
        ys = _half_ffn(ys, norm_ff1[layer], w1, w3, w2)
        if layer % 2 == 0:
            w_in = ab_w_in[j]
            o = 0
            cols = []
            for width in (wa, wa, wa, wi, idx_dim, n_idx_heads, wb, wb, wb):
                cols.append(w_in[:, o:o + width])
                o += width
            w_qa, w_ka, w_va, w_iq, w_ik, w_iw, w_qb, w_kb, w_vb = cols
            pad = lambda w, width: jnp.pad(w, ((0, 0), (0, width - w.shape[1])))
            w_pa = bf(jnp.concatenate([w_qa, w_ka, w_va, pad(w_iq, wa)], axis=1))
            w_kw = bf(pad(jnp.concatenate([w_ik, w_iw], axis=1), LANES))
            w_pb = bf(jnp.concatenate([w_qb, w_kb, w_vb], axis=1))
            w_oa, w_ob = bf(ab_w_out[j][:wa]), bf(ab_w_out[j][wa:])

            def mixer(y, tabs, n_pos_rows):
                qa, kaf, kab, vaf, vab, iq, ikw = _proj_a(
                    y, norm_mix[layer], w_pa, w_kw, tabs, n_pos_rows, ha, n_idx_heads, idx_dim)
                qb, kbf, kbb, vbf, vbb = _proj_b(y, norm_mix[layer], w_pb)
                return (qa, kaf, kab, vaf, vab, iq, ikw, qb, kbf, kbb, vbf, vbb)

            qa, kaf, kab, vaf, vab, iq, ikw, qb, kbf, kbb, vbf, vbb = mixer(yp, tabs_p, seq)
            r3 = lambda a: a.reshape(bp, seq, a.shape[-1])
            ik = ikw[:, :idx_dim]
            ik_hi, ik_lo = _split_bf16(ik)
            ik3 = jnp.concatenate([ik_hi, ik_hi, ik_lo], axis=1)
            o_a = _dsa_prompt(r3(qa), r3(iq), r3(ikw), r3(ik3), r3(kab), r3(vab), ha, n_idx_heads, idx_dim)
            o_b = _sb_prompt(r3(qb), r3(kbb), r3(vbb), hb)
            yp = _out_proj(yp, o_a.reshape(mp, wa), o_b.reshape(mp, wb), w_oa, w_ob)
            outs_p.append((kaf.reshape(bp, seq, ha, HEAD_DIM), vaf.reshape(bp, seq, ha, HEAD_DIM),
                           ik.reshape(bp, seq, idx_dim),
                           kbf.reshape(bp, seq, hb, HEAD_DIM), vbf.reshape(bp, seq, hb, HEAD_DIM)))

            qa, kaf, kab, vaf, vab, iq, ikw, qb, kbf, kbb, vbf, vbb = mixer(ys, tabs_s, min(512, ms))
            r3 = lambda a: a.reshape(bs, t, a.shape[-1])
            lanes_bf = lambda c: c.reshape(bs, past, wa).astype(BF16)
            o_a = _dsa_sample(r3(qa), r3(iq), r3(ikw), r3(kab), r3(vab),
                              lanes_bf(cache_a_k[j]), lanes_bf(cache_a_v[j]), cache_a_ik[j], n_idx_heads)
            o_b = _sb_sample(r3(qb), r3(kbb), r3(vbb), cache_b_k[j], cache_b_v[j])
            ys = _out_proj(ys, o_a.reshape(ms, wa), o_b.reshape(ms, wb), w_oa, w_ob)
            outs_s.append((kaf.reshape(bs, t, ha, HEAD_DIM), vaf.reshape(bs, t, ha, HEAD_DIM),
                           ikw[:, :idx_dim].reshape(bs, t, idx_dim),
                           kbf.reshape(bs, t, hb, HEAD_DIM), vbf.reshape(bs, t, hb, HEAD_DIM)))
        else:
            cw = c_w_in.shape[2] // 2
            ng = c_w_s.shape[1]
            w_u, w_v = bf(c_w_in[j][:, :cw]), bf(c_w_in[j][:, cw:])
            w_o = bf(c_w_out[j])
            i = jnp.arange(C_CHUNK)
            mask = (i[None, :] // CHUNK) <= (i[:, None] // CHUNK)
            w_m = jnp.where(mask[None], c_w_s[j], 0.0)
            bias_p = jnp.repeat(c_b_s[j].T, cw // ng, axis=1)
            per = C_CHUNK // t
            w_ms = jnp.einsum('ab,gij->gaibj', jnp.eye(per, dtype=F32), w_m[:, :t, :t]).reshape(ng, C_CHUNK, C_CHUNK)
            bias_s = jnp.tile(bias_p[:t], (per, 1))

            vn = _c_v(yp, norm_mix[layer], w_v, c_v_norm[j], False)[0]
            yp = _c_mix(yp, norm_mix[layer], w_u, vn, bf(w_m), bias_p, w_o)
            vn, vn_f32 = _c_v(ys, norm_mix[layer], w_v, c_v_norm[j], True)
            ys = _c_mix(ys, norm_mix[layer], w_u, vn, bf(w_ms), bias_s, w_o)
            s_cv.append(vn_f32.reshape(bs, t, cw))
        last = layer == depth - 1
        w1, w3, w2 = bf(ff2_w1[layer]), bf(ff2_w3[layer]), bf(ff2_w2[layer])
        fg = final_norm if last else None
        yp = _half_ffn(yp, norm_ff2[layer], w1, w3, w2, fg)
        ys = _half_ffn(ys, norm_ff2[layer], w1, w3, w2, fg)

    stack = lambda outs, k: jnp.stack([o[k] for o in outs])
    return (yp.reshape(bp, seq, d), ys.reshape(bs, t, d),
            stack(outs_p, 0), stack(outs_p, 1), stack(outs_p, 2), stack(outs_p, 3), stack(outs_p, 4),
            stack(outs_s, 0), stack(outs_s, 1), stack(outs_s, 2), stack(outs_s, 3), stack(outs_s, 4),
            jnp.stack(s_cv))
```

```python
import functools
import math

import jax
import jax.numpy as jnp
from jax import lax
from jax.experimental import pallas as pl
from jax.experimental.pallas import tpu as pltpu

F32 = jnp.float32
BF16 = jnp.bfloat16

RMS_EPS = 1e-6
CHUNK = 64
TOPK_MAX = 256
ROPE_THETA = 10000.0
C_CHUNK = 128
LANES = 128
HEAD_DIM = 128
NEG = -1e30
INT_MIN = -2147483648
SB_DEAD = -105.0
DSA_CHUNK = 256
VMEM_LIMIT_BYTES = 56 * 1024 * 1024

NT_DIMS = (((1,), (1,)), ((), ()))


def _params(*sem):
    return pltpu.CompilerParams(dimension_semantics=sem, vmem_limit_bytes=VMEM_LIMIT_BYTES)


def _rms(x, g):
    return x * lax.rsqrt(jnp.mean(x * x, axis=-1, keepdims=True) + RMS_EPS) * g


def _split_bf16(x):
    hi = x.astype(BF16)
    lo = (x - hi.astype(F32)).astype(BF16)
    return hi, lo


def _ffn_kernel(x_ref, g_ref, w1_ref, w3_ref, w2_ref, *rest, nf, final_norm):
    if final_norm:
        gf_ref, o_ref, h_scr, acc_scr = rest
    else:
        o_ref, h_scr, acc_scr = rest
    f = pl.program_id(1)

    @pl.when(f == 0)
    def _():
        h_scr[...] = _rms(x_ref[...], g_ref[...]).astype(BF16)
        acc_scr[...] = jnp.zeros_like(acc_scr)

    h = h_scr[...]
    a = jnp.dot(h, w1_ref[...], preferred_element_type=F32)
    b = jnp.dot(h, w3_ref[...], preferred_element_type=F32)
    p = (a * jax.nn.sigmoid(a) * b).astype(BF16)
    acc_scr[...] += jnp.dot(p, w2_ref[...], preferred_element_type=F32)

    @pl.when(f == nf - 1)
    def _():
        y = x_ref[...] + 0.5 * acc_scr[...]
        if final_norm:
            y = _rms(y, gf_ref[...])
        o_ref[...] = y


def _half_ffn(x, g, w1, w3, w2, final_g=None, tf=512):
    m, d = x.shape
    dff = w1.shape[1]
    tm = min(512, m)
    nf = dff // tf
    in_specs = [
        pl.BlockSpec((tm, d), lambda i, f: (i, 0)),
        pl.BlockSpec((1, d), lambda i, f: (0, 0)),
        pl.BlockSpec((d, tf), lambda i, f: (0, f)),
        pl.BlockSpec((d, tf), lambda i, f: (0, f)),
        pl.BlockSpec((tf, d), lambda i, f: (f, 0)),
    ]
    args = [x, g.reshape(1, d), w1, w3, w2]
    if final_g is not None:
        in_specs.append(pl.BlockSpec((1, d), lambda i, f: (0, 0)))
        args.append(final_g.reshape(1, d))
    return pl.pallas_call(
        functools.partial(_ffn_kernel, nf=nf, final_norm=final_g is not None),
        grid=(m // tm, nf),
        in_specs=in_specs,
        out_specs=pl.BlockSpec((tm, d), lambda i, f: (i, 0)),
        out_shape=jax.ShapeDtypeStruct((m, d), F32),
        scratch_shapes=[pltpu.VMEM((tm, d), BF16), pltpu.VMEM((tm, d), F32)],
        compiler_params=_params("parallel", "arbitrary"),
        name="half_ffn",
    )(*args)


def _rope_heads(z, cos, sin):
    outs = []
    for h in range(z.shape[1] // LANES):
        zh = z[:, h * LANES:(h + 1) * LANES]
        outs.append(zh * cos + pltpu.roll(zh, LANES // 2, axis=1) * sin)
    return outs


def _rope_pairs(z, cos, sin):
    lane = lax.broadcasted_iota(jnp.int32, (z.shape[0], LANES), 1)
    low = (lane % 64) < 32
    outs = []
    for h in range(z.shape[1] // LANES):
        zh = z[:, h * LANES:(h + 1) * LANES]
        partner = jnp.where(low, pltpu.roll(zh, LANES - 32, axis=1), pltpu.roll(zh, 32, axis=1))
        outs.append(zh * cos + partner * sin)
    return outs


def _proj_a_kernel(x_ref, g_ref, w_ref, wkw_ref, c128_ref, s128_ref, c64_ref, s64_ref,
                   qa_ref, kaf_ref, kab_ref, vaf_ref, vab_ref, iq_ref, ikw_ref, h_scr,
                   *, n_idx_heads, idx_dim):
    j = pl.program_id(1)

    @pl.when(j == 0)
    def _():
        h_scr[...] = _rms(x_ref[...], g_ref[...]).astype(BF16)

    z = jnp.dot(h_scr[...], w_ref[...], preferred_element_type=F32)
    wa = qa_ref.shape[1]

    @pl.when(j == 0)
    def _():
        scale = HEAD_DIM ** -0.5
        for h, r in enumerate(_rope_heads(z[:, :wa], c128_ref[...], s128_ref[...])):
            qa_ref[:, h * LANES:(h + 1) * LANES] = (r * scale).astype(BF16)

    @pl.when(j == 1)
    def _():
        for h, r in enumerate(_rope_heads(z[:, :wa], c128_ref[...], s128_ref[...])):
            kaf_ref[:, h * LANES:(h + 1) * LANES] = r
            kab_ref[:, h * LANES:(h + 1) * LANES] = r.astype(BF16)

    @pl.when(j == 2)
    def _():
        kv = z[:, :wa]
        vaf_ref[...] = kv
        vab_ref[...] = kv.astype(BF16)

    @pl.when(j == 3)
    def _():
        wi = iq_ref.shape[1]
        for h, r in enumerate(_rope_pairs(z[:, :wi], c64_ref[...], s64_ref[...])):
            iq_ref[:, h * LANES:(h + 1) * LANES] = r * (idx_dim ** -0.5)
        zz = jnp.dot(h_scr[...], wkw_ref[...], preferred_element_type=F32)
        r = _rope_pairs(zz, c64_ref[...], s64_ref[...])[0]
        lane = lax.broadcasted_iota(jnp.int32, zz.shape, 1)
        ikw_ref[...] = jnp.where(lane < idx_dim, r, zz * (n_idx_heads ** -0.5))


def _proj_a(x, g, w, w_kw, tabs, n_pos_rows, ha, n_idx_heads, idx_dim):
    m, d = x.shape
    tm = min(512, m)
    wa = ha * HEAD_DIM
    wi = n_idx_heads * idx_dim
    tn = w.shape[1] // 4
    npb = n_pos_rows // tm
    row = lambda i, j: (i, 0)
    tab = lambda i, j: (i % npb, 0)
    tab_spec = pl.BlockSpec((tm, LANES), tab)
    return pl.pallas_call(
        functools.partial(_proj_a_kernel, n_idx_heads=n_idx_heads, idx_dim=idx_dim),
        grid=(m // tm, 4),
        in_specs=[
            pl.BlockSpec((tm, d), row),
            pl.BlockSpec((1, d), lambda i, j: (0, 0)),
            pl.BlockSpec((d, tn), lambda i, j: (0, j)),
            pl.BlockSpec((d, LANES), lambda i, j: (0, 0)),
            tab_spec, tab_spec, tab_spec, tab_spec,
        ],
        out_specs=[
            pl.BlockSpec((tm, wa), row), pl.BlockSpec((tm, wa), row), pl.BlockSpec((tm, wa), row),
            pl.BlockSpec((tm, wa), row), pl.BlockSpec((tm, wa), row),
            pl.BlockSpec((tm, wi), row), pl.BlockSpec((tm, LANES), row),
        ],
        out_shape=[
            jax.ShapeDtypeStruct((m, wa), BF16), jax.ShapeDtypeStruct((m, wa), F32),
            jax.ShapeDtypeStruct((m, wa), BF16), jax.ShapeDtypeStruct((m, wa), F32),
            jax.ShapeDtypeStruct((m, wa), BF16),
            jax.ShapeDtypeStruct((m, wi), F32), jax.ShapeDtypeStruct((m, LANES), F32),
        ],
        scratch_shapes=[pltpu.VMEM((tm, d), BF16)],
        compiler_params=_params("parallel", "arbitrary"),
        name="proj_a",
    )(x, g.reshape(1, d), w, w_kw, *tabs)


def _proj_b_kernel(x_ref, g_ref, w_ref, qb_ref, kbf_ref, kbb_ref, vbf_ref, vbb_ref, h_scr):
    j = pl.program_id(1)

    @pl.when(j == 0)
    def _():
        h_scr[...] = _rms(x_ref[...], g_ref[...]).astype(BF16)

    z = jnp.dot(h_scr[...], w_ref[...], preferred_element_type=F32)

    @pl.when(j == 0)
    def _():
        qb_ref[...] = (z * (HEAD_DIM ** -0.5)).astype(BF16)

    @pl.when(j == 1)
    def _():
        kbf_ref[...] = z
        kbb_ref[...] = z.astype(BF16)

    @pl.when(j == 2)
    def _():
        vbf_ref[...] = z
        vbb_ref[...] = z.astype(BF16)


def _proj_b(x, g, w):
    m, d = x.shape
    tm = min(512, m)
    wb = w.shape[1] // 3
    row = lambda i, j: (i, 0)
    return pl.pallas_call(
        _proj_b_kernel,
        grid=(m // tm, 3),
        in_specs=[
            pl.BlockSpec((tm, d), row),
            pl.BlockSpec((1, d), lambda i, j: (0, 0)),
            pl.BlockSpec((d, wb), lambda i, j: (0, j)),
        ],
        out_specs=[pl.BlockSpec((tm, wb), row)] * 5,
        out_shape=[
            jax.ShapeDtypeStruct((m, wb), BF16), jax.ShapeDtypeStruct((m, wb), F32),
            jax.ShapeDtypeStruct((m, wb), BF16), jax.ShapeDtypeStruct((m, wb), F32),
            jax.ShapeDtypeStruct((m, wb), BF16),
        ],
        scratch_shapes=[pltpu.VMEM((tm, d), BF16)],
        compiler_params=_params("parallel", "arbitrary"),
        name="proj_b",
    )(x, g.reshape(1, d), w)


def _key_to_f32(key):
    bits = jnp.where(key >= 0, key, key ^ jnp.int32(0x7FFFFFFF))
    return lax.bitcast_convert_type(bits, F32)


def _tree(parts, op):
    while len(parts) > 1:
        parts = [op(parts[k], parts[k + 1]) if k + 1 < len(parts) else parts[k] for k in range(0, len(parts), 2)]
    return parts[0]


def _fold_lanes(x, op):
    return _tree([x[:, t * LANES:(t + 1) * LANES] for t in range(x.shape[1] // LANES)], op)


def _dsa_core(q_ref, iq, iw, main, tail, tri_ref, s_scr, iq3_scr, acc_scr, l_scr, o_ref,
              limit, n_main, ck, n_top, n_heads, n_idx_heads, idx_dim):
    tq = iq.shape[0]
    n_chunks = n_main + (0 if tail is None else 1)

    for h in range(n_idx_heads):
        x = iq[:, h * idx_dim:(h + 1) * idx_dim]
        hi = x.astype(BF16).astype(F32)
        iq3_scr[h * tq:(h + 1) * tq, :] = jnp.concatenate([hi, x - hi, hi], axis=1).astype(BF16)

    col0 = lax.broadcasted_iota(jnp.int32, (tq, ck), 1)

    def score_chunk(c, ikc):
        rel = lax.dot_general(iq3_scr[...], ikc, NT_DIMS, preferred_element_type=F32)
        acc = iw[:, 0:1] * jnp.maximum(rel[0:tq], 0.0)
        for h in range(1, n_idx_heads):
            acc = acc + iw[:, h:h + 1] * jnp.maximum(rel[h * tq:(h + 1) * tq], 0.0)
        s_scr[c] = jnp.where(col0 + c * ck < limit, acc, -jnp.inf)

    def score_main(c, _):
        score_chunk(c, main[0](c))
        return 0

    lax.fori_loop(0, n_main, score_main, 0)
    if tail is not None:
        score_chunk(n_main, tail[0]())

    def count(pred_fn):
        def body(c, acc):
            return acc + _fold_lanes(jnp.where(pred_fn(s_scr[c]), 1.0, 0.0), jnp.add)
        acc = lax.fori_loop(0, n_chunks, body, jnp.zeros((tq, LANES), F32))
        return jnp.sum(acc, axis=1, keepdims=True)

    kf = jnp.float32(n_top)
    cnt = count(lambda s: s >= 0.0)
    key = jnp.where(cnt >= kf, jnp.int32(0), jnp.int32(INT_MIN))

    def bit_step(b, key):
        cand = key | jnp.left_shift(jnp.int32(1), 30 - b)
        cand_f = _key_to_f32(cand)
        cnt = count(lambda s: s >= cand_f)
        return jnp.where(cnt >= kf, cand, key)

    key = lax.fori_loop(0, 31, bit_step, key)
    thr = _key_to_f32(key)
    need = kf - count(lambda s: s > thr)
    take_all = limit <= n_top

    def bias_chunk(c, carry):
        s = s_scr[c]
        eq = s == thr
        pre = jnp.dot(jnp.where(eq, 1.0, 0.0).astype(BF16), tri_ref[...], preferred_element_type=F32)
        tied = jnp.where((carry + pre) <= need, 0.0, NEG)
        bias = jnp.where(eq, tied, jnp.where(s > thr, 0.0, NEG))
        s_scr[c] = jnp.where(take_all, jnp.where(s > -jnp.inf, 0.0, NEG), bias)
        return carry + pre[:, ck - 1:ck]

    lax.fori_loop(0, n_chunks, bias_chunk, jnp.zeros((tq, 1), F32))

    heads = [slice(h * HEAD_DIM, (h + 1) * HEAD_DIM) for h in range(n_heads)]
    acc_scr[...] = jnp.zeros_like(acc_scr)
    l_scr[...] = jnp.zeros_like(l_scr)

    def att_chunk(c, kv, ms):
        bias = s_scr[c]
        new_m = []
        for h, hs in enumerate(heads):
            kc, vc = kv(h)
            lg = lax.dot_general(q_ref[:, hs], kc, NT_DIMS, preferred_element_type=F32) + bias
            m_new = jnp.maximum(ms[h], jnp.max(_fold_lanes(lg, jnp.maximum), axis=1, keepdims=True))
            alpha = jnp.exp(ms[h] - m_new)
            p = jnp.exp(lg - m_new)
            l_scr[:, hs] = alpha * l_scr[:, hs] + _fold_lanes(p, jnp.add)
            acc_scr[:, hs] = alpha * acc_scr[:, hs] + jnp.dot(p.astype(BF16), vc, preferred_element_type=F32)
            new_m.append(m_new)
        return tuple(new_m)

    ms = tuple(jnp.full((tq, 1), NEG, F32) for _ in heads)
    ms = lax.fori_loop(0, n_main, lambda c, ms: att_chunk(c, functools.partial(main[1], c), ms), ms)
    if tail is not None:
        att_chunk(n_main, tail[1], ms)
    for hs in heads:
        o_ref[:, hs] = (acc_scr[:, hs] / jnp.sum(l_scr[:, hs], axis=1, keepdims=True)).astype(o_ref.dtype)


def _tri_incl(n):
    r = jnp.arange(n)
    return (r[:, None] <= r[None, :]).astype(BF16)


def _fold_rows(x, op):
    return _tree([x[r * 8:(r + 1) * 8] for r in range(x.shape[0] // 8)], op)


def _topk_bias_t(s_scr, n_chunks, limit, n_top, ltri_ref):
    ck, nl = s_scr.shape[1], s_scr.shape[2]

    def count(pred_fn):
        def body(c, acc):
            return acc + _fold_rows(jnp.where(pred_fn(s_scr[c]), 1.0, 0.0), jnp.add)
        acc = lax.fori_loop(0, n_chunks, body, jnp.zeros((8, nl), F32))
        return jnp.sum(acc, axis=0, keepdims=True)

    kf = jnp.float32(n_top)
    cnt = count(lambda s: s >= 0.0)
    key = jnp.where(cnt >= kf, jnp.int32(0), jnp.int32(INT_MIN))

    def bit_step(b, key):
        cand = key | jnp.left_shift(jnp.int32(1), 30 - b)
        cand_f = _key_to_f32(cand)
        cnt = count(lambda s: s >= cand_f)
        return jnp.where(cnt >= kf, cand, key)

    key = lax.fori_loop(0, 31, bit_step, key)
    thr = _key_to_f32(key)
    need = kf - count(lambda s: s > thr)
    take_all = limit <= n_top

    def bias_chunk(c, carry):
        s = s_scr[c]
        eq = s == thr
        pre = jnp.dot(ltri_ref[...], jnp.where(eq, 1.0, 0.0).astype(BF16), preferred_element_type=F32)
        tied = jnp.where((carry + pre) <= need, 0.0, NEG)
        bias = jnp.where(eq, tied, jnp.where(s > thr, 0.0, NEG))
        s_scr[c] = jnp.where(take_all, jnp.where(s > -jnp.inf, 0.0, NEG), bias)
        return carry + pre[ck - 1:ck, :]

    lax.fori_loop(0, n_chunks, bias_chunk, jnp.zeros((1, nl), F32))


def _dsa_prompt_t_kernel(qt_ref, iq3t_ref, iwt_ref, ik3_ref, k_ref, vt_ref, ltri_ref, o_ref,
                         s_scr, acc_scr, lg_a, lg_b,
                         *, tq, ck, n_top, n_heads, n_idx_heads):
    i = pl.program_id(1)
    pos = i * tq + lax.broadcasted_iota(jnp.int32, (1, tq), 1)
    limit = (pos // CHUNK + 1) * CHUNK
    n_chunks = ((i + 1) * tq + ck - 1) // ck
    row0 = lax.broadcasted_iota(jnp.int32, (ck, tq), 0)
    iwt = iwt_ref[...]

    def rows(c):
        return pl.ds(pl.multiple_of(c * ck, ck), ck)

    def score_chunk(c, _):
        ikc = ik3_ref[rows(c), :]
        acc = None
        for g in range(n_idx_heads // 2):
            rel = jnp.dot(ikc, iq3t_ref[:, 2 * g * tq:2 * (g + 1) * tq], preferred_element_type=F32)
            for u in range(2):
                h = 2 * g + u
                term = iwt[h:h + 1, :] * jnp.maximum(rel[:, u * tq:(u + 1) * tq], 0.0)
                acc = term if acc is None else acc + term
        s_scr[c] = jnp.where(row0 + c * ck < limit, acc, -jnp.inf)
        return 0

    lax.fori_loop(0, n_chunks, score_chunk, 0)
    _topk_bias_t(s_scr, n_chunks, limit, n_top, ltri_ref)

    heads = [slice(h * HEAD_DIM, (h + 1) * HEAD_DIM) for h in range(n_heads)]
    acc_scr[...] = jnp.zeros_like(acc_scr)

    @pl.when(n_chunks % 2 == 1)
    def _():
        s_scr[n_chunks] = jnp.full((ck, tq), NEG, F32)

    def logits(c, buf):
        for h, hs in enumerate(heads):
            buf[h] = jnp.dot(k_ref[rows(c), hs], qt_ref[hs, :], preferred_element_type=F32)

    def attend(c, buf, ms, ls):
        bias = s_scr[c]
        new_m, new_l = [], []
        for h, hs in enumerate(heads):
            lg = buf[h] + bias
            m_new = jnp.maximum(ms[h], jnp.max(_fold_rows(lg, jnp.maximum), axis=0, keepdims=True))
            alpha = jnp.exp(ms[h] - m_new)
            p = jnp.exp(lg - m_new)
            new_l.append(alpha * ls[h] + _fold_rows(p, jnp.add))
            acc_scr[hs, :] = alpha * acc_scr[hs, :] + jnp.dot(vt_ref[c, hs, :], p.astype(BF16),
                                                              preferred_element_type=F32)
            new_m.append(m_new)
        return tuple(new_m), tuple(new_l)

    n_pairs = (n_chunks + 1) // 2
    last = 2 * n_pairs - 1

    def att_pair(pr, state):
        c0 = 2 * pr
        logits(c0 + 1, lg_b)
        state = attend(c0, lg_a, *state)
        logits(jnp.minimum(c0 + 2, last), lg_a)
        return attend(c0 + 1, lg_b, *state)

    logits(0, lg_a)
    init = (tuple(jnp.full((1, tq), NEG, F32) for _ in heads), tuple(jnp.zeros((8, tq), F32) for _ in heads))
    _, ls = lax.fori_loop(0, n_pairs, att_pair, init)
    for h, hs in enumerate(heads):
        o_t = acc_scr[hs, :] / jnp.sum(ls[h], axis=0, keepdims=True)
        o_ref[:, hs] = o_t.T.astype(o_ref.dtype)


def _dsa_prompt_t(qa, iq, ikw, ka, va, n_heads, n_idx_heads, idx_dim):
    b, s, wa = qa.shape
    tq = 128
    ck = DSA_CHUNK
    nq = s // tq
    n_top = min(TOPK_MAX, s // 4)
    assert n_idx_heads % 2 == 0 and s % (2 * ck) == 0

    qt = qa.transpose(0, 2, 1)
    x = iq.reshape(b, nq, tq, n_idx_heads, idx_dim)
    hi = x.astype(BF16)
    lo = (x - hi.astype(F32)).astype(BF16)
    iq3t = jnp.concatenate([hi, lo, hi], axis=-1).transpose(0, 1, 4, 3, 2)
    iq3t = iq3t.reshape(b, nq, 3 * idx_dim, n_idx_heads * tq)
    iwt = ikw[:, :, idx_dim:idx_dim + n_idx_heads].reshape(b, nq, tq, n_idx_heads).transpose(0, 1, 3, 2)
    ik_hi, ik_lo = _split_bf16(ikw[:, :, :idx_dim])
    ik3 = jnp.concatenate([ik_hi, ik_hi, ik_lo], axis=-1)
    vt = va.reshape(b, s // ck, ck, wa).transpose(0, 1, 3, 2)
    r = jnp.arange(ck)
    ltri = (r[None, :] <= r[:, None]).astype(BF16)

    return pl.pallas_call(
        functools.partial(_dsa_prompt_t_kernel, tq=tq, ck=ck, n_top=n_top, n_heads=n_heads,
                          n_idx_heads=n_idx_heads),
        grid=(b, nq),
        in_specs=[
            pl.BlockSpec((None, wa, tq), lambda bb, i: (bb, 0, i)),
            pl.BlockSpec((None, None, 3 * idx_dim, n_idx_heads * tq), lambda bb, i: (bb, i, 0, 0)),
            pl.BlockSpec((None, None, n_idx_heads, tq), lambda bb, i: (bb, i, 0, 0)),
            pl.BlockSpec((None, s, 3 * idx_dim), lambda bb, i: (bb, 0, 0)),
            pl.BlockSpec((None, s, wa), lambda bb, i: (bb, 0, 0)),
            pl.BlockSpec((None, s // ck, wa, ck), lambda bb, i: (bb, 0, 0, 0)),
            pl.BlockSpec((ck, ck), lambda bb, i: (0, 0)),
        ],
        out_specs=pl.BlockSpec((None, tq, wa), lambda bb, i: (bb, i, 0)),
        out_shape=jax.ShapeDtypeStruct((b, s, wa), BF16),
        scratch_shapes=[pltpu.VMEM((s // ck, ck, tq), F32), pltpu.VMEM((wa, tq), F32),
                        pltpu.VMEM((n_heads, ck, tq), F32), pltpu.VMEM((n_heads, ck, tq), F32)],
        compiler_params=_params("parallel", "arbitrary"),
        name="dsa_prompt",
    )(qt, iq3t, iwt, ik3, ka, vt, ltri)


def _sb_tile(q_h, kt, vt, m2, carry, vis):
    z = lax.dot_general(q_h, kt, NT_DIMS, preferred_element_type=F32)
    sp = jnp.maximum(z, 0.0) + jnp.log1p(jnp.exp(-jnp.abs(z)))
    lk = -sp if vis is None else jnp.where(vis, -sp, 0.0)
    hi, lo = _split_bf16(lk)
    after = jnp.dot(jnp.concatenate([hi, lo], axis=1), m2, preferred_element_type=F32)
    a = jnp.exp(z - sp + after + carry)
    if vis is not None:
        a = jnp.where(vis, a, 0.0)
    return jnp.sum(lk, axis=1, keepdims=True), jnp.dot(a.astype(BF16), vt, preferred_element_type=F32)


def _sb_core(q_ref, diag_kv, past_kv, m2_ref, o_ref, acc_scr, n_past, n_heads, tk):
    tq = q_ref.shape[0]
    vis = lax.broadcasted_iota(jnp.int32, (tq, tk), 1) < lax.broadcasted_iota(jnp.int32, (tq, tk), 0)
    m2 = m2_ref[...]
    heads = [slice(h * HEAD_DIM, (h + 1) * HEAD_DIM) for h in range(n_heads)]

    carries = []
    for h, hs in enumerate(heads):
        kt, vt = diag_kv(h)
        dc, contrib = _sb_tile(q_ref[:, hs], kt, vt, m2, jnp.zeros((tq, 1), F32), vis)
        acc_scr[:, hs] = contrib
        carries.append(dc)

    def alive(cs):
        m = cs[0]
        for c in cs[1:]:
            m = jnp.maximum(m, c)
        return jnp.max(m)

    def cond(state):
        step, top, _ = state
        return jnp.logical_and(step < n_past, top > SB_DEAD)

    def body(state):
        step, _, cs = state
        j = n_past - 1 - step
        new = []
        for h, hs in enumerate(heads):
            kt, vt = past_kv(j, h)
            dc, contrib = _sb_tile(q_ref[:, hs], kt, vt, m2, cs[h], None)
            acc_scr[:, hs] += contrib
            new.append(cs[h] + dc)
        return step + 1, alive(new), tuple(new)

    lax.while_loop(cond, body, (jnp.int32(0), alive(carries), tuple(carries)))
    o_ref[...] = acc_scr[...].astype(o_ref.dtype)


def _tri_after(n):
    r = jnp.arange(n)
    m = (r[:, None] > r[None, :]).astype(BF16)
    return jnp.concatenate([m, m], axis=0)


def _sb_prompt_kernel(q_ref, k_ref, v_ref, m2_ref, o_ref, acc_scr, *, n_heads, tk):
    i = pl.program_id(1)

    def tile(j, h):
        rows = pl.ds(pl.multiple_of(j * tk, tk), tk)
        cols = slice(h * HEAD_DIM, (h + 1) * HEAD_DIM)
        return k_ref[rows, cols], v_ref[rows, cols]

    _sb_core(q_ref, lambda h: tile(i, h), tile, m2_ref, o_ref, acc_scr, i, n_heads, tk)


def _sb_prompt(qb, kb, vb, n_heads):
    b, s, wb = qb.shape
    tq = tk = 128
    qblk = lambda bb, i: (bb, i, 0)
    full = lambda bb, i: (bb, 0, 0)
    return pl.pallas_call(
        functools.partial(_sb_prompt_kernel, n_heads=n_heads, tk=tk),
        grid=(b, s // tq),
        in_specs=[
            pl.BlockSpec((None, tq, wb), qblk),
            pl.BlockSpec((None, s, wb), full),
            pl.BlockSpec((None, s, wb), full),
            pl.BlockSpec((2 * tk, tk), lambda bb, i: (0, 0)),
        ],
        out_specs=pl.BlockSpec((None, tq, wb), qblk),
        out_shape=jax.ShapeDtypeStruct((b, s, wb), BF16),
        scratch_shapes=[pltpu.VMEM((tq, wb), F32)],
        compiler_params=_params("parallel", "arbitrary"),
        name="sb_prompt",
    )(qb, kb, vb, _tri_after(tk))


def _pad_rows(new_ref, buf):
    t = new_ref.shape[0]
    buf[0:t, :] = new_ref[...].astype(buf.dtype)
    buf[t:, :] = jnp.zeros((buf.shape[0] - t, buf.shape[1]), buf.dtype)


def _ik3(x):
    hi, lo = _split_bf16(x)
    return jnp.concatenate([hi, hi, lo], axis=1)


def _dsa_sample_kernel(q_ref, iq_ref, ikw_ref, kn_ref, vn_ref, ck_ref, cv_ref, cik_ref, tri_ref,
                       o_ref, knew, vnew, iknew, s_scr, iq3_scr, acc_scr, l_scr,
                       *, past, t, ck, n_top, n_heads, n_idx_heads, idx_dim):
    _pad_rows(kn_ref, knew)
    _pad_rows(vn_ref, vnew)
    iknew[0:t, :] = ikw_ref[:, 0:idx_dim]
    iknew[t:, :] = jnp.zeros((ck - t, idx_dim), F32)
    limit = jnp.full((t, 1), past + t, jnp.int32)
    iw = ikw_ref[:, idx_dim:idx_dim + n_idx_heads]

    def rows(c):
        return pl.ds(pl.multiple_of(c * ck, ck), ck)

    def cache_kv(c, h):
        cols = slice(h * HEAD_DIM, (h + 1) * HEAD_DIM)
        return ck_ref[rows(c), cols], cv_ref[rows(c), cols]

    def new_kv(h):
        cols = slice(h * HEAD_DIM, (h + 1) * HEAD_DIM)
        return knew[:, cols], vnew[:, cols]

    main = (lambda c: _ik3(cik_ref[rows(c), :]), cache_kv)
    tail = (lambda: _ik3(iknew[...]), new_kv)
    _dsa_core(q_ref, iq_ref[...], iw, main, tail, tri_ref, s_scr, iq3_scr, acc_scr, l_scr, o_ref,
              limit, past // ck, ck, n_top, n_heads, n_idx_heads, idx_dim)


def _dsa_sample(qa, iq, ikw, ka, va, cache_k, cache_v, cache_ik, n_idx_heads):
    b, t, wa = qa.shape
    past, n_heads = cache_k.shape[1], wa // HEAD_DIM
    idx_dim = cache_ik.shape[2]
    assert (past // CHUNK + 1) * CHUNK >= past + t, "new frames must sit in one open chunk"
    ck = DSA_CHUNK
    assert past % ck == 0 and t <= ck
    n_top = min(TOPK_MAX, (past + t) // 4)
    row = lambda bb: (bb, 0, 0)
    return pl.pallas_call(
        functools.partial(_dsa_sample_kernel, past=past, t=t, ck=ck, n_top=n_top, n_heads=n_heads,
                          n_idx_heads=n_idx_heads, idx_dim=idx_dim),
        grid=(b,),
        in_specs=[
            pl.BlockSpec((None, t, wa), row),
            pl.BlockSpec((None, t, iq.shape[2]), row),
            pl.BlockSpec((None, t, LANES), row),
            pl.BlockSpec((None, t, wa), row),
            pl.BlockSpec((None, t, wa), row),
            pl.BlockSpec((None, past, wa), row),
            pl.BlockSpec((None, past, wa), row),
            pl.BlockSpec((None, past, idx_dim), row),
            pl.BlockSpec((ck, ck), lambda bb: (0, 0)),
        ],
        out_specs=pl.BlockSpec((None, t, wa), row),
        out_shape=jax.ShapeDtypeStruct((b, t, wa), BF16),
        scratch_shapes=[
            pltpu.VMEM((ck, wa), BF16), pltpu.VMEM((ck, wa), BF16), pltpu.VMEM((ck, idx_dim), F32),
            pltpu.VMEM((past // ck + 1, t, ck), F32), pltpu.VMEM((n_idx_heads * t, 3 * idx_dim), BF16),
            pltpu.VMEM((t, wa), F32), pltpu.VMEM((t, wa), F32),
        ],
        compiler_params=_params("parallel"),
        name="dsa_sample",
    )(qa, iq, ikw, ka, va, cache_k, cache_v, cache_ik, _tri_incl(ck))


def _sb_sample_kernel(q_ref, kn_ref, vn_ref, ck_ref, cv_ref, m2_ref, o_ref, knew, vnew, acc_scr,
                      *, past, t, tk, n_heads):
    for new_ref, buf in ((kn_ref, knew), (vn_ref, vnew)):
        buf[0:t, :] = new_ref[...]
        buf[t:, :] = jnp.zeros((tk - t, buf.shape[1]), BF16)

    def new_kv(h):
        cols = slice(h * HEAD_DIM, (h + 1) * HEAD_DIM)
        return knew[:, cols], vnew[:, cols]

    def cache_kv(j, h):
        rows = pl.ds(pl.multiple_of(j * tk, tk), tk)
        return ck_ref[rows, h, :].astype(BF16), cv_ref[rows, h, :].astype(BF16)

    _sb_core(q_ref, new_kv, cache_kv, m2_ref, o_ref, acc_scr, past // tk, n_heads, tk)


def _sb_sample(qb, kb, vb, cache_k, cache_v):
    b, t, wb = qb.shape
    past, n_heads = cache_k.shape[1], cache_k.shape[2]
    tk = 128
    assert past % tk == 0 and t <= tk
    row = lambda bb: (bb, 0, 0)
    cache = lambda bb: (bb, 0, 0, 0)
    return pl.pallas_call(
        functools.partial(_sb_sample_kernel, past=past, t=t, tk=tk, n_heads=n_heads),
        grid=(b,),
        in_specs=[
            pl.BlockSpec((None, t, wb), row),
            pl.BlockSpec((None, t, wb), row),
            pl.BlockSpec((None, t, wb), row),
            pl.BlockSpec((None, past, n_heads, HEAD_DIM), cache),
            pl.BlockSpec((None, past, n_heads, HEAD_DIM), cache),
            pl.BlockSpec((2 * tk, tk), lambda bb: (0, 0)),
        ],
        out_specs=pl.BlockSpec((None, t, wb), row),
        out_shape=jax.ShapeDtypeStruct((b, t, wb), BF16),
        scratch_shapes=[pltpu.VMEM((tk, wb), BF16), pltpu.VMEM((tk, wb), BF16), pltpu.VMEM((t, wb), F32)],
        compiler_params=_params("parallel"),
        name="sb_sample",
    )(qb, kb, vb, cache_k, cache_v, _tri_after(tk))


def _out_proj_kernel(y_ref, oa_ref, ob_ref, wa_ref, wb_ref, o_ref):
    acc = jnp.dot(oa_ref[...], wa_ref[...], preferred_element_type=F32)
    acc = acc + jnp.dot(ob_ref[...], wb_ref[...], preferred_element_type=F32)
    o_ref[...] = y_ref[...] + acc


def _out_proj(y, oa, ob, w_a, w_b):
    m, d = y.shape
    tm = min(512, m)
    row = lambda i: (i, 0)
    const = lambda i: (0, 0)
    return pl.pallas_call(
        _out_proj_kernel,
        grid=(m // tm,),
        in_specs=[
            pl.BlockSpec((tm, d), row),
            pl.BlockSpec((tm, oa.shape[1]), row),
            pl.BlockSpec((tm, ob.shape[1]), row),
            pl.BlockSpec(w_a.shape, const),
            pl.BlockSpec(w_b.shape, const),
        ],
        out_specs=pl.BlockSpec((tm, d), row),
        out_shape=jax.ShapeDtypeStruct((m, d), F32),
        compiler_params=_params("parallel"),
        name="out_proj",
    )(y, oa, ob, w_a, w_b)


def _gelu(x):
    return 0.5 * x * (1.0 + jnp.tanh(math.sqrt(2.0 / math.pi) * (x + 0.044715 * (x * x * x))))


def _c_v_kernel(x_ref, g_ref, w_ref, vg_ref, *outs):
    h = _rms(x_ref[...], g_ref[...]).astype(BF16)
    v = _gelu(jnp.dot(h, w_ref[...], preferred_element_type=F32))
    vn = _rms(v, vg_ref[...])
    outs[0][...] = vn.astype(BF16)
    if len(outs) > 1:
        outs[1][...] = vn


def _c_v(x, g, w_v, v_gain, want_f32):
    m, d = x.shape
    cw = w_v.shape[1]
    tm = min(512, m)
    row = lambda i: (i, 0)
    const = lambda i: (0, 0)
    out_specs = [pl.BlockSpec((tm, cw), row)]
    out_shape = [jax.ShapeDtypeStruct((m, cw), BF16)]
    if want_f32:
        out_specs.append(pl.BlockSpec((tm, cw), row))
        out_shape.append(jax.ShapeDtypeStruct((m, cw), F32))
    return pl.pallas_call(
        _c_v_kernel,
        grid=(m // tm,),
        in_specs=[
            pl.BlockSpec((tm, d), row),
            pl.BlockSpec((1, d), const),
            pl.BlockSpec((d, cw), const),
            pl.BlockSpec((1, cw), const),
        ],
        out_specs=out_specs,
        out_shape=out_shape,
        compiler_params=_params("parallel"),
        name="c_v",
    )(x, g.reshape(1, d), w_v, v_gain.reshape(1, cw))


def _c_mix_kernel(x_ref, g_ref, wu_ref, vn_ref, wm_ref, bias_ref, wo_ref, o_ref, h_scr, acc_scr, p_scr,
                  *, nj, gs):
    j = pl.program_id(1)

    @pl.when(j == 0)
    def _():
        h_scr[...] = _rms(x_ref[...], g_ref[...]).astype(BF16)
        acc_scr[...] = jnp.zeros_like(acc_scr)

    u = _gelu(jnp.dot(h_scr[...], wu_ref[...], preferred_element_type=F32))
    tm = u.shape[0]
    for gg in range(gs):
        cs = slice(gg * LANES, (gg + 1) * LANES)
        wm = wm_ref[j * gs + gg]
        for r in range(tm // C_CHUNK):
            rs = slice(r * C_CHUNK, (r + 1) * C_CHUNK)
            mix = jnp.dot(wm, vn_ref[rs, cs], preferred_element_type=F32) + bias_ref[:, cs]
            p_scr[rs, cs] = (u[rs, cs] * mix).astype(BF16)
    acc_scr[...] += jnp.dot(p_scr[...], wo_ref[...], preferred_element_type=F32)

    @pl.when(j == nj - 1)
    def _():
        o_ref[...] = x_ref[...] + acc_scr[...]


def _c_mix(x, g, w_u, vn, w_m, bias, w_o, gs=4):
    m, d = x.shape
    cw = w_u.shape[1]
    ng = cw // LANES
    gs = min(gs, ng)
    nj = ng // gs
    tm = min(512, m)
    row = lambda i, j: (i, 0)
    return pl.pallas_call(
        functools.partial(_c_mix_kernel, nj=nj, gs=gs),
        grid=(m // tm, nj),
        in_specs=[
            pl.BlockSpec((tm, d), row),
            pl.BlockSpec((1, d), lambda i, j: (0, 0)),
            pl.BlockSpec((d, gs * LANES), lambda i, j: (0, j)),
            pl.BlockSpec((tm, gs * LANES), lambda i, j: (i, j)),
            pl.BlockSpec(w_m.shape, lambda i, j: (0, 0, 0)),
            pl.BlockSpec((C_CHUNK, gs * LANES), lambda i, j: (0, j)),
            pl.BlockSpec((gs * LANES, d), lambda i, j: (j, 0)),
        ],
        out_specs=pl.BlockSpec((tm, d), row),
        out_shape=jax.ShapeDtypeStruct((m, d), F32),
        scratch_shapes=[pltpu.VMEM((tm, d), BF16), pltpu.VMEM((tm, d), F32),
                        pltpu.VMEM((tm, gs * LANES), BF16)],
        compiler_params=_params("parallel", "arbitrary"),
        name="c_mix",
    )(x, g.reshape(1, d), w_u, vn, w_m, bias, w_o)


def _rope_tables(pos, reps):
    pos = pos.astype(F32)[:, None]

    def tab(half, copies):
        inv = ROPE_THETA ** (-jnp.arange(half, dtype=F32) / half)
        ang = pos * inv[None, :]
        c, s = jnp.cos(ang), jnp.sin(ang)
        return jnp.tile(jnp.concatenate([c, c], axis=1), (reps, copies)), \
            jnp.tile(jnp.concatenate([-s, s], axis=1), (reps, copies))

    c128, s128 = tab(HEAD_DIM // 2, 1)
    c64, s64 = tab(32, 2)
    return c128, s128, c64, s64


def kernel(x_prompt, x_sample, cache_a_k, cache_a_v, cache_a_ik, cache_b_k, cache_b_v, norm_ff1, ff1_w1, ff1_w3, ff1_w2, norm_mix, norm_ff2, ff2_w1, ff2_w3, ff2_w2, ab_w_in, ab_w_out, c_w_in, c_v_norm, c_w_s, c_b_s, c_w_out, final_norm):
    bp, seq, d = x_prompt.shape
    bs, t, _ = x_sample.shape
    past, ha = cache_a_k.shape[2], cache_a_k.shape[3]
    hb = cache_b_k.shape[3]
    idx_dim = cache_a_ik.shape[3]
    wa, wb = ha * HEAD_DIM, hb * HEAD_DIM
    n_idx_heads = (ab_w_in.shape[2] - 3 * wa - 3 * wb - idx_dim) // (idx_dim + 1)
    wi = n_idx_heads * idx_dim
    assert idx_dim == 64 and wi % LANES == 0 and wi <= wa and n_idx_heads <= LANES - idx_dim
    depth = norm_ff1.shape[0]
    mp, ms = bp * seq, bs * t

    yp = x_prompt.reshape(mp, d)
    ys = x_sample.reshape(ms, d)
    bf = lambda w: w.astype(BF16)

    tabs_p = _rope_tables(jnp.arange(seq), 1)
    tabs_s = _rope_tables(past + jnp.arange(t), min(512, ms) // t)

    outs_p, outs_s, s_cv = [], [], []
    for layer in range(depth):
        j = layer // 2
        w1, w3, w2 = bf(ff1_w1[layer]), bf(ff1_w3[layer]), bf(ff1_w2[layer])
        yp = _half_ffn(yp, norm_ff1[layer], w1, w3, w2)
        ys = _half_ffn(ys, norm_ff1[layer], w1, w3, w2)
        if layer % 2 == 0:
            w_in = ab_w_in[j]
            o = 0
            cols = []
            for width in (wa, wa, wa, wi, idx_dim, n_idx_heads, wb, wb, wb):
                cols.append(w_in[:, o:o + width])
                o += width
            w_qa, w_ka, w_va, w_iq, w_ik, w_iw, w_qb, w_kb, w_vb = cols
            pad = lambda w, width: jnp.pad(w, ((0, 0), (0, width - w.shape[1])))
            w_pa = bf(jnp.concatenate([w_qa, w_ka, w_va, pad(w_iq, wa)], axis=1))
            w_kw = bf(pad(jnp.concatenate([w_ik, w_iw], axis=1), LANES))
            w_pb = bf(jnp.concatenate([w_qb, w_kb, w_vb], axis=1))
            w_oa, w_ob = bf(ab_w_out[j][:wa]), bf(ab_w_out[j][wa:])

            def mixer(y, tabs, n_pos_rows):
                qa, kaf, kab, vaf, vab, iq, ikw = _proj_a(
                    y, norm_mix[layer], w_pa, w_kw, tabs, n_pos_rows, ha, n_idx_heads, idx_dim)
                qb, kbf, kbb, vbf, vbb = _proj_b(y, norm_mix[layer], w_pb)
                return (qa, kaf, kab, vaf, vab, iq, ikw, qb, kbf, kbb, vbf, vbb)

            qa, kaf, kab, vaf, vab, iq, ikw, qb, kbf, kbb, vbf, vbb = mixer(yp, tabs_p, seq)
            r3 = lambda a: a.reshape(bp, seq, a.shape[-1])
            ik = ikw[:, :idx_dim]
            o_a = _dsa_prompt_t(r3(qa), r3(iq), r3(ikw), r3(kab), r3(vab), ha, n_idx_heads, idx_dim)
            o_b = _sb_prompt(r3(qb), r3(kbb), r3(vbb), hb)
            yp = _out_proj(yp, o_a.reshape(mp, wa), o_b.reshape(mp, wb), w_oa, w_ob)
            outs_p.append((kaf.reshape(bp, seq, ha, HEAD_DIM), vaf.reshape(bp, seq, ha, HEAD_DIM),
                           ik.reshape(bp, seq, idx_dim),
                           kbf.reshape(bp, seq, hb, HEAD_DIM), vbf.reshape(bp, seq, hb, HEAD_DIM)))

            qa, kaf, kab, vaf, vab, iq, ikw, qb, kbf, kbb, vbf, vbb = mixer(ys, tabs_s, min(512, ms))
            r3 = lambda a: a.reshape(bs, t, a.shape[-1])
            lanes_bf = lambda c: c.reshape(bs, past, wa).astype(BF16)
            o_a = _dsa_sample(r3(qa), r3(iq), r3(ikw), r3(kab), r3(vab),
                              lanes_bf(cache_a_k[j]), lanes_bf(cache_a_v[j]), cache_a_ik[j], n_idx_heads)
            o_b = _sb_sample(r3(qb), r3(kbb), r3(vbb), cache_b_k[j], cache_b_v[j])
            ys = _out_proj(ys, o_a.reshape(ms, wa), o_b.reshape(ms, wb), w_oa, w_ob)
            outs_s.append((kaf.reshape(bs, t, ha, HEAD_DIM), vaf.reshape(bs, t, ha, HEAD_DIM),
                           ikw[:, :idx_dim].reshape(bs, t, idx_dim),
                           kbf.reshape(bs, t, hb, HEAD_DIM), vbf.reshape(bs, t, hb, HEAD_DIM)))
        else:
            cw = c_w_in.shape[2] // 2
            ng = c_w_s.shape[1]
            w_u, w_v = bf(c_w_in[j][:, :cw]), bf(c_w_in[j][:, cw:])
            w_o = bf(c_w_out[j])
            i = jnp.arange(C_CHUNK)
            mask = (i[None, :] // CHUNK) <= (i[:, None] // CHUNK)
            w_m = jnp.where(mask[None], c_w_s[j], 0.0)
            bias_p = jnp.repeat(c_b_s[j].T, cw // ng, axis=1)
            per = C_CHUNK // t
            w_ms = jnp.einsum('ab,gij->gaibj', jnp.eye(per, dtype=F32), w_m[:, :t, :t]).reshape(ng, C_CHUNK, C_CHUNK)
            bias_s = jnp.tile(bias_p[:t], (per, 1))

            vn = _c_v(yp, norm_mix[layer], w_v, c_v_norm[j], False)[0]
            yp = _c_mix(yp, norm_mix[layer], w_u, vn, bf(w_m), bias_p, w_o)
            vn, vn_f32 = _c_v(ys, norm_mix[layer], w_v, c_v_norm[j], True)
            ys = _c_mix(ys, norm_mix[layer], w_u, vn, bf(w_ms), bias_s, w_o)
            s_cv.append(vn_f32.reshape(bs, t, cw))
        last = layer == depth - 1
        w1, w3, w2 = bf(ff2_w1[layer]), bf(ff2_w3[layer]), bf(ff2_w2[layer])
        fg = final_norm if last else None
        yp = _half_ffn(yp, norm_ff2[layer], w1, w3, w2, fg)
        ys = _half_ffn(ys, norm_ff2[layer], w1, w3, w2, fg)

    stack = lambda outs, k: jnp.stack([o[k] for o in outs])
    return (yp.reshape(bp, seq, d), ys.reshape(bs, t, d),
            stack(outs_p, 0), stack(outs_p, 1), stack(outs_p, 2), stack(outs_p, 3), stack(outs_p, 4),
            stack(outs_s, 0), stack(outs_s, 1), stack(outs_s, 2), stack(outs_s, 3), stack(outs_s, 4),
            jnp.stack(s_cv))
```

```python
import functools
import math

import jax
import jax.numpy as jnp
from jax import lax
from jax.experimental import pallas as pl
from jax.experimental.pallas import tpu as pltpu

F32 = jnp.float32
BF16 = jnp.bfloat16

RMS_EPS = 1e-6
CHUNK = 64
TOPK_MAX = 256
ROPE_THETA = 10000.0
C_CHUNK = 128
LANES = 128
HEAD_DIM = 128
NEG = -1e30
INT_MIN = -2147483648
SB_DEAD = -105.0
DSA_CHUNK = 256
VMEM_LIMIT_BYTES = 56 * 1024 * 1024

NT_DIMS = (((1,), (1,)), ((), ()))


def _params(*sem):
    return pltpu.CompilerParams(dimension_semantics=sem, vmem_limit_bytes=VMEM_LIMIT_BYTES)


def _rms(x, g):
    return x * lax.rsqrt(jnp.mean(x * x, axis=-1, keepdims=True) + RMS_EPS) * g


def _split_bf16(x):
    hi = x.astype(BF16)
    lo = (x - hi.astype(F32)).astype(BF16)
    return hi, lo


def _ffn_kernel(x_ref, g_ref, w1_ref, w3_ref, w2_ref, *rest, nf, final_norm):
    if final_norm:
        gf_ref, o_ref, h_scr, acc_scr = rest
    else:
        o_ref, h_scr, acc_scr = rest
    f = pl.program_id(1)

    @pl.when(f == 0)
    def _():
        h_scr[...] = _rms(x_ref[...], g_ref[...]).astype(BF16)
        acc_scr[...] = jnp.zeros_like(acc_scr)

    h = h_scr[...]
    a = jnp.dot(h, w1_ref[...], preferred_element_type=F32)
    b = jnp.dot(h, w3_ref[...], preferred_element_type=F32)
    p = (a * jax.nn.sigmoid(a) * b).astype(BF16)
    acc_scr[...] += jnp.dot(p, w2_ref[...], preferred_element_type=F32)

    @pl.when(f == nf - 1)
    def _():
        y = x_ref[...] + 0.5 * acc_scr[...]
        if final_norm:
            y = _rms(y, gf_ref[...])
        o_ref[...] = y


def _half_ffn(x, g, w1, w3, w2, final_g=None, tf=512):
    m, d = x.shape
    dff = w1.shape[1]
    tm = min(512, m)
    nf = dff // tf
    in_specs = [
        pl.BlockSpec((tm, d), lambda i, f: (i, 0)),
        pl.BlockSpec((1, d), lambda i, f: (0, 0)),
        pl.BlockSpec((d, tf), lambda i, f: (0, f)),
        pl.BlockSpec((d, tf), lambda i, f: (0, f)),
        pl.BlockSpec((tf, d), lambda i, f: (f, 0)),
    ]
    args = [x, g.reshape(1, d), w1, w3, w2]
    if final_g is not None:
        in_specs.append(pl.BlockSpec((1, d), lambda i, f: (0, 0)))
        args.append(final_g.reshape(1, d))
    return pl.pallas_call(
        functools.partial(_ffn_kernel, nf=nf, final_norm=final_g is not None),
        grid=(m // tm, nf),
        in_specs=in_specs,
        out_specs=pl.BlockSpec((tm, d), lambda i, f: (i, 0)),
        out_shape=jax.ShapeDtypeStruct((m, d), F32),
        scratch_shapes=[pltpu.VMEM((tm, d), BF16), pltpu.VMEM((tm, d), F32)],
        compiler_params=_params("parallel", "arbitrary"),
        name="half_ffn",
    )(*args)


def _rope_heads(z, cos, sin):
    outs = []
    for h in range(z.shape[1] // LANES):
        zh = z[:, h * LANES:(h + 1) * LANES]
        outs.append(zh * cos + pltpu.roll(zh, LANES // 2, axis=1) * sin)
    return outs


def _rope_pairs(z, cos, sin):
    lane = lax.broadcasted_iota(jnp.int32, (z.shape[0], LANES), 1)
    low = (lane % 64) < 32
    outs = []
    for h in range(z.shape[1] // LANES):
        zh = z[:, h * LANES:(h + 1) * LANES]
        partner = jnp.where(low, pltpu.roll(zh, LANES - 32, axis=1), pltpu.roll(zh, 32, axis=1))
        outs.append(zh * cos + partner * sin)
    return outs


def _proj_a_kernel(x_ref, g_ref, w_ref, wkw_ref, c128_ref, s128_ref, c64_ref, s64_ref,
                   qa_ref, kaf_ref, kab_ref, vaf_ref, vab_ref, iq_ref, ikw_ref, h_scr,
                   *, n_idx_heads, idx_dim):
    j = pl.program_id(1)

    @pl.when(j == 0)
    def _():
        h_scr[...] = _rms(x_ref[...], g_ref[...]).astype(BF16)

    z = jnp.dot(h_scr[...], w_ref[...], preferred_element_type=F32)
    wa = qa_ref.shape[1]

    @pl.when(j == 0)
    def _():
        scale = HEAD_DIM ** -0.5
        for h, r in enumerate(_rope_heads(z[:, :wa], c128_ref[...], s128_ref[...])):
            qa_ref[:, h * LANES:(h + 1) * LANES] = (r * scale).astype(BF16)

    @pl.when(j == 1)
    def _():
        for h, r in enumerate(_rope_heads(z[:, :wa], c128_ref[...], s128_ref[...])):
            kaf_ref[:, h * LANES:(h + 1) * LANES] = r
            kab_ref[:, h * LANES:(h + 1) * LANES] = r.astype(BF16)

    @pl.when(j == 2)
    def _():
        kv = z[:, :wa]
        vaf_ref[...] = kv
        vab_ref[...] = kv.astype(BF16)

    @pl.when(j == 3)
    def _():
        wi = iq_ref.shape[1]
        for h, r in enumerate(_rope_pairs(z[:, :wi], c64_ref[...], s64_ref[...])):
            iq_ref[:, h * LANES:(h + 1) * LANES] = r * (idx_dim ** -0.5)
        zz = jnp.dot(h_scr[...], wkw_ref[...], preferred_element_type=F32)
        r = _rope_pairs(zz, c64_ref[...], s64_ref[...])[0]
        lane = lax.broadcasted_iota(jnp.int32, zz.shape, 1)
        ikw_ref[...] = jnp.where(lane < idx_dim, r, zz * (n_idx_heads ** -0.5))


def _proj_a(x, g, w, w_kw, tabs, n_pos_rows, ha, n_idx_heads, idx_dim):
    m, d = x.shape
    tm = min(512, m)
    wa = ha * HEAD_DIM
    wi = n_idx_heads * idx_dim
    tn = w.shape[1] // 4
    npb = n_pos_rows // tm
    row = lambda i, j: (i, 0)
    tab = lambda i, j: (i % npb, 0)
    tab_spec = pl.BlockSpec((tm, LANES), tab)
    return pl.pallas_call(
        functools.partial(_proj_a_kernel, n_idx_heads=n_idx_heads, idx_dim=idx_dim),
        grid=(m // tm, 4),
        in_specs=[
            pl.BlockSpec((tm, d), row),
            pl.BlockSpec((1, d), lambda i, j: (0, 0)),
            pl.BlockSpec((d, tn), lambda i, j: (0, j)),
            pl.BlockSpec((d, LANES), lambda i, j: (0, 0)),
            tab_spec, tab_spec, tab_spec, tab_spec,
        ],
        out_specs=[
            pl.BlockSpec((tm, wa), row), pl.BlockSpec((tm, wa), row), pl.BlockSpec((tm, wa), row),
            pl.BlockSpec((tm, wa), row), pl.BlockSpec((tm, wa), row),
            pl.BlockSpec((tm, wi), row), pl.BlockSpec((tm, LANES), row),
        ],
        out_shape=[
            jax.ShapeDtypeStruct((m, wa), BF16), jax.ShapeDtypeStruct((m, wa), F32),
            jax.ShapeDtypeStruct((m, wa), BF16), jax.ShapeDtypeStruct((m, wa), F32),
            jax.ShapeDtypeStruct((m, wa), BF16),
            jax.ShapeDtypeStruct((m, wi), F32), jax.ShapeDtypeStruct((m, LANES), F32),
        ],
        scratch_shapes=[pltpu.VMEM((tm, d), BF16)],
        compiler_params=_params("parallel", "arbitrary"),
        name="proj_a",
    )(x, g.reshape(1, d), w, w_kw, *tabs)


def _proj_b_kernel(x_ref, g_ref, w_ref, qb_ref, kbf_ref, kbb_ref, vbf_ref, vbb_ref, h_scr):
    j = pl.program_id(1)

    @pl.when(j == 0)
    def _():
        h_scr[...] = _rms(x_ref[...], g_ref[...]).astype(BF16)

    z = jnp.dot(h_scr[...], w_ref[...], preferred_element_type=F32)

    @pl.when(j == 0)
    def _():
        qb_ref[...] = (z * (HEAD_DIM ** -0.5)).astype(BF16)

    @pl.when(j == 1)
    def _():
        kbf_ref[...] = z
        kbb_ref[...] = z.astype(BF16)

    @pl.when(j == 2)
    def _():
        vbf_ref[...] = z
        vbb_ref[...] = z.astype(BF16)


def _proj_b(x, g, w):
    m, d = x.shape
    tm = min(512, m)
    wb = w.shape[1] // 3
    row = lambda i, j: (i, 0)
    return pl.pallas_call(
        _proj_b_kernel,
        grid=(m // tm, 3),
        in_specs=[
            pl.BlockSpec((tm, d), row),
            pl.BlockSpec((1, d), lambda i, j: (0, 0)),
            pl.BlockSpec((d, wb), lambda i, j: (0, j)),
        ],
        out_specs=[pl.BlockSpec((tm, wb), row)] * 5,
        out_shape=[
            jax.ShapeDtypeStruct((m, wb), BF16), jax.ShapeDtypeStruct((m, wb), F32),
            jax.ShapeDtypeStruct((m, wb), BF16), jax.ShapeDtypeStruct((m, wb), F32),
            jax.ShapeDtypeStruct((m, wb), BF16),
        ],
        scratch_shapes=[pltpu.VMEM((tm, d), BF16)],
        compiler_params=_params("parallel", "arbitrary"),
        name="proj_b",
    )(x, g.reshape(1, d), w)


def _key_to_f32(key):
    bits = jnp.where(key >= 0, key, key ^ jnp.int32(0x7FFFFFFF))
    return lax.bitcast_convert_type(bits, F32)


def _tree(parts, op):
    while len(parts) > 1:
        parts = [op(parts[k], parts[k + 1]) if k + 1 < len(parts) else parts[k] for k in range(0, len(parts), 2)]
    return parts[0]


def _fold_lanes(x, op):
    return _tree([x[:, t * LANES:(t + 1) * LANES] for t in range(x.shape[1] // LANES)], op)


def _dsa_mask(iq, iw, ik3_main, ik3_tail, tri_ref, s_scr, iq3_scr, limit, n_main, ck, n_top,
              n_idx_heads, idx_dim):
    tq = iq.shape[0]
    n_chunks = n_main + 1

    for h in range(n_idx_heads):
        x = iq[:, h * idx_dim:(h + 1) * idx_dim]
        hi = x.astype(BF16).astype(F32)
        iq3_scr[h * tq:(h + 1) * tq, :] = jnp.concatenate([hi, x - hi, hi], axis=1).astype(BF16)

    col0 = lax.broadcasted_iota(jnp.int32, (tq, ck), 1)

    def score_chunk(c, ikc):
        rel = lax.dot_general(iq3_scr[...], ikc, NT_DIMS, preferred_element_type=F32)
        acc = iw[:, 0:1] * jnp.maximum(rel[0:tq], 0.0)
        for h in range(1, n_idx_heads):
            acc = acc + iw[:, h:h + 1] * jnp.maximum(rel[h * tq:(h + 1) * tq], 0.0)
        s_scr[c] = jnp.where(col0 + c * ck < limit, acc, -jnp.inf)

    def score_main(c, _):
        score_chunk(c, ik3_main(c))
        return 0

    lax.fori_loop(0, n_main, score_main, 0)
    score_chunk(n_main, ik3_tail())

    def count(pred_fn):
        def body(c, acc):
            return acc + _fold_lanes(jnp.where(pred_fn(s_scr[c]), 1.0, 0.0), jnp.add)
        acc = lax.fori_loop(0, n_chunks, body, jnp.zeros((tq, LANES), F32))
        return jnp.sum(acc, axis=1, keepdims=True)

    kf = jnp.float32(n_top)
    cnt = count(lambda s: s >= 0.0)
    key = jnp.where(cnt >= kf, jnp.int32(0), jnp.int32(INT_MIN))

    def bit_step(b, key):
        cand = key | jnp.left_shift(jnp.int32(1), 30 - b)
        cand_f = _key_to_f32(cand)
        cnt = count(lambda s: s >= cand_f)
        return jnp.where(cnt >= kf, cand, key)

    key = lax.fori_loop(0, 31, bit_step, key)
    thr = _key_to_f32(key)
    need = kf - count(lambda s: s > thr)
    take_all = limit <= n_top

    def bias_chunk(c, carry):
        s = s_scr[c]
        eq = s == thr
        pre = jnp.dot(jnp.where(eq, 1.0, 0.0).astype(BF16), tri_ref[...], preferred_element_type=F32)
        tied = jnp.where((carry + pre) <= need, 0.0, NEG)
        bias = jnp.where(eq, tied, jnp.where(s > thr, 0.0, NEG))
        s_scr[c] = jnp.where(take_all, jnp.where(s > -jnp.inf, 0.0, NEG), bias)
        return carry + pre[:, ck - 1:ck]

    lax.fori_loop(0, n_chunks, bias_chunk, jnp.zeros((tq, 1), F32))


def _tri_incl(n):
    r = jnp.arange(n)
    return (r[:, None] <= r[None, :]).astype(BF16)


def _fold_rows(x, op):
    return _tree([x[r * 8:(r + 1) * 8] for r in range(x.shape[0] // 8)], op)


def _topk_bias_t(s_scr, n_chunks, limit, n_top, ltri_ref):
    ck, nl = s_scr.shape[1], s_scr.shape[2]

    def count(pred_fn):
        def body(c, acc):
            return acc + _fold_rows(jnp.where(pred_fn(s_scr[c]), 1.0, 0.0), jnp.add)
        acc = lax.fori_loop(0, n_chunks, body, jnp.zeros((8, nl), F32))
        return jnp.sum(acc, axis=0, keepdims=True)

    kf = jnp.float32(n_top)
    cnt = count(lambda s: s >= 0.0)
    key = jnp.where(cnt >= kf, jnp.int32(0), jnp.int32(INT_MIN))

    def bit_step(b, key):
        cand = key | jnp.left_shift(jnp.int32(1), 30 - b)
        cand_f = _key_to_f32(cand)
        cnt = count(lambda s: s >= cand_f)
        return jnp.where(cnt >= kf, cand, key)

    key = lax.fori_loop(0, 31, bit_step, key)
    thr = _key_to_f32(key)
    need = kf - count(lambda s: s > thr)
    take_all = limit <= n_top

    def bias_chunk(c, carry):
        s = s_scr[c]
        eq = s == thr
        pre = jnp.dot(ltri_ref[...], jnp.where(eq, 1.0, 0.0).astype(BF16), preferred_element_type=F32)
        tied = jnp.where((carry + pre) <= need, 0.0, NEG)
        bias = jnp.where(eq, tied, jnp.where(s > thr, 0.0, NEG))
        s_scr[c] = jnp.where(take_all, jnp.where(s > -jnp.inf, 0.0, NEG), bias)
        return carry + pre[ck - 1:ck, :]

    lax.fori_loop(0, n_chunks, bias_chunk, jnp.zeros((1, nl), F32))


def _dsa_prompt_t_kernel(qt_ref, iq3t_ref, iwt_ref, ik3_ref, k_ref, vt_ref, ltri_ref, o_ref,
                         s_scr, acc_scr, lg_a, lg_b,
                         *, tq, ck, n_top, n_heads, n_idx_heads):
    i = pl.program_id(1)
    pos = i * tq + lax.broadcasted_iota(jnp.int32, (1, tq), 1)
    limit = (pos // CHUNK + 1) * CHUNK
    n_chunks = ((i + 1) * tq + ck - 1) // ck
    row0 = lax.broadcasted_iota(jnp.int32, (ck, tq), 0)
    iwt = iwt_ref[...]

    def rows(c):
        return pl.ds(pl.multiple_of(c * ck, ck), ck)

    def score_chunk(c, _):
        ikc = ik3_ref[rows(c), :]
        acc = None
        for g in range(n_idx_heads // 2):
            rel = jnp.dot(ikc, iq3t_ref[:, 2 * g * tq:2 * (g + 1) * tq], preferred_element_type=F32)
            for u in range(2):
                h = 2 * g + u
                term = iwt[h:h + 1, :] * jnp.maximum(rel[:, u * tq:(u + 1) * tq], 0.0)
                acc = term if acc is None else acc + term
        s_scr[c] = jnp.where(row0 + c * ck < limit, acc, -jnp.inf)
        return 0

    lax.fori_loop(0, n_chunks, score_chunk, 0)
    _topk_bias_t(s_scr, n_chunks, limit, n_top, ltri_ref)

    heads = [slice(h * HEAD_DIM, (h + 1) * HEAD_DIM) for h in range(n_heads)]
    acc_scr[...] = jnp.zeros_like(acc_scr)

    @pl.when(n_chunks % 2 == 1)
    def _():
        s_scr[n_chunks] = jnp.full((ck, tq), NEG, F32)

    def logits(c, buf):
        for h, hs in enumerate(heads):
            buf[h] = jnp.dot(k_ref[rows(c), hs], qt_ref[hs, :], preferred_element_type=F32)

    def attend(c, buf, ms, ls):
        bias = s_scr[c]
        new_m, new_l = [], []
        for h, hs in enumerate(heads):
            lg = buf[h] + bias
            m_new = jnp.maximum(ms[h], jnp.max(_fold_rows(lg, jnp.maximum), axis=0, keepdims=True))
            alpha = jnp.exp(ms[h] - m_new)
            p = jnp.exp(lg - m_new)
            new_l.append(alpha * ls[h] + _fold_rows(p, jnp.add))
            acc_scr[hs, :] = alpha * acc_scr[hs, :] + jnp.dot(vt_ref[c, hs, :], p.astype(BF16),
                                                              preferred_element_type=F32)
            new_m.append(m_new)
        return tuple(new_m), tuple(new_l)

    n_pairs = (n_chunks + 1) // 2
    last = 2 * n_pairs - 1

    def att_pair(pr, state):
        c0 = 2 * pr
        logits(c0 + 1, lg_b)
        state = attend(c0, lg_a, *state)
        logits(jnp.minimum(c0 + 2, last), lg_a)
        return attend(c0 + 1, lg_b, *state)

    logits(0, lg_a)
    init = (tuple(jnp.full((1, tq), NEG, F32) for _ in heads), tuple(jnp.zeros((8, tq), F32) for _ in heads))
    _, ls = lax.fori_loop(0, n_pairs, att_pair, init)
    for h, hs in enumerate(heads):
        o_t = acc_scr[hs, :] / jnp.sum(ls[h], axis=0, keepdims=True)
        o_ref[:, hs] = o_t.T.astype(o_ref.dtype)


def _dsa_prompt_t(qa, iq, ikw, ka, va, n_heads, n_idx_heads, idx_dim):
    b, s, wa = qa.shape
    tq = 128
    ck = DSA_CHUNK
    nq = s // tq
    n_top = min(TOPK_MAX, s // 4)
    assert n_idx_heads % 2 == 0 and s % (2 * ck) == 0

    qt = qa.transpose(0, 2, 1)
    x = iq.reshape(b, nq, tq, n_idx_heads, idx_dim)
    hi = x.astype(BF16)
    lo = (x - hi.astype(F32)).astype(BF16)
    iq3t = jnp.concatenate([hi, lo, hi], axis=-1).transpose(0, 1, 4, 3, 2)
    iq3t = iq3t.reshape(b, nq, 3 * idx_dim, n_idx_heads * tq)
    iwt = ikw[:, :, idx_dim:idx_dim + n_idx_heads].reshape(b, nq, tq, n_idx_heads).transpose(0, 1, 3, 2)
    ik_hi, ik_lo = _split_bf16(ikw[:, :, :idx_dim])
    ik3 = jnp.concatenate([ik_hi, ik_hi, ik_lo], axis=-1)
    vt = va.reshape(b, s // ck, ck, wa).transpose(0, 1, 3, 2)
    r = jnp.arange(ck)
    ltri = (r[None, :] <= r[:, None]).astype(BF16)

    return pl.pallas_call(
        functools.partial(_dsa_prompt_t_kernel, tq=tq, ck=ck, n_top=n_top, n_heads=n_heads,
                          n_idx_heads=n_idx_heads),
        grid=(b, nq),
        in_specs=[
            pl.BlockSpec((None, wa, tq), lambda bb, i: (bb, 0, i)),
            pl.BlockSpec((None, None, 3 * idx_dim, n_idx_heads * tq), lambda bb, i: (bb, i, 0, 0)),
            pl.BlockSpec((None, None, n_idx_heads, tq), lambda bb, i: (bb, i, 0, 0)),
            pl.BlockSpec((None, s, 3 * idx_dim), lambda bb, i: (bb, 0, 0)),
            pl.BlockSpec((None, s, wa), lambda bb, i: (bb, 0, 0)),
            pl.BlockSpec((None, s // ck, wa, ck), lambda bb, i: (bb, 0, 0, 0)),
            pl.BlockSpec((ck, ck), lambda bb, i: (0, 0)),
        ],
        out_specs=pl.BlockSpec((None, tq, wa), lambda bb, i: (bb, i, 0)),
        out_shape=jax.ShapeDtypeStruct((b, s, wa), BF16),
        scratch_shapes=[pltpu.VMEM((s // ck, ck, tq), F32), pltpu.VMEM((wa, tq), F32),
                        pltpu.VMEM((n_heads, ck, tq), F32), pltpu.VMEM((n_heads, ck, tq), F32)],
        compiler_params=_params("parallel", "arbitrary"),
        name="dsa_prompt",
    )(qt, iq3t, iwt, ik3, ka, vt, ltri)


def _sb_tile(q_h, kt, vt, m2, carry, vis):
    z = lax.dot_general(q_h, kt, NT_DIMS, preferred_element_type=F32)
    sp = jnp.maximum(z, 0.0) + jnp.log1p(jnp.exp(-jnp.abs(z)))
    lk = -sp if vis is None else jnp.where(vis, -sp, 0.0)
    hi, lo = _split_bf16(lk)
    after = jnp.dot(jnp.concatenate([hi, lo], axis=1), m2, preferred_element_type=F32)
    a = jnp.exp(z - sp + after + carry)
    if vis is not None:
        a = jnp.where(vis, a, 0.0)
    return jnp.sum(lk, axis=1, keepdims=True), jnp.dot(a.astype(BF16), vt, preferred_element_type=F32)


def _sb_core(q_ref, diag_kv, past_kv, m2_ref, o_ref, acc_scr, n_past, n_heads, tk):
    tq = q_ref.shape[0]
    vis = lax.broadcasted_iota(jnp.int32, (tq, tk), 1) < lax.broadcasted_iota(jnp.int32, (tq, tk), 0)
    m2 = m2_ref[...]
    heads = [slice(h * HEAD_DIM, (h + 1) * HEAD_DIM) for h in range(n_heads)]

    carries = []
    for h, hs in enumerate(heads):
        kt, vt = diag_kv(h)
        dc, contrib = _sb_tile(q_ref[:, hs], kt, vt, m2, jnp.zeros((tq, 1), F32), vis)
        acc_scr[:, hs] = contrib
        carries.append(dc)

    def alive(cs):
        m = cs[0]
        for c in cs[1:]:
            m = jnp.maximum(m, c)
        return jnp.max(m)

    def cond(state):
        step, top, _ = state
        return jnp.logical_and(step < n_past, top > SB_DEAD)

    def body(state):
        step, _, cs = state
        j = n_past - 1 - step
        new = []
        for h, hs in enumerate(heads):
            kt, vt = past_kv(j, h)
            dc, contrib = _sb_tile(q_ref[:, hs], kt, vt, m2, cs[h], None)
            acc_scr[:, hs] += contrib
            new.append(cs[h] + dc)
        return step + 1, alive(new), tuple(new)

    lax.while_loop(cond, body, (jnp.int32(0), alive(carries), tuple(carries)))
    o_ref[...] = acc_scr[...].astype(o_ref.dtype)


def _tri_after(n):
    r = jnp.arange(n)
    m = (r[:, None] > r[None, :]).astype(BF16)
    return jnp.concatenate([m, m], axis=0)


def _sb_prompt_kernel(q_ref, k_ref, v_ref, m2_ref, o_ref, acc_scr, *, n_heads, tk):
    i = pl.program_id(1)

    def tile(j, h):
        rows = pl.ds(pl.multiple_of(j * tk, tk), tk)
        cols = slice(h * HEAD_DIM, (h + 1) * HEAD_DIM)
        return k_ref[rows, cols], v_ref[rows, cols]

    _sb_core(q_ref, lambda h: tile(i, h), tile, m2_ref, o_ref, acc_scr, i, n_heads, tk)


def _sb_prompt(qb, kb, vb, n_heads):
    b, s, wb = qb.shape
    tq = tk = 128
    qblk = lambda bb, i: (bb, i, 0)
    full = lambda bb, i: (bb, 0, 0)
    return pl.pallas_call(
        functools.partial(_sb_prompt_kernel, n_heads=n_heads, tk=tk),
        grid=(b, s // tq),
        in_specs=[
            pl.BlockSpec((None, tq, wb), qblk),
            pl.BlockSpec((None, s, wb), full),
            pl.BlockSpec((None, s, wb), full),
            pl.BlockSpec((2 * tk, tk), lambda bb, i: (0, 0)),
        ],
        out_specs=pl.BlockSpec((None, tq, wb), qblk),
        out_shape=jax.ShapeDtypeStruct((b, s, wb), BF16),
        scratch_shapes=[pltpu.VMEM((tq, wb), F32)],
        compiler_params=_params("parallel", "arbitrary"),
        name="sb_prompt",
    )(qb, kb, vb, _tri_after(tk))


def _pad_rows(new_ref, buf):
    t = new_ref.shape[0]
    buf[0:t, :] = new_ref[...].astype(buf.dtype)
    buf[t:, :] = jnp.zeros((buf.shape[0] - t, buf.shape[1]), buf.dtype)


def _ik3(x):
    hi, lo = _split_bf16(x)
    return jnp.concatenate([hi, hi, lo], axis=1)


def _dsa_sample_kernel(q_ref, iq_ref, ikw_ref, kn_ref, vn_ref, cik_ref, tri_ref, ck_hbm, cv_hbm,
                       o_ref, knew, vnew, iknew, s_scr, iq3_scr, kbuf, vbuf, sem,
                       *, past, t, ck, n_top, n_heads, n_idx_heads, idx_dim):
    b = pl.program_id(0)

    def head_copies(h):
        return (pltpu.make_async_copy(ck_hbm.at[b, :, h, :], kbuf.at[h], sem.at[0, h]),
                pltpu.make_async_copy(cv_hbm.at[b, :, h, :], vbuf.at[h], sem.at[1, h]))

    for h in range(n_heads):
        for cp in head_copies(h):
            cp.start()
    tn = knew.shape[0]
    _pad_rows(kn_ref, knew)
    _pad_rows(vn_ref, vnew)
    iknew[0:t, :] = ikw_ref[:, 0:idx_dim]
    iknew[t:, :] = jnp.zeros((ck - t, idx_dim), F32)
    limit = jnp.full((t, 1), past + t, jnp.int32)
    iw = ikw_ref[:, idx_dim:idx_dim + n_idx_heads]
    n_main = past // ck

    def ik3_main(c):
        return _ik3(cik_ref[pl.ds(pl.multiple_of(c * ck, ck), ck), :])

    _dsa_mask(iq_ref[...], iw, ik3_main, lambda: _ik3(iknew[...]), tri_ref, s_scr, iq3_scr,
              limit, n_main, ck, n_top, n_idx_heads, idx_dim)

    bias_past = jnp.concatenate([s_scr[c] for c in range(n_main)], axis=1)
    bias_new = s_scr[n_main][:, 0:tn]
    for h in range(n_heads):
        hs = slice(h * HEAD_DIM, (h + 1) * HEAD_DIM)
        q_h = q_ref[:, hs]
        for cp in head_copies(h):
            cp.wait()
        lg_p = lax.dot_general(q_h, kbuf[h].astype(BF16), NT_DIMS, preferred_element_type=F32) + bias_past
        lg_n = lax.dot_general(q_h, knew[:, hs], NT_DIMS, preferred_element_type=F32) + bias_new
        m = jnp.maximum(jnp.max(_fold_lanes(lg_p, jnp.maximum), axis=1, keepdims=True),
                        jnp.max(lg_n, axis=1, keepdims=True))
        p_p = jnp.exp(lg_p - m)
        p_n = jnp.exp(lg_n - m)
        l = jnp.sum(_fold_lanes(p_p, jnp.add), axis=1, keepdims=True) + jnp.sum(p_n, axis=1, keepdims=True)
        o = jnp.dot(p_p.astype(BF16), vbuf[h].astype(BF16), preferred_element_type=F32)
        o = o + jnp.dot(p_n.astype(BF16), vnew[:, hs], preferred_element_type=F32)
        o_ref[:, hs] = (o / l).astype(o_ref.dtype)


def _dsa_sample(qa, iq, ikw, ka, va, cache_k, cache_v, cache_ik, n_idx_heads):
    b, t, wa = qa.shape
    past, n_heads = cache_k.shape[1], cache_k.shape[2]
    idx_dim = cache_ik.shape[2]
    assert (past // CHUNK + 1) * CHUNK >= past + t, "new frames must sit in one open chunk"
    ck = DSA_CHUNK
    tn = LANES
    assert past % ck == 0 and t <= tn <= ck
    n_top = min(TOPK_MAX, (past + t) // 4)
    row = lambda bb: (bb, 0, 0)
    hbm = pl.BlockSpec(memory_space=pl.ANY)
    return pl.pallas_call(
        functools.partial(_dsa_sample_kernel, past=past, t=t, ck=ck, n_top=n_top, n_heads=n_heads,
                          n_idx_heads=n_idx_heads, idx_dim=idx_dim),
        grid=(b,),
        in_specs=[
            pl.BlockSpec((None, t, wa), row),
            pl.BlockSpec((None, t, iq.shape[2]), row),
            pl.BlockSpec((None, t, LANES), row),
            pl.BlockSpec((None, t, wa), row),
            pl.BlockSpec((None, t, wa), row),
            pl.BlockSpec((None, past, idx_dim), row),
            pl.BlockSpec((ck, ck), lambda bb: (0, 0)),
            hbm, hbm,
        ],
        out_specs=pl.BlockSpec((None, t, wa), row),
        out_shape=jax.ShapeDtypeStruct((b, t, wa), BF16),
        scratch_shapes=[
            pltpu.VMEM((tn, wa), BF16), pltpu.VMEM((tn, wa), BF16), pltpu.VMEM((ck, idx_dim), F32),
            pltpu.VMEM((past // ck + 1, t, ck), F32), pltpu.VMEM((n_idx_heads * t, 3 * idx_dim), BF16),
            pltpu.VMEM((n_heads, past, HEAD_DIM), F32), pltpu.VMEM((n_heads, past, HEAD_DIM), F32),
            pltpu.SemaphoreType.DMA((2, n_heads)),
        ],
        compiler_params=_params("arbitrary"),
        name="dsa_sample",
    )(qa, iq, ikw, ka, va, cache_ik, _tri_incl(ck), cache_k, cache_v)


def _sb_sample_kernel(q_ref, kn_ref, vn_ref, ck_ref, cv_ref, m2_ref, o_ref, knew, vnew, acc_scr,
                      *, past, t, tk, n_heads):
    for new_ref, buf in ((kn_ref, knew), (vn_ref, vnew)):
        buf[0:t, :] = new_ref[...]
        buf[t:, :] = jnp.zeros((tk - t, buf.shape[1]), BF16)

    def new_kv(h):
        cols = slice(h * HEAD_DIM, (h + 1) * HEAD_DIM)
        return knew[:, cols], vnew[:, cols]

    def cache_kv(j, h):
        rows = pl.ds(pl.multiple_of(j * tk, tk), tk)
        return ck_ref[rows, h, :].astype(BF16), cv_ref[rows, h, :].astype(BF16)

    _sb_core(q_ref, new_kv, cache_kv, m2_ref, o_ref, acc_scr, past // tk, n_heads, tk)


def _sb_sample(qb, kb, vb, cache_k, cache_v):
    b, t, wb = qb.shape
    past, n_heads = cache_k.shape[1], cache_k.shape[2]
    tk = 128
    assert past % tk == 0 and t <= tk
    row = lambda bb: (bb, 0, 0)
    cache = lambda bb: (bb, 0, 0, 0)
    return pl.pallas_call(
        functools.partial(_sb_sample_kernel, past=past, t=t, tk=tk, n_heads=n_heads),
        grid=(b,),
        in_specs=[
            pl.BlockSpec((None, t, wb), row),
            pl.BlockSpec((None, t, wb), row),
            pl.BlockSpec((None, t, wb), row),
            pl.BlockSpec((None, past, n_heads, HEAD_DIM), cache),
            pl.BlockSpec((None, past, n_heads, HEAD_DIM), cache),
            pl.BlockSpec((2 * tk, tk), lambda bb: (0, 0)),
        ],
        out_specs=pl.BlockSpec((None, t, wb), row),
        out_shape=jax.ShapeDtypeStruct((b, t, wb), BF16),
        scratch_shapes=[pltpu.VMEM((tk, wb), BF16), pltpu.VMEM((tk, wb), BF16), pltpu.VMEM((t, wb), F32)],
        compiler_params=_params("parallel"),
        name="sb_sample",
    )(qb, kb, vb, cache_k, cache_v, _tri_after(tk))


def _out_proj_kernel(y_ref, oa_ref, ob_ref, wa_ref, wb_ref, o_ref):
    acc = jnp.dot(oa_ref[...], wa_ref[...], preferred_element_type=F32)
    acc = acc + jnp.dot(ob_ref[...], wb_ref[...], preferred_element_type=F32)
    o_ref[...] = y_ref[...] + acc


def _out_proj(y, oa, ob, w_a, w_b):
    m, d = y.shape
    tm = min(512, m)
    row = lambda i: (i, 0)
    const = lambda i: (0, 0)
    return pl.pallas_call(
        _out_proj_kernel,
        grid=(m // tm,),
        in_specs=[
            pl.BlockSpec((tm, d), row),
            pl.BlockSpec((tm, oa.shape[1]), row),
            pl.BlockSpec((tm, ob.shape[1]), row),
            pl.BlockSpec(w_a.shape, const),
            pl.BlockSpec(w_b.shape, const),
        ],
        out_specs=pl.BlockSpec((tm, d), row),
        out_shape=jax.ShapeDtypeStruct((m, d), F32),
        compiler_params=_params("parallel"),
        name="out_proj",
    )(y, oa, ob, w_a, w_b)


def _gelu(x):
    return 0.5 * x * (1.0 + jnp.tanh(math.sqrt(2.0 / math.pi) * (x + 0.044715 * (x * x * x))))


def _c_v_kernel(x_ref, g_ref, w_ref, vg_ref, *outs):
    h = _rms(x_ref[...], g_ref[...]).astype(BF16)
    v = _gelu(jnp.dot(h, w_ref[...], preferred_element_type=F32))
    vn = _rms(v, vg_ref[...])
    outs[0][...] = vn.astype(BF16)
    if len(outs) > 1:
        outs[1][...] = vn


def _c_v(x, g, w_v, v_gain, want_f32):
    m, d = x.shape
    cw = w_v.shape[1]
    tm = min(512, m)
    row = lambda i: (i, 0)
    const = lambda i: (0, 0)
    out_specs = [pl.BlockSpec((tm, cw), row)]
    out_shape = [jax.ShapeDtypeStruct((m, cw), BF16)]
    if want_f32:
        out_specs.append(pl.BlockSpec((tm, cw), row))
        out_shape.append(jax.ShapeDtypeStruct((m, cw), F32))
    return pl.pallas_call(
        _c_v_kernel,
        grid=(m // tm,),
        in_specs=[
            pl.BlockSpec((tm, d), row),
            pl.BlockSpec((1, d), const),
            pl.BlockSpec((d, cw), const),
            pl.BlockSpec((1, cw), const),
        ],
        out_specs=out_specs,
        out_shape=out_shape,
        compiler_params=_params("parallel"),
        name="c_v",
    )(x, g.reshape(1, d), w_v, v_gain.reshape(1, cw))


def _c_mix_kernel(x_ref, g_ref, wu_ref, vn_ref, wm_ref, bias_ref, wo_ref, o_ref, h_scr, acc_scr, p_scr,
                  *, nj, gs):
    j = pl.program_id(1)

    @pl.when(j == 0)
    def _():
        h_scr[...] = _rms(x_ref[...], g_ref[...]).astype(BF16)
        acc_scr[...] = jnp.zeros_like(acc_scr)

    u = _gelu(jnp.dot(h_scr[...], wu_ref[...], preferred_element_type=F32))
    tm = u.shape[0]
    for gg in range(gs):
        cs = slice(gg * LANES, (gg + 1) * LANES)
        wm = wm_ref[j * gs + gg]
        for r in range(tm // C_CHUNK):
            rs = slice(r * C_CHUNK, (r + 1) * C_CHUNK)
            mix = jnp.dot(wm, vn_ref[rs, cs], preferred_element_type=F32) + bias_ref[:, cs]
            p_scr[rs, cs] = (u[rs, cs] * mix).astype(BF16)
    acc_scr[...] += jnp.dot(p_scr[...], wo_ref[...], preferred_element_type=F32)

    @pl.when(j == nj - 1)
    def _():
        o_ref[...] = x_ref[...] + acc_scr[...]


def _c_mix(x, g, w_u, vn, w_m, bias, w_o, gs=4):
    m, d = x.shape
    cw = w_u.shape[1]
    ng = cw // LANES
    gs = min(gs, ng)
    nj = ng // gs
    tm = min(512, m)
    row = lambda i, j: (i, 0)
    return pl.pallas_call(
        functools.partial(_c_mix_kernel, nj=nj, gs=gs),
        grid=(m // tm, nj),
        in_specs=[
            pl.BlockSpec((tm, d), row),
            pl.BlockSpec((1, d), lambda i, j: (0, 0)),
            pl.BlockSpec((d, gs * LANES), lambda i, j: (0, j)),
            pl.BlockSpec((tm, gs * LANES), lambda i, j: (i, j)),
            pl.BlockSpec(w_m.shape, lambda i, j: (0, 0, 0)),
            pl.BlockSpec((C_CHUNK, gs * LANES), lambda i, j: (0, j)),
            pl.BlockSpec((gs * LANES, d), lambda i, j: (j, 0)),
        ],
        out_specs=pl.BlockSpec((tm, d), row),
        out_shape=jax.ShapeDtypeStruct((m, d), F32),
        scratch_shapes=[pltpu.VMEM((tm, d), BF16), pltpu.VMEM((tm, d), F32),
                        pltpu.VMEM((tm, gs * LANES), BF16)],
        compiler_params=_params("parallel", "arbitrary"),
        name="c_mix",
    )(x, g.reshape(1, d), w_u, vn, w_m, bias, w_o)


def _rope_tables(pos, reps):
    pos = pos.astype(F32)[:, None]

    def tab(half, copies):
        inv = ROPE_THETA ** (-jnp.arange(half, dtype=F32) / half)
        ang = pos * inv[None, :]
        c, s = jnp.cos(ang), jnp.sin(ang)
        return jnp.tile(jnp.concatenate([c, c], axis=1), (reps, copies)), \
            jnp.tile(jnp.concatenate([-s, s], axis=1), (reps, copies))

    c128, s128 = tab(HEAD_DIM // 2, 1)
    c64, s64 = tab(32, 2)
    return c128, s128, c64, s64


def kernel(x_prompt, x_sample, cache_a_k, cache_a_v, cache_a_ik, cache_b_k, cache_b_v, norm_ff1, ff1_w1, ff1_w3, ff1_w2, norm_mix, norm_ff2, ff2_w1, ff2_w3, ff2_w2, ab_w_in, ab_w_out, c_w_in, c_v_norm, c_w_s, c_b_s, c_w_out, final_norm):
    bp, seq, d = x_prompt.shape
    bs, t, _ = x_sample.shape
    past, ha = cache_a_k.shape[2], cache_a_k.shape[3]
    hb = cache_b_k.shape[3]
    idx_dim = cache_a_ik.shape[3]
    wa, wb = ha * HEAD_DIM, hb * HEAD_DIM
    n_idx_heads = (ab_w_in.shape[2] - 3 * wa - 3 * wb - idx_dim) // (idx_dim + 1)
    wi = n_idx_heads * idx_dim
    assert idx_dim == 64 and wi % LANES == 0 and wi <= wa and n_idx_heads <= LANES - idx_dim
    depth = norm_ff1.shape[0]
    mp, ms = bp * seq, bs * t

    yp = x_prompt.reshape(mp, d)
    ys = x_sample.reshape(ms, d)
    bf = lambda w: w.astype(BF16)

    tabs_p = _rope_tables(jnp.arange(seq), 1)
    tabs_s = _rope_tables(past + jnp.arange(t), min(512, ms) // t)

    outs_p, outs_s, s_cv = [], [], []
    for layer in range(depth):
        j = layer // 2
        w1, w3, w2 = bf(ff1_w1[layer]), bf(ff1_w3[layer]), bf(ff1_w2[layer])
        yp = _half_ffn(yp, norm_ff1[layer], w1, w3, w2)
        ys = _half_ffn(ys, norm_ff1[layer], w1, w3, w2)
        if layer % 2 == 0:
            w_in = ab_w_in[j]
            o = 0
            cols = []
            for width in (wa, wa, wa, wi, idx_dim, n_idx_heads, wb, wb, wb):
                cols.append(w_in[:, o:o + width])
                o += width
            w_qa, w_ka, w_va, w_iq, w_ik, w_iw, w_qb, w_kb, w_vb = cols
            pad = lambda w, width: jnp.pad(w, ((0, 0), (0, width - w.shape[1])))
            w_pa = bf(jnp.concatenate([w_qa, w_ka, w_va, pad(w_iq, wa)], axis=1))
            w_kw = bf(pad(jnp.concatenate([w_ik, w_iw], axis=1), LANES))
            w_pb = bf(jnp.concatenate([w_qb, w_kb, w_vb], axis=1))
            w_oa, w_ob = bf(ab_w_out[j][:wa]), bf(ab_w_out[j][wa:])

            def mixer(y, tabs, n_pos_rows):
                qa, kaf, kab, vaf, vab, iq, ikw = _proj_a(
                    y, norm_mix[layer], w_pa, w_kw, tabs, n_pos_rows, ha, n_idx_heads, idx_dim)
                qb, kbf, kbb, vbf, vbb = _proj_b(y, norm_mix[layer], w_pb)
                return (qa, kaf, kab, vaf, vab, iq, ikw, qb, kbf, kbb, vbf, vbb)

            qa, kaf, kab, vaf, vab, iq, ikw, qb, kbf, kbb, vbf, vbb = mixer(yp, tabs_p, seq)
            r3 = lambda a: a.reshape(bp, seq, a.shape[-1])
            ik = ikw[:, :idx_dim]
            o_a = _dsa_prompt_t(r3(qa), r3(iq), r3(ikw), r3(kab), r3(vab), ha, n_idx_heads, idx_dim)
            o_b = _sb_prompt(r3(qb), r3(kbb), r3(vbb), hb)
            yp = _out_proj(yp, o_a.reshape(mp, wa), o_b.reshape(mp, wb), w_oa, w_ob)
            outs_p.append((kaf.reshape(bp, seq, ha, HEAD_DIM), vaf.reshape(bp, seq, ha, HEAD_DIM),
                           ik.reshape(bp, seq, idx_dim),
                           kbf.reshape(bp, seq, hb, HEAD_DIM), vbf.reshape(bp, seq, hb, HEAD_DIM)))

            qa, kaf, kab, vaf, vab, iq, ikw, qb, kbf, kbb, vbf, vbb = mixer(ys, tabs_s, min(512, ms))
            r3 = lambda a: a.reshape(bs, t, a.shape[-1])
            o_a = _dsa_sample(r3(qa), r3(iq), r3(ikw), r3(kab), r3(vab),
                              cache_a_k[j], cache_a_v[j], cache_a_ik[j], n_idx_heads)
            o_b = _sb_sample(r3(qb), r3(kbb), r3(vbb), cache_b_k[j], cache_b_v[j])
            ys = _out_proj(ys, o_a.reshape(ms, wa), o_b.reshape(ms, wb), w_oa, w_ob)
            outs_s.append((kaf.reshape(bs, t, ha, HEAD_DIM), vaf.reshape(bs, t, ha, HEAD_DIM),
                           ikw[:, :idx_dim].reshape(bs, t, idx_dim),
                           kbf.reshape(bs, t, hb, HEAD_DIM), vbf.reshape(bs, t, hb, HEAD_DIM)))
        else:
            cw = c_w_in.shape[2] // 2
            ng = c_w_s.shape[1]
            w_u, w_v = bf(c_w_in[j][:, :cw]), bf(c_w_in[j][:, cw:])
            w_o = bf(c_w_out[j])
            i = jnp.arange(C_CHUNK)
            mask = (i[None, :] // CHUNK) <= (i[:, None] // CHUNK)
            w_m = jnp.where(mask[None], c_w_s[j], 0.0)
            bias_p = jnp.repeat(c_b_s[j].T, cw // ng, axis=1)
            per = C_CHUNK // t
            w_ms = jnp.einsum('ab,gij->gaibj', jnp.eye(per, dtype=F32), w_m[:, :t, :t]).reshape(ng, C_CHUNK, C_CHUNK)
            bias_s = jnp.tile(bias_p[:t], (per, 1))

            vn = _c_v(yp, norm_mix[layer], w_v, c_v_norm[j], False)[0]
            yp = _c_mix(yp, norm_mix[layer], w_u, vn, bf(w_m), bias_p, w_o)
            vn, vn_f32 = _c_v(ys, norm_mix[layer], w_v, c_v_norm[j], True)
            ys = _c_mix(ys, norm_mix[layer], w_u, vn, bf(w_ms), bias_s, w_o)
            s_cv.append(vn_f32.reshape(bs, t, cw))
        last = layer == depth - 1
        w1, w3, w2 = bf(ff2_w1[layer]), bf(ff2_w3[layer]), bf(ff2_w2[layer])
        fg = final_norm if last else None
        yp = _half_ffn(yp, norm_ff2[layer], w1, w3, w2, fg)
        ys = _half_ffn(ys, norm_ff2[layer], w1, w3, w2, fg)

    stack = lambda outs, k: jnp.stack([o[k] for o in outs])
    return (yp.reshape(bp, seq, d), ys.reshape(bs, t, d),
            stack(outs_p, 0), stack(outs_p, 1), stack(outs_p, 2), stack(outs_p, 3), stack(outs_p, 4),
            stack(outs_s, 0), stack(outs_s, 1), stack(outs_s, 2), stack(outs_s, 3), stack(outs_s, 4),
            jnp.stack(s_cv))
```

```python
import functools
import math

import jax
import jax.numpy as jnp
from jax import lax
from jax.experimental import pallas as pl
from jax.experimental.pallas import tpu as pltpu

F32 = jnp.float32
BF16 = jnp.bfloat16

RMS_EPS = 1e-6
CHUNK = 64
TOPK_MAX = 256
ROPE_THETA = 10000.0
C_CHUNK = 128
LANES = 128
HEAD_DIM = 128
NEG = -1e30
INT_MIN = -2147483648
SB_DEAD = -105.0
DSA_CHUNK = 256
VMEM_LIMIT_BYTES = 56 * 1024 * 1024

NT_DIMS = (((1,), (1,)), ((), ()))


def _params(*sem):
    return pltpu.CompilerParams(dimension_semantics=sem, vmem_limit_bytes=VMEM_LIMIT_BYTES)


def _rms(x, g):
    return x * lax.rsqrt(jnp.mean(x * x, axis=-1, keepdims=True) + RMS_EPS) * g


def _split_bf16(x):
    hi = x.astype(BF16)
    lo = (x - hi.astype(F32)).astype(BF16)
    return hi, lo


def _ffn_kernel(x_ref, g_ref, w1_ref, w3_ref, w2_ref, *rest, nf, final_norm, emit_bf16):
    rest = list(rest)
    gf_ref = rest.pop(0) if final_norm else None
    o_ref = rest.pop(0)
    wb_refs = [rest.pop(0) for _ in range(3)] if emit_bf16 else None
    h_scr, acc_scr = rest
    f = pl.program_id(1)

    @pl.when(f == 0)
    def _():
        h_scr[...] = _rms(x_ref[...], g_ref[...]).astype(BF16)
        acc_scr[...] = jnp.zeros_like(acc_scr)

    w1, w3, w2 = w1_ref[...], w3_ref[...], w2_ref[...]
    if emit_bf16:
        w1, w3, w2 = w1.astype(BF16), w3.astype(BF16), w2.astype(BF16)
        for ref, w in zip(wb_refs, (w1, w3, w2)):
            ref[...] = w
    h = h_scr[...]
    a = jnp.dot(h, w1, preferred_element_type=F32)
    b = jnp.dot(h, w3, preferred_element_type=F32)
    p = (a * jax.nn.sigmoid(a) * b).astype(BF16)
    acc_scr[...] += jnp.dot(p, w2, preferred_element_type=F32)

    @pl.when(f == nf - 1)
    def _():
        y = x_ref[...] + 0.5 * acc_scr[...]
        if final_norm:
            y = _rms(y, gf_ref[...])
        o_ref[...] = y


def _half_ffn(x, g, w1, w3, w2, final_g=None, emit_bf16=False):
    m, d = x.shape
    dff = w1.shape[1]
    tm = min(512, m)
    tf = 256 if emit_bf16 else 512
    assert dff % tf == 0 and (not emit_bf16 or m == tm)
    nf = dff // tf
    w13_spec = pl.BlockSpec((d, tf), lambda i, f: (0, f))
    w2_spec = pl.BlockSpec((tf, d), lambda i, f: (f, 0))
    in_specs = [
        pl.BlockSpec((tm, d), lambda i, f: (i, 0)),
        pl.BlockSpec((1, d), lambda i, f: (0, 0)),
        w13_spec, w13_spec, w2_spec,
    ]
    args = [x, g.reshape(1, d), w1, w3, w2]
    if final_g is not None:
        in_specs.append(pl.BlockSpec((1, d), lambda i, f: (0, 0)))
        args.append(final_g.reshape(1, d))
    out_specs = [pl.BlockSpec((tm, d), lambda i, f: (i, 0))]
    out_shape = [jax.ShapeDtypeStruct((m, d), F32)]
    if emit_bf16:
        out_specs += [w13_spec, w13_spec, w2_spec]
        out_shape += [jax.ShapeDtypeStruct(w.shape, BF16) for w in (w1, w3, w2)]
    outs = pl.pallas_call(
        functools.partial(_ffn_kernel, nf=nf, final_norm=final_g is not None, emit_bf16=emit_bf16),
        grid=(m // tm, nf),
        in_specs=in_specs,
        out_specs=out_specs,
        out_shape=out_shape,
        scratch_shapes=[pltpu.VMEM((tm, d), BF16), pltpu.VMEM((tm, d), F32)],
        compiler_params=_params("parallel", "arbitrary"),
        name="half_ffn",
    )(*args)
    return outs if emit_bf16 else outs[0]


def _rope_heads(z, cos, sin):
    outs = []
    for h in range(z.shape[1] // LANES):
        zh = z[:, h * LANES:(h + 1) * LANES]
        outs.append(zh * cos + pltpu.roll(zh, LANES // 2, axis=1) * sin)
    return outs


def _rope_pairs(z, cos, sin):
    lane = lax.broadcasted_iota(jnp.int32, (z.shape[0], LANES), 1)
    low = (lane % 64) < 32
    outs = []
    for h in range(z.shape[1] // LANES):
        zh = z[:, h * LANES:(h + 1) * LANES]
        partner = jnp.where(low, pltpu.roll(zh, LANES - 32, axis=1), pltpu.roll(zh, 32, axis=1))
        outs.append(zh * cos + partner * sin)
    return outs


def _proj_a_kernel(x_ref, g_ref, w_ref, wkw_ref, c128_ref, s128_ref, c64_ref, s64_ref,
                   qa_ref, kaf_ref, kab_ref, vaf_ref, vab_ref, iq_ref, ikw_ref, h_scr,
                   *, n_idx_heads, idx_dim):
    j = pl.program_id(1)

    @pl.when(j == 0)
    def _():
        h_scr[...] = _rms(x_ref[...], g_ref[...]).astype(BF16)

    z = jnp.dot(h_scr[...], w_ref[...], preferred_element_type=F32)
    wa = qa_ref.shape[1]

    @pl.when(j == 0)
    def _():
        scale = HEAD_DIM ** -0.5
        for h, r in enumerate(_rope_heads(z[:, :wa], c128_ref[...], s128_ref[...])):
            qa_ref[:, h * LANES:(h + 1) * LANES] = (r * scale).astype(BF16)

    @pl.when(j == 1)
    def _():
        for h, r in enumerate(_rope_heads(z[:, :wa], c128_ref[...], s128_ref[...])):
            kaf_ref[:, h * LANES:(h + 1) * LANES] = r
            kab_ref[:, h * LANES:(h + 1) * LANES] = r.astype(BF16)

    @pl.when(j == 2)
    def _():
        kv = z[:, :wa]
        vaf_ref[...] = kv
        vab_ref[...] = kv.astype(BF16)

    @pl.when(j == 3)
    def _():
        wi = iq_ref.shape[1]
        for h, r in enumerate(_rope_pairs(z[:, :wi], c64_ref[...], s64_ref[...])):
            iq_ref[:, h * LANES:(h + 1) * LANES] = r * (idx_dim ** -0.5)
        zz = jnp.dot(h_scr[...], wkw_ref[...], preferred_element_type=F32)
        r = _rope_pairs(zz, c64_ref[...], s64_ref[...])[0]
        lane = lax.broadcasted_iota(jnp.int32, zz.shape, 1)
        ikw_ref[...] = jnp.where(lane < idx_dim, r, zz * (n_idx_heads ** -0.5))


def _proj_a(x, g, w, w_kw, tabs, n_pos_rows, ha, n_idx_heads, idx_dim):
    m, d = x.shape
    tm = min(512, m)
    wa = ha * HEAD_DIM
    wi = n_idx_heads * idx_dim
    tn = w.shape[1] // 4
    npb = n_pos_rows // tm
    row = lambda i, j: (i, 0)
    tab = lambda i, j: (i % npb, 0)
    tab_spec = pl.BlockSpec((tm, LANES), tab)
    return pl.pallas_call(
        functools.partial(_proj_a_kernel, n_idx_heads=n_idx_heads, idx_dim=idx_dim),
        grid=(m // tm, 4),
        in_specs=[
            pl.BlockSpec((tm, d), row),
            pl.BlockSpec((1, d), lambda i, j: (0, 0)),
            pl.BlockSpec((d, tn), lambda i, j: (0, j)),
            pl.BlockSpec((d, LANES), lambda i, j: (0, 0)),
            tab_spec, tab_spec, tab_spec, tab_spec,
        ],
        out_specs=[
            pl.BlockSpec((tm, wa), row), pl.BlockSpec((tm, wa), row), pl.BlockSpec((tm, wa), row),
            pl.BlockSpec((tm, wa), row), pl.BlockSpec((tm, wa), row),
            pl.BlockSpec((tm, wi), row), pl.BlockSpec((tm, LANES), row),
        ],
        out_shape=[
            jax.ShapeDtypeStruct((m, wa), BF16), jax.ShapeDtypeStruct((m, wa), F32),
            jax.ShapeDtypeStruct((m, wa), BF16), jax.ShapeDtypeStruct((m, wa), F32),
            jax.ShapeDtypeStruct((m, wa), BF16),
            jax.ShapeDtypeStruct((m, wi), F32), jax.ShapeDtypeStruct((m, LANES), F32),
        ],
        scratch_shapes=[pltpu.VMEM((tm, d), BF16)],
        compiler_params=_params("parallel", "arbitrary"),
        name="proj_a",
    )(x, g.reshape(1, d), w, w_kw, *tabs)


def _proj_a_t_kernel(x_ref, g_ref, w_ref, wkw_ref, c128_ref, s128_ref, c64_ref, s64_ref,
                     qt_ref, kaf_ref, kab_ref, vaf_ref, vt_ref, iq3t_ref, ikw_ref, ik3_ref, iwt_ref, h_scr,
                     *, n_idx_heads, idx_dim, tq, ck):
    j = pl.program_id(1)

    @pl.when(j == 0)
    def _():
        h_scr[...] = _rms(x_ref[...], g_ref[...]).astype(BF16)

    z = jnp.dot(h_scr[...], w_ref[...], preferred_element_type=F32)
    tm = z.shape[0]
    wa = kaf_ref.shape[1]

    @pl.when(j == 0)
    def _():
        scale = HEAD_DIM ** -0.5
        for h, r in enumerate(_rope_heads(z[:, :wa], c128_ref[...], s128_ref[...])):
            qt_ref[h * LANES:(h + 1) * LANES, :] = (r * scale).T.astype(BF16)

    @pl.when(j == 1)
    def _():
        for h, r in enumerate(_rope_heads(z[:, :wa], c128_ref[...], s128_ref[...])):
            kaf_ref[:, h * LANES:(h + 1) * LANES] = r
            kab_ref[:, h * LANES:(h + 1) * LANES] = r.astype(BF16)

    @pl.when(j == 2)
    def _():
        vaf_ref[...] = z[:, :wa]
        for h in range(wa // LANES):
            for c in range(tm // ck):
                vt_ref[c, h * LANES:(h + 1) * LANES, :] = \
                    z[c * ck:(c + 1) * ck, h * LANES:(h + 1) * LANES].T.astype(BF16)

    @pl.when(j == 3)
    def _():
        wi = n_idx_heads * idx_dim
        for g, r in enumerate(_rope_pairs(z[:, :wi], c64_ref[...], s64_ref[...])):
            r = r * (idx_dim ** -0.5)
            hi = r.astype(BF16).astype(F32)
            hi_t, lo_t = hi.T.astype(BF16), (r - hi).T.astype(BF16)
            for u in range(2):
                rows = slice(u * idx_dim, (u + 1) * idx_dim)
                for qb in range(tm // tq):
                    cols = slice(qb * tq, (qb + 1) * tq)
                    dst = slice((2 * g + u) * tq, (2 * g + u + 1) * tq)
                    iq3t_ref[qb, 0:idx_dim, dst] = hi_t[rows, cols]
                    iq3t_ref[qb, idx_dim:2 * idx_dim, dst] = lo_t[rows, cols]
                    iq3t_ref[qb, 2 * idx_dim:3 * idx_dim, dst] = hi_t[rows, cols]
        zz = jnp.dot(h_scr[...], wkw_ref[...], preferred_element_type=F32)
        r = _rope_pairs(zz, c64_ref[...], s64_ref[...])[0]
        lane = lax.broadcasted_iota(jnp.int32, zz.shape, 1)
        val = jnp.where(lane < idx_dim, r, zz * (n_idx_heads ** -0.5))
        ikw_ref[...] = val
        ik = val[:, 0:idx_dim]
        ik_hi = ik.astype(BF16).astype(F32)
        ik3_ref[...] = jnp.concatenate([ik_hi, ik_hi, ik - ik_hi], axis=1).astype(BF16)
        val_t = val.T
        for qb in range(tm // tq):
            iwt_ref[qb] = val_t[idx_dim:idx_dim + n_idx_heads, qb * tq:(qb + 1) * tq]


def _proj_a_t(x, g, w, w_kw, tabs, n_pos_rows, ha, n_idx_heads, idx_dim, tq, ck):
    m, d = x.shape
    tm = min(512, m)
    wa = ha * HEAD_DIM
    assert n_idx_heads * idx_dim <= wa and n_idx_heads % 2 == 0 and tm % ck == 0 and tm % tq == 0
    tn = w.shape[1] // 4
    npb = n_pos_rows // tm
    row = lambda i, j: (i, 0)
    lead = lambda i, j: (i, 0, 0)
    tab_spec = pl.BlockSpec((tm, LANES), lambda i, j: (i % npb, 0))
    return pl.pallas_call(
        functools.partial(_proj_a_t_kernel, n_idx_heads=n_idx_heads, idx_dim=idx_dim, tq=tq, ck=ck),
        grid=(m // tm, 4),
        in_specs=[
            pl.BlockSpec((tm, d), row),
            pl.BlockSpec((1, d), lambda i, j: (0, 0)),
            pl.BlockSpec((d, tn), lambda i, j: (0, j)),
            pl.BlockSpec((d, LANES), lambda i, j: (0, 0)),
            tab_spec, tab_spec, tab_spec, tab_spec,
        ],
        out_specs=[
            pl.BlockSpec((wa, tm), lambda i, j: (0, i)),
            pl.BlockSpec((tm, wa), row), pl.BlockSpec((tm, wa), row), pl.BlockSpec((tm, wa), row),
            pl.BlockSpec((tm // ck, wa, ck), lead),
            pl.BlockSpec((tm // tq, 3 * idx_dim, n_idx_heads * tq), lead),
            pl.BlockSpec((tm, LANES), row), pl.BlockSpec((tm, 3 * idx_dim), row),
            pl.BlockSpec((tm // tq, n_idx_heads, tq), lead),
        ],
        out_shape=[
            jax.ShapeDtypeStruct((wa, m), BF16),
            jax.ShapeDtypeStruct((m, wa), F32), jax.ShapeDtypeStruct((m, wa), BF16),
            jax.ShapeDtypeStruct((m, wa), F32),
            jax.ShapeDtypeStruct((m // ck, wa, ck), BF16),
            jax.ShapeDtypeStruct((m // tq, 3 * idx_dim, n_idx_heads * tq), BF16),
            jax.ShapeDtypeStruct((m, LANES), F32), jax.ShapeDtypeStruct((m, 3 * idx_dim), BF16),
            jax.ShapeDtypeStruct((m // tq, n_idx_heads, tq), F32),
        ],
        scratch_shapes=[pltpu.VMEM((tm, d), BF16)],
        compiler_params=_params("parallel", "arbitrary"),
        name="proj_a_t",
    )(x, g.reshape(1, d), w, w_kw, *tabs)


def _proj_b_kernel(x_ref, g_ref, w_ref, qb_ref, kbf_ref, kbb_ref, vbf_ref, vbb_ref, h_scr):
    j = pl.program_id(1)

    @pl.when(j == 0)
    def _():
        h_scr[...] = _rms(x_ref[...], g_ref[...]).astype(BF16)

    z = jnp.dot(h_scr[...], w_ref[...], preferred_element_type=F32)

    @pl.when(j == 0)
    def _():
        qb_ref[...] = (z * (HEAD_DIM ** -0.5)).astype(BF16)

    @pl.when(j == 1)
    def _():
        kbf_ref[...] = z
        kbb_ref[...] = z.astype(BF16)

    @pl.when(j == 2)
    def _():
        vbf_ref[...] = z
        vbb_ref[...] = z.astype(BF16)


def _proj_b(x, g, w):
    m, d = x.shape
    tm = min(512, m)
    wb = w.shape[1] // 3
    row = lambda i, j: (i, 0)
    return pl.pallas_call(
        _proj_b_kernel,
        grid=(m // tm, 3),
        in_specs=[
            pl.BlockSpec((tm, d), row),
            pl.BlockSpec((1, d), lambda i, j: (0, 0)),
            pl.BlockSpec((d, wb), lambda i, j: (0, j)),
        ],
        out_specs=[pl.BlockSpec((tm, wb), row)] * 5,
        out_shape=[
            jax.ShapeDtypeStruct((m, wb), BF16), jax.ShapeDtypeStruct((m, wb), F32),
            jax.ShapeDtypeStruct((m, wb), BF16), jax.ShapeDtypeStruct((m, wb), F32),
            jax.ShapeDtypeStruct((m, wb), BF16),
        ],
        scratch_shapes=[pltpu.VMEM((tm, d), BF16)],
        compiler_params=_params("parallel", "arbitrary"),
        name="proj_b",
    )(x, g.reshape(1, d), w)


def _key_to_f32(key):
    bits = jnp.where(key >= 0, key, key ^ jnp.int32(0x7FFFFFFF))
    return lax.bitcast_convert_type(bits, F32)


def _tree(parts, op):
    while len(parts) > 1:
        parts = [op(parts[k], parts[k + 1]) if k + 1 < len(parts) else parts[k] for k in range(0, len(parts), 2)]
    return parts[0]


def _fold_lanes(x, op):
    return _tree([x[:, t * LANES:(t + 1) * LANES] for t in range(x.shape[1] // LANES)], op)


def _dsa_mask(iq, iw, ik3_main, ik3_tail, tri_ref, s_scr, iq3_scr, limit, n_main, ck, n_top,
              n_idx_heads, idx_dim):
    tq = iq.shape[0]
    n_chunks = n_main + 1

    for h in range(n_idx_heads):
        x = iq[:, h * idx_dim:(h + 1) * idx_dim]
        hi = x.astype(BF16).astype(F32)
        iq3_scr[h * tq:(h + 1) * tq, :] = jnp.concatenate([hi, x - hi, hi], axis=1).astype(BF16)

    col0 = lax.broadcasted_iota(jnp.int32, (tq, ck), 1)

    def score_chunk(c, ikc):
        rel = lax.dot_general(iq3_scr[...], ikc, NT_DIMS, preferred_element_type=F32)
        acc = iw[:, 0:1] * jnp.maximum(rel[0:tq], 0.0)
        for h in range(1, n_idx_heads):
            acc = acc + iw[:, h:h + 1] * jnp.maximum(rel[h * tq:(h + 1) * tq], 0.0)
        s_scr[c] = jnp.where(col0 + c * ck < limit, acc, -jnp.inf)

    def score_main(c, _):
        score_chunk(c, ik3_main(c))
        return 0

    lax.fori_loop(0, n_main, score_main, 0)
    score_chunk(n_main, ik3_tail())

    def count(pred_fn):
        def body(c, acc):
            return acc + _fold_lanes(jnp.where(pred_fn(s_scr[c]), 1.0, 0.0), jnp.add)
        acc = lax.fori_loop(0, n_chunks, body, jnp.zeros((tq, LANES), F32))
        return jnp.sum(acc, axis=1, keepdims=True)

    kf = jnp.float32(n_top)
    cnt = count(lambda s: s >= 0.0)
    key = jnp.where(cnt >= kf, jnp.int32(0), jnp.int32(INT_MIN))

    def bit_step(b, key):
        cand = key | jnp.left_shift(jnp.int32(1), 30 - b)
        cand_f = _key_to_f32(cand)
        cnt = count(lambda s: s >= cand_f)
        return jnp.where(cnt >= kf, cand, key)

    key = lax.fori_loop(0, 31, bit_step, key)
    thr = _key_to_f32(key)
    need = kf - count(lambda s: s > thr)
    take_all = limit <= n_top

    def bias_chunk(c, carry):
        s = s_scr[c]
        eq = s == thr
        pre = jnp.dot(jnp.where(eq, 1.0, 0.0).astype(BF16), tri_ref[...], preferred_element_type=F32)
        tied = jnp.where((carry + pre) <= need, 0.0, NEG)
        bias = jnp.where(eq, tied, jnp.where(s > thr, 0.0, NEG))
        s_scr[c] = jnp.where(take_all, jnp.where(s > -jnp.inf, 0.0, NEG), bias)
        return carry + pre[:, ck - 1:ck]

    lax.fori_loop(0, n_chunks, bias_chunk, jnp.zeros((tq, 1), F32))


def _tri_incl(n):
    r = jnp.arange(n)
    return (r[:, None] <= r[None, :]).astype(BF16)


def _fold_rows(x, op):
    return _tree([x[r * 8:(r + 1) * 8] for r in range(x.shape[0] // 8)], op)


def _topk_bias_t(s_scr, n_chunks, limit, n_top, ltri_ref):
    ck, nl = s_scr.shape[1], s_scr.shape[2]

    def count(pred_fn):
        def body(c, acc):
            return acc + _fold_rows(jnp.where(pred_fn(s_scr[c]), 1.0, 0.0), jnp.add)
        acc = lax.fori_loop(0, n_chunks, body, jnp.zeros((8, nl), F32))
        return jnp.sum(acc, axis=0, keepdims=True)

    kf = jnp.float32(n_top)
    cnt = count(lambda s: s >= 0.0)
    key = jnp.where(cnt >= kf, jnp.int32(0), jnp.int32(INT_MIN))

    def bit_step(b, key):
        cand = key | jnp.left_shift(jnp.int32(1), 30 - b)
        cand_f = _key_to_f32(cand)
        cnt = count(lambda s: s >= cand_f)
        return jnp.where(cnt >= kf, cand, key)

    key = lax.fori_loop(0, 31, bit_step, key)
    thr = _key_to_f32(key)
    need = kf - count(lambda s: s > thr)
    take_all = limit <= n_top

    def bias_chunk(c, carry):
        s = s_scr[c]
        eq = s == thr
        pre = jnp.dot(ltri_ref[...], jnp.where(eq, 1.0, 0.0).astype(BF16), preferred_element_type=F32)
        tied = jnp.where((carry + pre) <= need, 0.0, NEG)
        bias = jnp.where(eq, tied, jnp.where(s > thr, 0.0, NEG))
        s_scr[c] = jnp.where(take_all, jnp.where(s > -jnp.inf, 0.0, NEG), bias)
        return carry + pre[ck - 1:ck, :]

    lax.fori_loop(0, n_chunks, bias_chunk, jnp.zeros((1, nl), F32))


def _dsa_prompt_t_kernel(qt_ref, iq3t_ref, iwt_ref, ik3_ref, k_ref, vt_ref, ltri_ref, o_ref,
                         s_scr, acc_scr, lg_a, lg_b,
                         *, tq, ck, n_top, n_heads, n_idx_heads):
    i = pl.program_id(1)
    pos = i * tq + lax.broadcasted_iota(jnp.int32, (1, tq), 1)
    limit = (pos // CHUNK + 1) * CHUNK
    n_chunks = ((i + 1) * tq + ck - 1) // ck
    row0 = lax.broadcasted_iota(jnp.int32, (ck, tq), 0)
    iwt = iwt_ref[...]

    def rows(c):
        return pl.ds(pl.multiple_of(c * ck, ck), ck)

    def score_chunk(c, _):
        ikc = ik3_ref[rows(c), :]
        acc = None
        for g in range(n_idx_heads // 2):
            rel = jnp.dot(ikc, iq3t_ref[:, 2 * g * tq:2 * (g + 1) * tq], preferred_element_type=F32)
            for u in range(2):
                h = 2 * g + u
                term = iwt[h:h + 1, :] * jnp.maximum(rel[:, u * tq:(u + 1) * tq], 0.0)
                acc = term if acc is None else acc + term
        s_scr[c] = jnp.where(row0 + c * ck < limit, acc, -jnp.inf)
        return 0

    lax.fori_loop(0, n_chunks, score_chunk, 0)
    _topk_bias_t(s_scr, n_chunks, limit, n_top, ltri_ref)

    heads = [slice(h * HEAD_DIM, (h + 1) * HEAD_DIM) for h in range(n_heads)]
    acc_scr[...] = jnp.zeros_like(acc_scr)

    @pl.when(n_chunks % 2 == 1)
    def _():
        s_scr[n_chunks] = jnp.full((ck, tq), NEG, F32)

    def logits(c, buf):
        for h, hs in enumerate(heads):
            buf[h] = jnp.dot(k_ref[rows(c), hs], qt_ref[hs, :], preferred_element_type=F32)

    def attend(c, buf, ms, ls):
        bias = s_scr[c]
        new_m, new_l = [], []
        for h, hs in enumerate(heads):
            lg = buf[h] + bias
            m_new = jnp.maximum(ms[h], jnp.max(_fold_rows(lg, jnp.maximum), axis=0, keepdims=True))
            alpha = jnp.exp(ms[h] - m_new)
            p = jnp.exp(lg - m_new)
            new_l.append(alpha * ls[h] + _fold_rows(p, jnp.add))
            acc_scr[hs, :] = alpha * acc_scr[hs, :] + jnp.dot(vt_ref[c, hs, :], p.astype(BF16),
                                                              preferred_element_type=F32)
            new_m.append(m_new)
        return tuple(new_m), tuple(new_l)

    n_pairs = (n_chunks + 1) // 2
    last = 2 * n_pairs - 1

    def att_pair(pr, state):
        c0 = 2 * pr
        logits(c0 + 1, lg_b)
        state = attend(c0, lg_a, *state)
        logits(jnp.minimum(c0 + 2, last), lg_a)
        return attend(c0 + 1, lg_b, *state)

    logits(0, lg_a)
    init = (tuple(jnp.full((1, tq), NEG, F32) for _ in heads), tuple(jnp.zeros((8, tq), F32) for _ in heads))
    _, ls = lax.fori_loop(0, n_pairs, att_pair, init)
    for h, hs in enumerate(heads):
        o_t = acc_scr[hs, :] / jnp.sum(ls[h], axis=0, keepdims=True)
        o_ref[:, hs] = o_t.T.astype(o_ref.dtype)


def _dsa_prompt_t(qt, iq3t, iwt, ik3, ka, vt, b, tq, ck):
    wa, m = qt.shape
    s = m // b
    nq = s // tq
    n_heads = wa // HEAD_DIM
    n_idx_heads = iwt.shape[1]
    idx_dim = ik3.shape[1] // 3
    n_top = min(TOPK_MAX, s // 4)
    assert s % (2 * ck) == 0
    ik3 = ik3.reshape(b, s, 3 * idx_dim)
    ka = ka.reshape(b, s, wa)
    vt = vt.reshape(b, s // ck, wa, ck)
    r = jnp.arange(ck)
    ltri = (r[None, :] <= r[:, None]).astype(BF16)

    return pl.pallas_call(
        functools.partial(_dsa_prompt_t_kernel, tq=tq, ck=ck, n_top=n_top, n_heads=n_heads,
                          n_idx_heads=n_idx_heads),
        grid=(b, nq),
        in_specs=[
            pl.BlockSpec((wa, tq), lambda bb, i: (0, bb * nq + i)),
            pl.BlockSpec((None, 3 * idx_dim, n_idx_heads * tq), lambda bb, i: (bb * nq + i, 0, 0)),
            pl.BlockSpec((None, n_idx_heads, tq), lambda bb, i: (bb * nq + i, 0, 0)),
            pl.BlockSpec((None, s, 3 * idx_dim), lambda bb, i: (bb, 0, 0)),
            pl.BlockSpec((None, s, wa), lambda bb, i: (bb, 0, 0)),
            pl.BlockSpec((None, s // ck, wa, ck), lambda bb, i: (bb, 0, 0, 0)),
            pl.BlockSpec((ck, ck), lambda bb, i: (0, 0)),
        ],
        out_specs=pl.BlockSpec((None, tq, wa), lambda bb, i: (bb, i, 0)),
        out_shape=jax.ShapeDtypeStruct((b, s, wa), BF16),
        scratch_shapes=[pltpu.VMEM((s // ck, ck, tq), F32), pltpu.VMEM((wa, tq), F32),
                        pltpu.VMEM((n_heads, ck, tq), F32), pltpu.VMEM((n_heads, ck, tq), F32)],
        compiler_params=_params("parallel", "arbitrary"),
        name="dsa_prompt",
    )(qt, iq3t, iwt, ik3, ka, vt, ltri)


def _sb_tile(q_h, kt, vt, m2, carry, vis):
    z = lax.dot_general(q_h, kt, NT_DIMS, preferred_element_type=F32)
    sp = jnp.maximum(z, 0.0) + jnp.log1p(jnp.exp(-jnp.abs(z)))
    lk = -sp if vis is None else jnp.where(vis, -sp, 0.0)
    hi, lo = _split_bf16(lk)
    after = jnp.dot(jnp.concatenate([hi, lo], axis=1), m2, preferred_element_type=F32)
    a = jnp.exp(z - sp + after + carry)
    if vis is not None:
        a = jnp.where(vis, a, 0.0)
    return jnp.sum(lk, axis=1, keepdims=True), jnp.dot(a.astype(BF16), vt, preferred_element_type=F32)


def _sb_core(q_ref, diag_kv, past_kv, m2_ref, o_ref, acc_scr, n_past, n_heads, tk):
    tq = q_ref.shape[0]
    vis = lax.broadcasted_iota(jnp.int32, (tq, tk), 1) < lax.broadcasted_iota(jnp.int32, (tq, tk), 0)
    m2 = m2_ref[...]
    heads = [slice(h * HEAD_DIM, (h + 1) * HEAD_DIM) for h in range(n_heads)]

    carries = []
    for h, hs in enumerate(heads):
        kt, vt = diag_kv(h)
        dc, contrib = _sb_tile(q_ref[:, hs], kt, vt, m2, jnp.zeros((tq, 1), F32), vis)
        acc_scr[:, hs] = contrib
        carries.append(dc)

    def alive(cs):
        m = cs[0]
        for c in cs[1:]:
            m = jnp.maximum(m, c)
        return jnp.max(m)

    def cond(state):
        step, top, _ = state
        return jnp.logical_and(step < n_past, top > SB_DEAD)

    def body(state):
        step, _, cs = state
        j = n_past - 1 - step
        new = []
        for h, hs in enumerate(heads):
            kt, vt = past_kv(j, h)
            dc, contrib = _sb_tile(q_ref[:, hs], kt, vt, m2, cs[h], None)
            acc_scr[:, hs] += contrib
            new.append(cs[h] + dc)
        return step + 1, alive(new), tuple(new)

    lax.while_loop(cond, body, (jnp.int32(0), alive(carries), tuple(carries)))
    o_ref[...] = acc_scr[...].astype(o_ref.dtype)


def _tri_after(n):
    r = jnp.arange(n)
    m = (r[:, None] > r[None, :]).astype(BF16)
    return jnp.concatenate([m, m], axis=0)


def _sb_prompt_kernel(q_ref, k_ref, v_ref, m2_ref, o_ref, acc_scr, *, n_heads, tk):
    i = pl.program_id(1)

    def tile(j, h):
        rows = pl.ds(pl.multiple_of(j * tk, tk), tk)
        cols = slice(h * HEAD_DIM, (h + 1) * HEAD_DIM)
        return k_ref[rows, cols], v_ref[rows, cols]

    _sb_core(q_ref, lambda h: tile(i, h), tile, m2_ref, o_ref, acc_scr, i, n_heads, tk)


def _sb_prompt(qb, kb, vb, n_heads):
    b, s, wb = qb.shape
    tq = tk = 128
    qblk = lambda bb, i: (bb, i, 0)
    full = lambda bb, i: (bb, 0, 0)
    return pl.pallas_call(
        functools.partial(_sb_prompt_kernel, n_heads=n_heads, tk=tk),
        grid=(b, s // tq),
        in_specs=[
            pl.BlockSpec((None, tq, wb), qblk),
            pl.BlockSpec((None, s, wb), full),
            pl.BlockSpec((None, s, wb), full),
            pl.BlockSpec((2 * tk, tk), lambda bb, i: (0, 0)),
        ],
        out_specs=pl.BlockSpec((None, tq, wb), qblk),
        out_shape=jax.ShapeDtypeStruct((b, s, wb), BF16),
        scratch_shapes=[pltpu.VMEM((tq, wb), F32)],
        compiler_params=_params("parallel", "arbitrary"),
        name="sb_prompt",
    )(qb, kb, vb, _tri_after(tk))


def _pad_rows(new_ref, buf):
    t = new_ref.shape[0]
    buf[0:t, :] = new_ref[...].astype(buf.dtype)
    buf[t:, :] = jnp.zeros((buf.shape[0] - t, buf.shape[1]), buf.dtype)


def _ik3(x):
    hi, lo = _split_bf16(x)
    return jnp.concatenate([hi, hi, lo], axis=1)


def _dsa_sample_kernel(q_ref, iq_ref, ikw_ref, kn_ref, vn_ref, cik_ref, tri_ref, ck_hbm, cv_hbm,
                       o_ref, knew, vnew, iknew, s_scr, iq3_scr, kbuf, vbuf, sem,
                       *, past, t, ck, n_top, n_heads, n_idx_heads, idx_dim):
    b = pl.program_id(0)

    def head_copies(h):
        return (pltpu.make_async_copy(ck_hbm.at[b, :, h, :], kbuf.at[h], sem.at[0, h]),
                pltpu.make_async_copy(cv_hbm.at[b, :, h, :], vbuf.at[h], sem.at[1, h]))

    for h in range(n_heads):
        for cp in head_copies(h):
            cp.start()
    tn = knew.shape[0]
    _pad_rows(kn_ref, knew)
    _pad_rows(vn_ref, vnew)
    iknew[0:t, :] = ikw_ref[:, 0:idx_dim]
    iknew[t:, :] = jnp.zeros((ck - t, idx_dim), F32)
    limit = jnp.full((t, 1), past + t, jnp.int32)
    iw = ikw_ref[:, idx_dim:idx_dim + n_idx_heads]
    n_main = past // ck

    def ik3_main(c):
        return _ik3(cik_ref[pl.ds(pl.multiple_of(c * ck, ck), ck), :])

    _dsa_mask(iq_ref[...], iw, ik3_main, lambda: _ik3(iknew[...]), tri_ref, s_scr, iq3_scr,
              limit, n_main, ck, n_top, n_idx_heads, idx_dim)

    bias_past = jnp.concatenate([s_scr[c] for c in range(n_main)], axis=1)
    bias_new = s_scr[n_main][:, 0:tn]
    for h in range(n_heads):
        hs = slice(h * HEAD_DIM, (h + 1) * HEAD_DIM)
        q_h = q_ref[:, hs]
        for cp in head_copies(h):
            cp.wait()
        lg_p = lax.dot_general(q_h, kbuf[h].astype(BF16), NT_DIMS, preferred_element_type=F32) + bias_past
        lg_n = lax.dot_general(q_h, knew[:, hs], NT_DIMS, preferred_element_type=F32) + bias_new
        m = jnp.maximum(jnp.max(_fold_lanes(lg_p, jnp.maximum), axis=1, keepdims=True),
                        jnp.max(lg_n, axis=1, keepdims=True))
        p_p = jnp.exp(lg_p - m)
        p_n = jnp.exp(lg_n - m)
        l = jnp.sum(_fold_lanes(p_p, jnp.add), axis=1, keepdims=True) + jnp.sum(p_n, axis=1, keepdims=True)
        o = jnp.dot(p_p.astype(BF16), vbuf[h].astype(BF16), preferred_element_type=F32)
        o = o + jnp.dot(p_n.astype(BF16), vnew[:, hs], preferred_element_type=F32)
        o_ref[:, hs] = (o / l).astype(o_ref.dtype)


def _dsa_sample(qa, iq, ikw, ka, va, cache_k, cache_v, cache_ik, n_idx_heads):
    b, t, wa = qa.shape
    past, n_heads = cache_k.shape[1], cache_k.shape[2]
    idx_dim = cache_ik.shape[2]
    assert (past // CHUNK + 1) * CHUNK >= past + t, "new frames must sit in one open chunk"
    ck = DSA_CHUNK
    tn = LANES
    assert past % ck == 0 and t <= tn <= ck
    n_top = min(TOPK_MAX, (past + t) // 4)
    row = lambda bb: (bb, 0, 0)
    hbm = pl.BlockSpec(memory_space=pl.ANY)
    return pl.pallas_call(
        functools.partial(_dsa_sample_kernel, past=past, t=t, ck=ck, n_top=n_top, n_heads=n_heads,
                          n_idx_heads=n_idx_heads, idx_dim=idx_dim),
        grid=(b,),
        in_specs=[
            pl.BlockSpec((None, t, wa), row),
            pl.BlockSpec((None, t, iq.shape[2]), row),
            pl.BlockSpec((None, t, LANES), row),
            pl.BlockSpec((None, t, wa), row),
            pl.BlockSpec((None, t, wa), row),
            pl.BlockSpec((None, past, idx_dim), row),
            pl.BlockSpec((ck, ck), lambda bb: (0, 0)),
            hbm, hbm,
        ],
        out_specs=pl.BlockSpec((None, t, wa), row),
        out_shape=jax.ShapeDtypeStruct((b, t, wa), BF16),
        scratch_shapes=[
            pltpu.VMEM((tn, wa), BF16), pltpu.VMEM((tn, wa), BF16), pltpu.VMEM((ck, idx_dim), F32),
            pltpu.VMEM((past // ck + 1, t, ck), F32), pltpu.VMEM((n_idx_heads * t, 3 * idx_dim), BF16),
            pltpu.VMEM((n_heads, past, HEAD_DIM), F32), pltpu.VMEM((n_heads, past, HEAD_DIM), F32),
            pltpu.SemaphoreType.DMA((2, n_heads)),
        ],
        compiler_params=_params("arbitrary"),
        name="dsa_sample",
    )(qa, iq, ikw, ka, va, cache_ik, _tri_incl(ck), cache_k, cache_v)


def _sb_sample_kernel(q_ref, kn_ref, vn_ref, ck_ref, cv_ref, m2_ref, o_ref, knew, vnew, acc_scr,
                      *, past, t, tk, n_heads):
    for new_ref, buf in ((kn_ref, knew), (vn_ref, vnew)):
        buf[0:t, :] = new_ref[...]
        buf[t:, :] = jnp.zeros((tk - t, buf.shape[1]), BF16)

    def new_kv(h):
        cols = slice(h * HEAD_DIM, (h + 1) * HEAD_DIM)
        return knew[:, cols], vnew[:, cols]

    def cache_kv(j, h):
        rows = pl.ds(pl.multiple_of(j * tk, tk), tk)
        return ck_ref[rows, h, :].astype(BF16), cv_ref[rows, h, :].astype(BF16)

    _sb_core(q_ref, new_kv, cache_kv, m2_ref, o_ref, acc_scr, past // tk, n_heads, tk)


def _sb_sample(qb, kb, vb, cache_k, cache_v):
    b, t, wb = qb.shape
    past, n_heads = cache_k.shape[1], cache_k.shape[2]
    tk = 128
    assert past % tk == 0 and t <= tk
    row = lambda bb: (bb, 0, 0)
    cache = lambda bb: (bb, 0, 0, 0)
    return pl.pallas_call(
        functools.partial(_sb_sample_kernel, past=past, t=t, tk=tk, n_heads=n_heads),
        grid=(b,),
        in_specs=[
            pl.BlockSpec((None, t, wb), row),
            pl.BlockSpec((None, t, wb), row),
            pl.BlockSpec((None, t, wb), row),
            pl.BlockSpec((None, past, n_heads, HEAD_DIM), cache),
            pl.BlockSpec((None, past, n_heads, HEAD_DIM), cache),
            pl.BlockSpec((2 * tk, tk), lambda bb: (0, 0)),
        ],
        out_specs=pl.BlockSpec((None, t, wb), row),
        out_shape=jax.ShapeDtypeStruct((b, t, wb), BF16),
        scratch_shapes=[pltpu.VMEM((tk, wb), BF16), pltpu.VMEM((tk, wb), BF16), pltpu.VMEM((t, wb), F32)],
        compiler_params=_params("parallel"),
        name="sb_sample",
    )(qb, kb, vb, cache_k, cache_v, _tri_after(tk))


def _out_proj_kernel(y_ref, oa_ref, ob_ref, wa_ref, wb_ref, o_ref):
    acc = jnp.dot(oa_ref[...], wa_ref[...], preferred_element_type=F32)
    acc = acc + jnp.dot(ob_ref[...], wb_ref[...], preferred_element_type=F32)
    o_ref[...] = y_ref[...] + acc


def _out_proj(y, oa, ob, w_a, w_b):
    m, d = y.shape
    tm = min(512, m)
    row = lambda i: (i, 0)
    const = lambda i: (0, 0)
    return pl.pallas_call(
        _out_proj_kernel,
        grid=(m // tm,),
        in_specs=[
            pl.BlockSpec((tm, d), row),
            pl.BlockSpec((tm, oa.shape[1]), row),
            pl.BlockSpec((tm, ob.shape[1]), row),
            pl.BlockSpec(w_a.shape, const),
            pl.BlockSpec(w_b.shape, const),
        ],
        out_specs=pl.BlockSpec((tm, d), row),
        out_shape=jax.ShapeDtypeStruct((m, d), F32),
        compiler_params=_params("parallel"),
        name="out_proj",
    )(y, oa, ob, w_a, w_b)


def _gelu(x):
    return 0.5 * x * (1.0 + jnp.tanh(math.sqrt(2.0 / math.pi) * (x + 0.044715 * (x * x * x))))


def _c_v_kernel(x_ref, g_ref, w_ref, vg_ref, *outs):
    h = _rms(x_ref[...], g_ref[...]).astype(BF16)
    v = _gelu(jnp.dot(h, w_ref[...], preferred_element_type=F32))
    vn = _rms(v, vg_ref[...])
    outs[0][...] = vn.astype(BF16)
    if len(outs) > 1:
        outs[1][...] = vn


def _c_v(x, g, w_v, v_gain, want_f32):
    m, d = x.shape
    cw = w_v.shape[1]
    tm = min(512, m)
    row = lambda i: (i, 0)
    const = lambda i: (0, 0)
    out_specs = [pl.BlockSpec((tm, cw), row)]
    out_shape = [jax.ShapeDtypeStruct((m, cw), BF16)]
    if want_f32:
        out_specs.append(pl.BlockSpec((tm, cw), row))
        out_shape.append(jax.ShapeDtypeStruct((m, cw), F32))
    return pl.pallas_call(
        _c_v_kernel,
        grid=(m // tm,),
        in_specs=[
            pl.BlockSpec((tm, d), row),
            pl.BlockSpec((1, d), const),
            pl.BlockSpec((d, cw), const),
            pl.BlockSpec((1, cw), const),
        ],
        out_specs=out_specs,
        out_shape=out_shape,
        compiler_params=_params("parallel"),
        name="c_v",
    )(x, g.reshape(1, d), w_v, v_gain.reshape(1, cw))


def _c_mix_kernel(x_ref, g_ref, wu_ref, vn_ref, wm_ref, bias_ref, wo_ref, o_ref, h_scr, acc_scr, p_scr,
                  *, nj, gs):
    j = pl.program_id(1)

    @pl.when(j == 0)
    def _():
        h_scr[...] = _rms(x_ref[...], g_ref[...]).astype(BF16)
        acc_scr[...] = jnp.zeros_like(acc_scr)

    u = _gelu(jnp.dot(h_scr[...], wu_ref[...], preferred_element_type=F32))
    tm = u.shape[0]
    for gg in range(gs):
        cs = slice(gg * LANES, (gg + 1) * LANES)
        wm = wm_ref[j * gs + gg]
        for r in range(tm // C_CHUNK):
            rs = slice(r * C_CHUNK, (r + 1) * C_CHUNK)
            mix = jnp.dot(wm, vn_ref[rs, cs], preferred_element_type=F32) + bias_ref[:, cs]
            p_scr[rs, cs] = (u[rs, cs] * mix).astype(BF16)
    acc_scr[...] += jnp.dot(p_scr[...], wo_ref[...], preferred_element_type=F32)

    @pl.when(j == nj - 1)
    def _():
        o_ref[...] = x_ref[...] + acc_scr[...]


def _c_mix(x, g, w_u, vn, w_m, bias, w_o, gs=4):
    m, d = x.shape
    cw = w_u.shape[1]
    ng = cw // LANES
    gs = min(gs, ng)
    nj = ng // gs
    tm = min(512, m)
    row = lambda i, j: (i, 0)
    return pl.pallas_call(
        functools.partial(_c_mix_kernel, nj=nj, gs=gs),
        grid=(m // tm, nj),
        in_specs=[
            pl.BlockSpec((tm, d), row),
            pl.BlockSpec((1, d), lambda i, j: (0, 0)),
            pl.BlockSpec((d, gs * LANES), lambda i, j: (0, j)),
            pl.BlockSpec((tm, gs * LANES), lambda i, j: (i, j)),
            pl.BlockSpec(w_m.shape, lambda i, j: (0, 0, 0)),
            pl.BlockSpec((C_CHUNK, gs * LANES), lambda i, j: (0, j)),
            pl.BlockSpec((gs * LANES, d), lambda i, j: (j, 0)),
        ],
        out_specs=pl.BlockSpec((tm, d), row),
        out_shape=jax.ShapeDtypeStruct((m, d), F32),
        scratch_shapes=[pltpu.VMEM((tm, d), BF16), pltpu.VMEM((tm, d), F32),
                        pltpu.VMEM((tm, gs * LANES), BF16)],
        compiler_params=_params("parallel", "arbitrary"),
        name="c_mix",
    )(x, g.reshape(1, d), w_u, vn, w_m, bias, w_o)


def _rope_tables(pos, reps):
    pos = pos.astype(F32)[:, None]

    def tab(half, copies):
        inv = ROPE_THETA ** (-jnp.arange(half, dtype=F32) / half)
        ang = pos * inv[None, :]
        c, s = jnp.cos(ang), jnp.sin(ang)
        return jnp.tile(jnp.concatenate([c, c], axis=1), (reps, copies)), \
            jnp.tile(jnp.concatenate([-s, s], axis=1), (reps, copies))

    c128, s128 = tab(HEAD_DIM // 2, 1)
    c64, s64 = tab(32, 2)
    return c128, s128, c64, s64


def kernel(x_prompt, x_sample, cache_a_k, cache_a_v, cache_a_ik, cache_b_k, cache_b_v, norm_ff1, ff1_w1, ff1_w3, ff1_w2, norm_mix, norm_ff2, ff2_w1, ff2_w3, ff2_w2, ab_w_in, ab_w_out, c_w_in, c_v_norm, c_w_s, c_b_s, c_w_out, final_norm):
    bp, seq, d = x_prompt.shape
    bs, t, _ = x_sample.shape
    past, ha = cache_a_k.shape[2], cache_a_k.shape[3]
    hb = cache_b_k.shape[3]
    idx_dim = cache_a_ik.shape[3]
    wa, wb = ha * HEAD_DIM, hb * HEAD_DIM
    n_idx_heads = (ab_w_in.shape[2] - 3 * wa - 3 * wb - idx_dim) // (idx_dim + 1)
    wi = n_idx_heads * idx_dim
    assert idx_dim == 64 and wi % LANES == 0 and wi <= wa and n_idx_heads <= LANES - idx_dim
    depth = norm_ff1.shape[0]
    mp, ms = bp * seq, bs * t

    yp = x_prompt.reshape(mp, d)
    ys = x_sample.reshape(ms, d)
    bf = lambda w: w.astype(BF16)

    tabs_p = _rope_tables(jnp.arange(seq), 1)
    tabs_s = _rope_tables(past + jnp.arange(t), min(512, ms) // t)

    outs_p, outs_s, s_cv = [], [], []
    for layer in range(depth):
        j = layer // 2
        ys, w1, w3, w2 = _half_ffn(ys, norm_ff1[layer], ff1_w1[layer], ff1_w3[layer], ff1_w2[layer],
                                   emit_bf16=True)
        yp = _half_ffn(yp, norm_ff1[layer], w1, w3, w2)
        if layer % 2 == 0:
            w_in = ab_w_in[j]
            o = 0
            cols = []
            for width in (wa, wa, wa, wi, idx_dim, n_idx_heads, wb, wb, wb):
                cols.append(w_in[:, o:o + width])
                o += width
            w_qa, w_ka, w_va, w_iq, w_ik, w_iw, w_qb, w_kb, w_vb = cols
            pad = lambda w, width: jnp.pad(w, ((0, 0), (0, width - w.shape[1])))
            w_pa = bf(jnp.concatenate([w_qa, w_ka, w_va, pad(w_iq, wa)], axis=1))
            w_kw = bf(pad(jnp.concatenate([w_ik, w_iw], axis=1), LANES))
            w_pb = bf(jnp.concatenate([w_qb, w_kb, w_vb], axis=1))
            w_oa, w_ob = bf(ab_w_out[j][:wa]), bf(ab_w_out[j][wa:])

            tq = 128
            qt, kaf, kab, vaf, vt, iq3t, ikw, ik3, iwt = _proj_a_t(
                yp, norm_mix[layer], w_pa, w_kw, tabs_p, seq, ha, n_idx_heads, idx_dim, tq, DSA_CHUNK)
            qb, kbf, kbb, vbf, vbb = _proj_b(yp, norm_mix[layer], w_pb)
            r3 = lambda a: a.reshape(bp, seq, a.shape[-1])
            ik = ikw[:, :idx_dim]
            o_a = _dsa_prompt_t(qt, iq3t, iwt, ik3, kab, vt, bp, tq, DSA_CHUNK)
            o_b = _sb_prompt(r3(qb), r3(kbb), r3(vbb), hb)
            yp = _out_proj(yp, o_a.reshape(mp, wa), o_b.reshape(mp, wb), w_oa, w_ob)
            outs_p.append((kaf.reshape(bp, seq, ha, HEAD_DIM), vaf.reshape(bp, seq, ha, HEAD_DIM),
                           ik.reshape(bp, seq, idx_dim),
                           kbf.reshape(bp, seq, hb, HEAD_DIM), vbf.reshape(bp, seq, hb, HEAD_DIM)))

            qa, kaf, kab, vaf, vab, iq, ikw = _proj_a(
                ys, norm_mix[layer], w_pa, w_kw, tabs_s, min(512, ms), ha, n_idx_heads, idx_dim)
            qb, kbf, kbb, vbf, vbb = _proj_b(ys, norm_mix[layer], w_pb)
            r3 = lambda a: a.reshape(bs, t, a.shape[-1])
            o_a = _dsa_sample(r3(qa), r3(iq), r3(ikw), r3(kab), r3(vab),
                              cache_a_k[j], cache_a_v[j], cache_a_ik[j], n_idx_heads)
            o_b = _sb_sample(r3(qb), r3(kbb), r3(vbb), cache_b_k[j], cache_b_v[j])
            ys = _out_proj(ys, o_a.reshape(ms, wa), o_b.reshape(ms, wb), w_oa, w_ob)
            outs_s.append((kaf.reshape(bs, t, ha, HEAD_DIM), vaf.reshape(bs, t, ha, HEAD_DIM),
                           ikw[:, :idx_dim].reshape(bs, t, idx_dim),
                           kbf.reshape(bs, t, hb, HEAD_DIM), vbf.reshape(bs, t, hb, HEAD_DIM)))
        else:
            cw = c_w_in.shape[2] // 2
            ng = c_w_s.shape[1]
            w_u, w_v = bf(c_w_in[j][:, :cw]), bf(c_w_in[j][:, cw:])
            w_o = bf(c_w_out[j])
            i = jnp.arange(C_CHUNK)
            mask = (i[None, :] // CHUNK) <= (i[:, None] // CHUNK)
            w_m = jnp.where(mask[None], c_w_s[j], 0.0)
            bias_p = jnp.repeat(c_b_s[j].T, cw // ng, axis=1)
            per = C_CHUNK // t
            w_ms = jnp.einsum('ab,gij->gaibj', jnp.eye(per, dtype=F32), w_m[:, :t, :t]).reshape(ng, C_CHUNK, C_CHUNK)
            bias_s = jnp.tile(bias_p[:t], (per, 1))

            vn = _c_v(yp, norm_mix[layer], w_v, c_v_norm[j], False)[0]
            yp = _c_mix(yp, norm_mix[layer], w_u, vn, bf(w_m), bias_p, w_o)
            vn, vn_f32 = _c_v(ys, norm_mix[layer], w_v, c_v_norm[j], True)
            ys = _c_mix(ys, norm_mix[layer], w_u, vn, bf(w_ms), bias_s, w_o)
            s_cv.append(vn_f32.reshape(bs, t, cw))
        last = layer == depth - 1
        fg = final_norm if last else None
        ys, w1, w3, w2 = _half_ffn(ys, norm_ff2[layer], ff2_w1[layer], ff2_w3[layer], ff2_w2[layer], fg,
                                   emit_bf16=True)
        yp = _half_ffn(yp, norm_ff2[layer], w1, w3, w2, fg)

    stack = lambda outs, k: jnp.stack([o[k] for o in outs])
    return (yp.reshape(bp, seq, d), ys.reshape(bs, t, d),
            stack(outs_p, 0), stack(outs_p, 1), stack(outs_p, 2), stack(outs_p, 3), stack(outs_p, 4),
            stack(outs_s, 0), stack(outs_s, 1), stack(outs_s, 2), stack(outs_s, 3), stack(outs_s, 4),
            jnp.stack(s_cv))
```

```python
import functools
import math

import jax
import jax.numpy as jnp
from jax import lax
from jax.experimental import pallas as pl
from jax.experimental.pallas import tpu as pltpu

F32 = jnp.float32
BF16 = jnp.bfloat16

RMS_EPS = 1e-6
CHUNK = 64
TOPK_MAX = 256
ROPE_THETA = 10000.0
C_CHUNK = 128
LANES = 128
HEAD_DIM = 128
NEG = -1e30
INT_MIN = -2147483648
SB_DEAD = -105.0
DSA_CHUNK = 256
VMEM_LIMIT_BYTES = 56 * 1024 * 1024

NT_DIMS = (((1,), (1,)), ((), ()))


def _params(*sem):
    return pltpu.CompilerParams(dimension_semantics=sem, vmem_limit_bytes=VMEM_LIMIT_BYTES)


def _rms(x, g):
    return x * lax.rsqrt(jnp.mean(x * x, axis=-1, keepdims=True) + RMS_EPS) * g


def _split_bf16(x):
    hi = x.astype(BF16)
    lo = (x - hi.astype(F32)).astype(BF16)
    return hi, lo


def _ffn_kernel(x_ref, g_ref, w1_ref, w3_ref, w2_ref, *rest, nf, final_norm, emit_bf16):
    rest = list(rest)
    gf_ref = rest.pop(0) if final_norm else None
    o_ref = rest.pop(0)
    wb_refs = [rest.pop(0) for _ in range(3)] if emit_bf16 else None
    h_scr, acc_scr = rest
    f = pl.program_id(1)

    @pl.when(f == 0)
    def _():
        h_scr[...] = _rms(x_ref[...], g_ref[...]).astype(BF16)
        acc_scr[...] = jnp.zeros_like(acc_scr)

    w1, w3, w2 = w1_ref[...], w3_ref[...], w2_ref[...]
    if emit_bf16:
        w1, w3, w2 = w1.astype(BF16), w3.astype(BF16), w2.astype(BF16)
        for ref, w in zip(wb_refs, (w1, w3, w2)):
            ref[...] = w
    h = h_scr[...]
    a = jnp.dot(h, w1, preferred_element_type=F32)
    b = jnp.dot(h, w3, preferred_element_type=F32)
    p = (a * jax.nn.sigmoid(a) * b).astype(BF16)
    acc_scr[...] += jnp.dot(p, w2, preferred_element_type=F32)

    @pl.when(f == nf - 1)
    def _():
        y = x_ref[...] + 0.5 * acc_scr[...]
        if final_norm:
            y = _rms(y, gf_ref[...])
        o_ref[...] = y


def _ffn_tile(dff, target):
    return max(t for t in range(LANES, min(dff, target) + 1, LANES) if dff % t == 0)


def _half_ffn(x, g, w1, w3, w2, final_g=None, emit_bf16=False):
    m, d = x.shape
    dff = w1.shape[1]
    tm = min(512, m)
    tf = _ffn_tile(dff, 256 if emit_bf16 else 704)
    assert not emit_bf16 or m == tm
    nf = dff // tf
    w13_spec = pl.BlockSpec((d, tf), lambda i, f: (0, f))
    w2_spec = pl.BlockSpec((tf, d), lambda i, f: (f, 0))
    in_specs = [
        pl.BlockSpec((tm, d), lambda i, f: (i, 0)),
        pl.BlockSpec((1, d), lambda i, f: (0, 0)),
        w13_spec, w13_spec, w2_spec,
    ]
    args = [x, g.reshape(1, d), w1, w3, w2]
    if final_g is not None:
        in_specs.append(pl.BlockSpec((1, d), lambda i, f: (0, 0)))
        args.append(final_g.reshape(1, d))
    out_specs = [pl.BlockSpec((tm, d), lambda i, f: (i, 0))]
    out_shape = [jax.ShapeDtypeStruct((m, d), F32)]
    if emit_bf16:
        out_specs += [w13_spec, w13_spec, w2_spec]
        out_shape += [jax.ShapeDtypeStruct(w.shape, BF16) for w in (w1, w3, w2)]
    outs = pl.pallas_call(
        functools.partial(_ffn_kernel, nf=nf, final_norm=final_g is not None, emit_bf16=emit_bf16),
        grid=(m // tm, nf),
        in_specs=in_specs,
        out_specs=out_specs,
        out_shape=out_shape,
        scratch_shapes=[pltpu.VMEM((tm, d), BF16), pltpu.VMEM((tm, d), F32)],
        compiler_params=_params("parallel", "arbitrary"),
        name="half_ffn",
    )(*args)
    return outs if emit_bf16 else outs[0]


def _rope_heads(z, cos, sin):
    outs = []
    for h in range(z.shape[1] // LANES):
        zh = z[:, h * LANES:(h + 1) * LANES]
        outs.append(zh * cos + pltpu.roll(zh, LANES // 2, axis=1) * sin)
    return outs


def _rope_pairs(z, cos, sin):
    lane = lax.broadcasted_iota(jnp.int32, (z.shape[0], LANES), 1)
    low = (lane % 64) < 32
    outs = []
    for h in range(z.shape[1] // LANES):
        zh = z[:, h * LANES:(h + 1) * LANES]
        partner = jnp.where(low, pltpu.roll(zh, LANES - 32, axis=1), pltpu.roll(zh, 32, axis=1))
        outs.append(zh * cos + partner * sin)
    return outs


def _proj_a_kernel(x_ref, g_ref, w_ref, wkw_ref, c128_ref, s128_ref, c64_ref, s64_ref,
                   qa_ref, kaf_ref, kab_ref, vaf_ref, vab_ref, iq_ref, ikw_ref, h_scr,
                   *, n_idx_heads, idx_dim):
    j = pl.program_id(1)

    @pl.when(j == 0)
    def _():
        h_scr[...] = _rms(x_ref[...], g_ref[...]).astype(BF16)

    z = jnp.dot(h_scr[...], w_ref[...], preferred_element_type=F32)
    wa = qa_ref.shape[1]

    @pl.when(j == 0)
    def _():
        scale = HEAD_DIM ** -0.5
        for h, r in enumerate(_rope_heads(z[:, :wa], c128_ref[...], s128_ref[...])):
            qa_ref[:, h * LANES:(h + 1) * LANES] = (r * scale).astype(BF16)

    @pl.when(j == 1)
    def _():
        for h, r in enumerate(_rope_heads(z[:, :wa], c128_ref[...], s128_ref[...])):
            kaf_ref[:, h * LANES:(h + 1) * LANES] = r
            kab_ref[:, h * LANES:(h + 1) * LANES] = r.astype(BF16)

    @pl.when(j == 2)
    def _():
        kv = z[:, :wa]
        vaf_ref[...] = kv
        vab_ref[...] = kv.astype(BF16)

    @pl.when(j == 3)
    def _():
        wi = iq_ref.shape[1]
        for h, r in enumerate(_rope_pairs(z[:, :wi], c64_ref[...], s64_ref[...])):
            iq_ref[:, h * LANES:(h + 1) * LANES] = r * (idx_dim ** -0.5)
        zz = jnp.dot(h_scr[...], wkw_ref[...], preferred_element_type=F32)
        r = _rope_pairs(zz, c64_ref[...], s64_ref[...])[0]
        lane = lax.broadcasted_iota(jnp.int32, zz.shape, 1)
        ikw_ref[...] = jnp.where(lane < idx_dim, r, zz * (n_idx_heads ** -0.5))


def _proj_a(x, g, w, w_kw, tabs, n_pos_rows, ha, n_idx_heads, idx_dim):
    m, d = x.shape
    tm = min(512, m)
    wa = ha * HEAD_DIM
    wi = n_idx_heads * idx_dim
    tn = w.shape[1] // 4
    npb = n_pos_rows // tm
    row = lambda i, j: (i, 0)
    tab = lambda i, j: (i % npb, 0)
    tab_spec = pl.BlockSpec((tm, LANES), tab)
    return pl.pallas_call(
        functools.partial(_proj_a_kernel, n_idx_heads=n_idx_heads, idx_dim=idx_dim),
        grid=(m // tm, 4),
        in_specs=[
            pl.BlockSpec((tm, d), row),
            pl.BlockSpec((1, d), lambda i, j: (0, 0)),
            pl.BlockSpec((d, tn), lambda i, j: (0, j)),
            pl.BlockSpec((d, LANES), lambda i, j: (0, 0)),
            tab_spec, tab_spec, tab_spec, tab_spec,
        ],
        out_specs=[
            pl.BlockSpec((tm, wa), row), pl.BlockSpec((tm, wa), row), pl.BlockSpec((tm, wa), row),
            pl.BlockSpec((tm, wa), row), pl.BlockSpec((tm, wa), row),
            pl.BlockSpec((tm, wi), row), pl.BlockSpec((tm, LANES), row),
        ],
        out_shape=[
            jax.ShapeDtypeStruct((m, wa), BF16), jax.ShapeDtypeStruct((m, wa), F32),
            jax.ShapeDtypeStruct((m, wa), BF16), jax.ShapeDtypeStruct((m, wa), F32),
            jax.ShapeDtypeStruct((m, wa), BF16),
            jax.ShapeDtypeStruct((m, wi), F32), jax.ShapeDtypeStruct((m, LANES), F32),
        ],
        scratch_shapes=[pltpu.VMEM((tm, d), BF16)],
        compiler_params=_params("parallel", "arbitrary"),
        name="proj_a",
    )(x, g.reshape(1, d), w, w_kw, *tabs)


def _proj_a_t_kernel(x_ref, g_ref, w_ref, wkw_ref, c128_ref, s128_ref, c64_ref, s64_ref,
                     qt_ref, kaf_ref, kab_ref, vaf_ref, vt_ref, iq3t_ref, ikw_ref, ik3_ref, iwt_ref, h_scr,
                     *, n_idx_heads, idx_dim, tq, ck):
    j = pl.program_id(1)

    @pl.when(j == 0)
    def _():
        h_scr[...] = _rms(x_ref[...], g_ref[...]).astype(BF16)

    z = jnp.dot(h_scr[...], w_ref[...], preferred_element_type=F32)
    tm = z.shape[0]
    wa = kaf_ref.shape[1]

    @pl.when(j == 0)
    def _():
        scale = HEAD_DIM ** -0.5
        for h, r in enumerate(_rope_heads(z[:, :wa], c128_ref[...], s128_ref[...])):
            qt_ref[h * LANES:(h + 1) * LANES, :] = (r * scale).T.astype(BF16)

    @pl.when(j == 1)
    def _():
        for h, r in enumerate(_rope_heads(z[:, :wa], c128_ref[...], s128_ref[...])):
            kaf_ref[:, h * LANES:(h + 1) * LANES] = r
            kab_ref[:, h * LANES:(h + 1) * LANES] = r.astype(BF16)

    @pl.when(j == 2)
    def _():
        vaf_ref[...] = z[:, :wa]
        for h in range(wa // LANES):
            for c in range(tm // ck):
                vt_ref[c, h * LANES:(h + 1) * LANES, :] = \
                    z[c * ck:(c + 1) * ck, h * LANES:(h + 1) * LANES].T.astype(BF16)

    @pl.when(j == 3)
    def _():
        wi = n_idx_heads * idx_dim
        for g, r in enumerate(_rope_pairs(z[:, :wi], c64_ref[...], s64_ref[...])):
            r = r * (idx_dim ** -0.5)
            hi = r.astype(BF16).astype(F32)
            hi_t, lo_t = hi.T.astype(BF16), (r - hi).T.astype(BF16)
            for u in range(2):
                rows = slice(u * idx_dim, (u + 1) * idx_dim)
                for qb in range(tm // tq):
                    cols = slice(qb * tq, (qb + 1) * tq)
                    dst = slice((2 * g + u) * tq, (2 * g + u + 1) * tq)
                    iq3t_ref[qb, 0:idx_dim, dst] = hi_t[rows, cols]
                    iq3t_ref[qb, idx_dim:2 * idx_dim, dst] = lo_t[rows, cols]
                    iq3t_ref[qb, 2 * idx_dim:3 * idx_dim, dst] = hi_t[rows, cols]
        zz = jnp.dot(h_scr[...], wkw_ref[...], preferred_element_type=F32)
        r = _rope_pairs(zz, c64_ref[...], s64_ref[...])[0]
        lane = lax.broadcasted_iota(jnp.int32, zz.shape, 1)
        val = jnp.where(lane < idx_dim, r, zz * (n_idx_heads ** -0.5))
        ikw_ref[...] = val
        ik = val[:, 0:idx_dim]
        ik_hi = ik.astype(BF16).astype(F32)
        ik3_ref[...] = jnp.concatenate([ik_hi, ik_hi, ik - ik_hi], axis=1).astype(BF16)
        val_t = val.T
        for qb in range(tm // tq):
            iwt_ref[qb] = val_t[idx_dim:idx_dim + n_idx_heads, qb * tq:(qb + 1) * tq]


def _proj_a_t(x, g, w, w_kw, tabs, n_pos_rows, ha, n_idx_heads, idx_dim, tq, ck):
    m, d = x.shape
    tm = min(512, m)
    wa = ha * HEAD_DIM
    assert n_idx_heads * idx_dim <= wa and n_idx_heads % 2 == 0 and tm % ck == 0 and tm % tq == 0
    tn = w.shape[1] // 4
    npb = n_pos_rows // tm
    row = lambda i, j: (i, 0)
    lead = lambda i, j: (i, 0, 0)
    tab_spec = pl.BlockSpec((tm, LANES), lambda i, j: (i % npb, 0))
    return pl.pallas_call(
        functools.partial(_proj_a_t_kernel, n_idx_heads=n_idx_heads, idx_dim=idx_dim, tq=tq, ck=ck),
        grid=(m // tm, 4),
        in_specs=[
            pl.BlockSpec((tm, d), row),
            pl.BlockSpec((1, d), lambda i, j: (0, 0)),
            pl.BlockSpec((d, tn), lambda i, j: (0, j)),
            pl.BlockSpec((d, LANES), lambda i, j: (0, 0)),
            tab_spec, tab_spec, tab_spec, tab_spec,
        ],
        out_specs=[
            pl.BlockSpec((wa, tm), lambda i, j: (0, i)),
            pl.BlockSpec((tm, wa), row), pl.BlockSpec((tm, wa), row), pl.BlockSpec((tm, wa), row),
            pl.BlockSpec((tm // ck, wa, ck), lead),
            pl.BlockSpec((tm // tq, 3 * idx_dim, n_idx_heads * tq), lead),
            pl.BlockSpec((tm, LANES), row), pl.BlockSpec((tm, 3 * idx_dim), row),
            pl.BlockSpec((tm // tq, n_idx_heads, tq), lead),
        ],
        out_shape=[
            jax.ShapeDtypeStruct((wa, m), BF16),
            jax.ShapeDtypeStruct((m, wa), F32), jax.ShapeDtypeStruct((m, wa), BF16),
            jax.ShapeDtypeStruct((m, wa), F32),
            jax.ShapeDtypeStruct((m // ck, wa, ck), BF16),
            jax.ShapeDtypeStruct((m // tq, 3 * idx_dim, n_idx_heads * tq), BF16),
            jax.ShapeDtypeStruct((m, LANES), F32), jax.ShapeDtypeStruct((m, 3 * idx_dim), BF16),
            jax.ShapeDtypeStruct((m // tq, n_idx_heads, tq), F32),
        ],
        scratch_shapes=[pltpu.VMEM((tm, d), BF16)],
        compiler_params=_params("parallel", "arbitrary"),
        name="proj_a_t",
    )(x, g.reshape(1, d), w, w_kw, *tabs)


def _proj_b_kernel(x_ref, g_ref, w_ref, qb_ref, kbf_ref, kbb_ref, vbf_ref, vbb_ref, h_scr):
    j = pl.program_id(1)

    @pl.when(j == 0)
    def _():
        h_scr[...] = _rms(x_ref[...], g_ref[...]).astype(BF16)

    z = jnp.dot(h_scr[...], w_ref[...], preferred_element_type=F32)

    @pl.when(j == 0)
    def _():
        qb_ref[...] = (z * (HEAD_DIM ** -0.5)).astype(BF16)

    @pl.when(j == 1)
    def _():
        kbf_ref[...] = z
        kbb_ref[...] = z.astype(BF16)

    @pl.when(j == 2)
    def _():
        vbf_ref[...] = z
        vbb_ref[...] = z.astype(BF16)


def _proj_b(x, g, w):
    m, d = x.shape
    tm = min(512, m)
    wb = w.shape[1] // 3
    row = lambda i, j: (i, 0)
    return pl.pallas_call(
        _proj_b_kernel,
        grid=(m // tm, 3),
        in_specs=[
            pl.BlockSpec((tm, d), row),
            pl.BlockSpec((1, d), lambda i, j: (0, 0)),
            pl.BlockSpec((d, wb), lambda i, j: (0, j)),
        ],
        out_specs=[pl.BlockSpec((tm, wb), row)] * 5,
        out_shape=[
            jax.ShapeDtypeStruct((m, wb), BF16), jax.ShapeDtypeStruct((m, wb), F32),
            jax.ShapeDtypeStruct((m, wb), BF16), jax.ShapeDtypeStruct((m, wb), F32),
            jax.ShapeDtypeStruct((m, wb), BF16),
        ],
        scratch_shapes=[pltpu.VMEM((tm, d), BF16)],
        compiler_params=_params("parallel", "arbitrary"),
        name="proj_b",
    )(x, g.reshape(1, d), w)


def _key_to_f32(key):
    bits = jnp.where(key >= 0, key, key ^ jnp.int32(0x7FFFFFFF))
    return lax.bitcast_convert_type(bits, F32)


def _tree(parts, op):
    while len(parts) > 1:
        parts = [op(parts[k], parts[k + 1]) if k + 1 < len(parts) else parts[k] for k in range(0, len(parts), 2)]
    return parts[0]


def _fold_lanes(x, op):
    return _tree([x[:, t * LANES:(t + 1) * LANES] for t in range(x.shape[1] // LANES)], op)


def _dsa_mask(iq, iw, ik3_main, ik3_tail, tri_ref, s_scr, iq3_scr, limit, n_main, ck, n_top,
              n_idx_heads, idx_dim):
    tq = iq.shape[0]
    n_chunks = n_main + 1

    for h in range(n_idx_heads):
        x = iq[:, h * idx_dim:(h + 1) * idx_dim]
        hi = x.astype(BF16).astype(F32)
        iq3_scr[h * tq:(h + 1) * tq, :] = jnp.concatenate([hi, x - hi, hi], axis=1).astype(BF16)

    col0 = lax.broadcasted_iota(jnp.int32, (tq, ck), 1)

    def score_chunk(c, ikc):
        rel = lax.dot_general(iq3_scr[...], ikc, NT_DIMS, preferred_element_type=F32)
        acc = iw[:, 0:1] * jnp.maximum(rel[0:tq], 0.0)
        for h in range(1, n_idx_heads):
            acc = acc + iw[:, h:h + 1] * jnp.maximum(rel[h * tq:(h + 1) * tq], 0.0)
        s_scr[c] = jnp.where(col0 + c * ck < limit, acc, -jnp.inf)

    def score_main(c, _):
        score_chunk(c, ik3_main(c))
        return 0

    lax.fori_loop(0, n_main, score_main, 0)
    score_chunk(n_main, ik3_tail())

    def count(pred_fn):
        def body(c, acc):
            return acc + _fold_lanes(jnp.where(pred_fn(s_scr[c]), 1.0, 0.0), jnp.add)
        acc = lax.fori_loop(0, n_chunks, body, jnp.zeros((tq, LANES), F32))
        return jnp.sum(acc, axis=1, keepdims=True)

    kf = jnp.float32(n_top)
    cnt = count(lambda s: s >= 0.0)
    key = jnp.where(cnt >= kf, jnp.int32(0), jnp.int32(INT_MIN))

    def bit_step(b, key):
        cand = key | jnp.left_shift(jnp.int32(1), 30 - b)
        cand_f = _key_to_f32(cand)
        cnt = count(lambda s: s >= cand_f)
        return jnp.where(cnt >= kf, cand, key)

    key = lax.fori_loop(0, 31, bit_step, key)
    thr = _key_to_f32(key)
    need = kf - count(lambda s: s > thr)
    take_all = limit <= n_top

    def bias_chunk(c, carry):
        s = s_scr[c]
        eq = s == thr
        pre = jnp.dot(jnp.where(eq, 1.0, 0.0).astype(BF16), tri_ref[...], preferred_element_type=F32)
        tied = jnp.where((carry + pre) <= need, 0.0, NEG)
        bias = jnp.where(eq, tied, jnp.where(s > thr, 0.0, NEG))
        s_scr[c] = jnp.where(take_all, jnp.where(s > -jnp.inf, 0.0, NEG), bias)
        return carry + pre[:, ck - 1:ck]

    lax.fori_loop(0, n_chunks, bias_chunk, jnp.zeros((tq, 1), F32))


def _tri_incl(n):
    r = jnp.arange(n)
    return (r[:, None] <= r[None, :]).astype(BF16)


def _fold_rows(x, op):
    return _tree([x[r * 8:(r + 1) * 8] for r in range(x.shape[0] // 8)], op)


def _topk_bias_t(s_scr, n_pairs, limit, n_top, ltri_ref):
    ck, nl = s_scr.shape[1], s_scr.shape[2]

    def count(pred_fn):
        def body(pr, acc):
            a = _fold_rows(jnp.where(pred_fn(s_scr[2 * pr]), 1.0, 0.0), jnp.add)
            b = _fold_rows(jnp.where(pred_fn(s_scr[2 * pr + 1]), 1.0, 0.0), jnp.add)
            return acc + (a + b)
        acc = lax.fori_loop(0, n_pairs, body, jnp.zeros((8, nl), F32))
        return jnp.sum(acc, axis=0, keepdims=True)

    kf = jnp.float32(n_top)
    cnt = count(lambda s: s >= 0.0)
    state = (jnp.where(cnt >= kf, jnp.int32(0), jnp.int32(INT_MIN)), jnp.where(cnt >= kf, cnt, jnp.float32(2.0 ** 30)))

    def bit_step(b, state):
        key, n_ge = state
        cand = key | jnp.left_shift(jnp.int32(1), 30 - b)
        cand_f = _key_to_f32(cand)
        cnt = count(lambda s: s >= cand_f)
        return jnp.where(cnt >= kf, cand, key), jnp.where(cnt >= kf, cnt, n_ge)

    key, n_ge = lax.fori_loop(0, 31, bit_step, state)
    thr = _key_to_f32(key)
    take_all = limit <= n_top
    all_mask = lambda s: jnp.where(s > -jnp.inf, 0.0, NEG)

    def every_tie_taken():
        def chunk(c, _):
            s = s_scr[c]
            s_scr[c] = jnp.where(take_all, all_mask(s), jnp.where(s >= thr, 0.0, NEG))
            return 0
        lax.fori_loop(0, 2 * n_pairs, chunk, 0)

    def ties_by_index():
        need = kf - count(lambda s: s > thr)

        def chunk(c, carry):
            s = s_scr[c]
            eq = s == thr
            pre = jnp.dot(ltri_ref[...], jnp.where(eq, 1.0, 0.0).astype(BF16), preferred_element_type=F32)
            tied = jnp.where((carry + pre) <= need, 0.0, NEG)
            bias = jnp.where(eq, tied, jnp.where(s > thr, 0.0, NEG))
            s_scr[c] = jnp.where(take_all, all_mask(s), bias)
            return carry + pre[ck - 1:ck, :]
        lax.fori_loop(0, 2 * n_pairs, chunk, jnp.zeros((1, nl), F32))

    surplus = jnp.max(jnp.where(take_all, 0.0, n_ge - kf))
    lax.cond(surplus > 0.0, ties_by_index, every_tie_taken)


def _dsa_prompt_t_kernel(qt_ref, iq3t_ref, iwt_ref, ik3_ref, k_ref, vt_ref, ltri_ref, o_ref,
                         s_scr, acc_scr, lg_a, lg_b,
                         *, tq, ck, n_top, n_heads, n_idx_heads):
    i = pl.program_id(1)
    pos = i * tq + lax.broadcasted_iota(jnp.int32, (1, tq), 1)
    limit = (pos // CHUNK + 1) * CHUNK
    n_chunks = ((i + 1) * tq + ck - 1) // ck
    row0 = lax.broadcasted_iota(jnp.int32, (ck, tq), 0)
    iwt = iwt_ref[...]

    def rows(c):
        return pl.ds(pl.multiple_of(c * ck, ck), ck)

    def score_chunk(c, _):
        ikc = ik3_ref[rows(c), :]
        acc = None
        for g in range(n_idx_heads // 2):
            rel = jnp.dot(ikc, iq3t_ref[:, 2 * g * tq:2 * (g + 1) * tq], preferred_element_type=F32)
            for u in range(2):
                h = 2 * g + u
                term = iwt[h:h + 1, :] * jnp.maximum(rel[:, u * tq:(u + 1) * tq], 0.0)
                acc = term if acc is None else acc + term
        s_scr[c] = jnp.where(row0 + c * ck < limit, acc, -jnp.inf)
        return 0

    lax.fori_loop(0, n_chunks, score_chunk, 0)

    n_pairs = (n_chunks + 1) // 2
    last = 2 * n_pairs - 1

    @pl.when(n_chunks % 2 == 1)
    def _():
        s_scr[n_chunks] = jnp.full((ck, tq), -jnp.inf, F32)

    _topk_bias_t(s_scr, n_pairs, limit, n_top, ltri_ref)

    heads = [slice(h * HEAD_DIM, (h + 1) * HEAD_DIM) for h in range(n_heads)]
    acc_scr[...] = jnp.zeros_like(acc_scr)

    def logits(c, buf):
        for h, hs in enumerate(heads):
            buf[h] = jnp.dot(k_ref[rows(c), hs], qt_ref[hs, :], preferred_element_type=F32)

    def attend(c, buf, ms, ls):
        bias = s_scr[c]
        new_m, new_l = [], []
        for h, hs in enumerate(heads):
            lg = buf[h] + bias
            m_new = jnp.maximum(ms[h], jnp.max(_fold_rows(lg, jnp.maximum), axis=0, keepdims=True))
            alpha = jnp.exp(ms[h] - m_new)
            p = jnp.exp(lg - m_new)
            new_l.append(alpha * ls[h] + _fold_rows(p, jnp.add))
            acc_scr[hs, :] = alpha * acc_scr[hs, :] + jnp.dot(vt_ref[c, hs, :], p.astype(BF16),
                                                              preferred_element_type=F32)
            new_m.append(m_new)
        return tuple(new_m), tuple(new_l)

    def att_pair(pr, state):
        c0 = 2 * pr
        logits(c0 + 1, lg_b)
        state = attend(c0, lg_a, *state)
        logits(jnp.minimum(c0 + 2, last), lg_a)
        return attend(c0 + 1, lg_b, *state)

    logits(0, lg_a)
    init = (tuple(jnp.full((1, tq), NEG, F32) for _ in heads), tuple(jnp.zeros((8, tq), F32) for _ in heads))
    _, ls = lax.fori_loop(0, n_pairs, att_pair, init)
    for h, hs in enumerate(heads):
        o_t = acc_scr[hs, :] / jnp.sum(ls[h], axis=0, keepdims=True)
        o_ref[:, hs] = o_t.T.astype(o_ref.dtype)


def _dsa_prompt_t(qt, iq3t, iwt, ik3, ka, vt, b, tq, ck):
    wa, m = qt.shape
    s = m // b
    nq = s // tq
    n_heads = wa // HEAD_DIM
    n_idx_heads = iwt.shape[1]
    idx_dim = ik3.shape[1] // 3
    n_top = min(TOPK_MAX, s // 4)
    assert s % (2 * ck) == 0
    ik3 = ik3.reshape(b, s, 3 * idx_dim)
    ka = ka.reshape(b, s, wa)
    vt = vt.reshape(b, s // ck, wa, ck)
    r = jnp.arange(ck)
    ltri = (r[None, :] <= r[:, None]).astype(BF16)

    return pl.pallas_call(
        functools.partial(_dsa_prompt_t_kernel, tq=tq, ck=ck, n_top=n_top, n_heads=n_heads,
                          n_idx_heads=n_idx_heads),
        grid=(b, nq),
        in_specs=[
            pl.BlockSpec((wa, tq), lambda bb, i: (0, bb * nq + i)),
            pl.BlockSpec((None, 3 * idx_dim, n_idx_heads * tq), lambda bb, i: (bb * nq + i, 0, 0)),
            pl.BlockSpec((None, n_idx_heads, tq), lambda bb, i: (bb * nq + i, 0, 0)),
            pl.BlockSpec((None, s, 3 * idx_dim), lambda bb, i: (bb, 0, 0)),
            pl.BlockSpec((None, s, wa), lambda bb, i: (bb, 0, 0)),
            pl.BlockSpec((None, s // ck, wa, ck), lambda bb, i: (bb, 0, 0, 0)),
            pl.BlockSpec((ck, ck), lambda bb, i: (0, 0)),
        ],
        out_specs=pl.BlockSpec((None, tq, wa), lambda bb, i: (bb, i, 0)),
        out_shape=jax.ShapeDtypeStruct((b, s, wa), BF16),
        scratch_shapes=[pltpu.VMEM((s // ck, ck, tq), F32), pltpu.VMEM((wa, tq), F32),
                        pltpu.VMEM((n_heads, ck, tq), F32), pltpu.VMEM((n_heads, ck, tq), F32)],
        compiler_params=_params("parallel", "arbitrary"),
        name="dsa_prompt",
    )(qt, iq3t, iwt, ik3, ka, vt, ltri)


def _sb_tile(q_h, kt, vt, m2, carry, vis):
    z = lax.dot_general(q_h, kt, NT_DIMS, preferred_element_type=F32)
    sp = jnp.maximum(z, 0.0) + jnp.log1p(jnp.exp(-jnp.abs(z)))
    lk = -sp if vis is None else jnp.where(vis, -sp, 0.0)
    hi, lo = _split_bf16(lk)
    after = jnp.dot(jnp.concatenate([hi, lo], axis=1), m2, preferred_element_type=F32)
    a = jnp.exp(z - sp + after + carry)
    if vis is not None:
        a = jnp.where(vis, a, 0.0)
    return jnp.sum(lk, axis=1, keepdims=True), jnp.dot(a.astype(BF16), vt, preferred_element_type=F32)


def _sb_core(q_ref, diag_kv, past_kv, m2_ref, o_ref, acc_scr, n_past, n_heads, tk):
    tq = q_ref.shape[0]
    vis = lax.broadcasted_iota(jnp.int32, (tq, tk), 1) < lax.broadcasted_iota(jnp.int32, (tq, tk), 0)
    m2 = m2_ref[...]
    heads = [slice(h * HEAD_DIM, (h + 1) * HEAD_DIM) for h in range(n_heads)]

    carries = []
    for h, hs in enumerate(heads):
        kt, vt = diag_kv(h)
        dc, contrib = _sb_tile(q_ref[:, hs], kt, vt, m2, jnp.zeros((tq, 1), F32), vis)
        acc_scr[:, hs] = contrib
        carries.append(dc)

    def alive(cs):
        m = cs[0]
        for c in cs[1:]:
            m = jnp.maximum(m, c)
        return jnp.max(m)

    def cond(state):
        step, top, _ = state
        return jnp.logical_and(step < n_past, top > SB_DEAD)

    def body(state):
        step, _, cs = state
        j = n_past - 1 - step
        new = []
        for h, hs in enumerate(heads):
            kt, vt = past_kv(j, h)
            dc, contrib = _sb_tile(q_ref[:, hs], kt, vt, m2, cs[h], None)
            acc_scr[:, hs] += contrib
            new.append(cs[h] + dc)
        return step + 1, alive(new), tuple(new)

    lax.while_loop(cond, body, (jnp.int32(0), alive(carries), tuple(carries)))
    o_ref[...] = acc_scr[...].astype(o_ref.dtype)


def _tri_after(n):
    r = jnp.arange(n)
    m = (r[:, None] > r[None, :]).astype(BF16)
    return jnp.concatenate([m, m], axis=0)


def _sb_prompt_kernel(q_ref, k_ref, v_ref, m2_ref, o_ref, acc_scr, *, n_heads, tk):
    i = pl.program_id(1)

    def tile(j, h):
        rows = pl.ds(pl.multiple_of(j * tk, tk), tk)
        cols = slice(h * HEAD_DIM, (h + 1) * HEAD_DIM)
        return k_ref[rows, cols], v_ref[rows, cols]

    _sb_core(q_ref, lambda h: tile(i, h), tile, m2_ref, o_ref, acc_scr, i, n_heads, tk)


def _sb_prompt(qb, kb, vb, n_heads):
    b, s, wb = qb.shape
    tq = tk = 128
    qblk = lambda bb, i: (bb, i, 0)
    full = lambda bb, i: (bb, 0, 0)
    return pl.pallas_call(
        functools.partial(_sb_prompt_kernel, n_heads=n_heads, tk=tk),
        grid=(b, s // tq),
        in_specs=[
            pl.BlockSpec((None, tq, wb), qblk),
            pl.BlockSpec((None, s, wb), full),
            pl.BlockSpec((None, s, wb), full),
            pl.BlockSpec((2 * tk, tk), lambda bb, i: (0, 0)),
        ],
        out_specs=pl.BlockSpec((None, tq, wb), qblk),
        out_shape=jax.ShapeDtypeStruct((b, s, wb), BF16),
        scratch_shapes=[pltpu.VMEM((tq, wb), F32)],
        compiler_params=_params("parallel", "arbitrary"),
        name="sb_prompt",
    )(qb, kb, vb, _tri_after(tk))


def _pad_rows(new_ref, buf):
    t = new_ref.shape[0]
    buf[0:t, :] = new_ref[...].astype(buf.dtype)
    buf[t:, :] = jnp.zeros((buf.shape[0] - t, buf.shape[1]), buf.dtype)


def _ik3(x):
    hi, lo = _split_bf16(x)
    return jnp.concatenate([hi, hi, lo], axis=1)


def _dsa_sample_kernel(q_ref, iq_ref, ikw_ref, kn_ref, vn_ref, cik_ref, tri_ref, ck_hbm, cv_hbm,
                       o_ref, knew, vnew, iknew, s_scr, iq3_scr, kbuf, vbuf, sem,
                       *, past, t, ck, n_top, n_heads, n_idx_heads, idx_dim):
    b = pl.program_id(0)

    def head_copies(h):
        return (pltpu.make_async_copy(ck_hbm.at[b, :, h, :], kbuf.at[h], sem.at[0, h]),
                pltpu.make_async_copy(cv_hbm.at[b, :, h, :], vbuf.at[h], sem.at[1, h]))

    for h in range(n_heads):
        for cp in head_copies(h):
            cp.start()
    tn = knew.shape[0]
    _pad_rows(kn_ref, knew)
    _pad_rows(vn_ref, vnew)
    iknew[0:t, :] = ikw_ref[:, 0:idx_dim]
    iknew[t:, :] = jnp.zeros((ck - t, idx_dim), F32)
    limit = jnp.full((t, 1), past + t, jnp.int32)
    iw = ikw_ref[:, idx_dim:idx_dim + n_idx_heads]
    n_main = past // ck

    def ik3_main(c):
        return _ik3(cik_ref[pl.ds(pl.multiple_of(c * ck, ck), ck), :])

    _dsa_mask(iq_ref[...], iw, ik3_main, lambda: _ik3(iknew[...]), tri_ref, s_scr, iq3_scr,
              limit, n_main, ck, n_top, n_idx_heads, idx_dim)

    bias_past = jnp.concatenate([s_scr[c] for c in range(n_main)], axis=1)
    bias_new = s_scr[n_main][:, 0:tn]
    for h in range(n_heads):
        hs = slice(h * HEAD_DIM, (h + 1) * HEAD_DIM)
        q_h = q_ref[:, hs]
        for cp in head_copies(h):
            cp.wait()
        lg_p = lax.dot_general(q_h, kbuf[h].astype(BF16), NT_DIMS, preferred_element_type=F32) + bias_past
        lg_n = lax.dot_general(q_h, knew[:, hs], NT_DIMS, preferred_element_type=F32) + bias_new
        m = jnp.maximum(jnp.max(_fold_lanes(lg_p, jnp.maximum), axis=1, keepdims=True),
                        jnp.max(lg_n, axis=1, keepdims=True))
        p_p = jnp.exp(lg_p - m)
        p_n = jnp.exp(lg_n - m)
        l = jnp.sum(_fold_lanes(p_p, jnp.add), axis=1, keepdims=True) + jnp.sum(p_n, axis=1, keepdims=True)
        o = jnp.dot(p_p.astype(BF16), vbuf[h].astype(BF16), preferred_element_type=F32)
        o = o + jnp.dot(p_n.astype(BF16), vnew[:, hs], preferred_element_type=F32)
        o_ref[:, hs] = (o / l).astype(o_ref.dtype)


def _dsa_sample(qa, iq, ikw, ka, va, cache_k, cache_v, cache_ik, n_idx_heads):
    b, t, wa = qa.shape
    past, n_heads = cache_k.shape[1], cache_k.shape[2]
    idx_dim = cache_ik.shape[2]
    assert (past // CHUNK + 1) * CHUNK >= past + t, "new frames must sit in one open chunk"
    ck = DSA_CHUNK
    tn = LANES
    assert past % ck == 0 and t <= tn <= ck
    n_top = min(TOPK_MAX, (past + t) // 4)
    row = lambda bb: (bb, 0, 0)
    hbm = pl.BlockSpec(memory_space=pl.ANY)
    return pl.pallas_call(
        functools.partial(_dsa_sample_kernel, past=past, t=t, ck=ck, n_top=n_top, n_heads=n_heads,
                          n_idx_heads=n_idx_heads, idx_dim=idx_dim),
        grid=(b,),
        in_specs=[
            pl.BlockSpec((None, t, wa), row),
            pl.BlockSpec((None, t, iq.shape[2]), row),
            pl.BlockSpec((None, t, LANES), row),
            pl.BlockSpec((None, t, wa), row),
            pl.BlockSpec((None, t, wa), row),
            pl.BlockSpec((None, past, idx_dim), row),
            pl.BlockSpec((ck, ck), lambda bb: (0, 0)),
            hbm, hbm,
        ],
        out_specs=pl.BlockSpec((None, t, wa), row),
        out_shape=jax.ShapeDtypeStruct((b, t, wa), BF16),
        scratch_shapes=[
            pltpu.VMEM((tn, wa), BF16), pltpu.VMEM((tn, wa), BF16), pltpu.VMEM((ck, idx_dim), F32),
            pltpu.VMEM((past // ck + 1, t, ck), F32), pltpu.VMEM((n_idx_heads * t, 3 * idx_dim), BF16),
            pltpu.VMEM((n_heads, past, HEAD_DIM), F32), pltpu.VMEM((n_heads, past, HEAD_DIM), F32),
            pltpu.SemaphoreType.DMA((2, n_heads)),
        ],
        compiler_params=_params("arbitrary"),
        name="dsa_sample",
    )(qa, iq, ikw, ka, va, cache_ik, _tri_incl(ck), cache_k, cache_v)


def _sb_sample_kernel(q_ref, kn_ref, vn_ref, ck_ref, cv_ref, m2_ref, o_ref, knew, vnew, acc_scr,
                      *, past, t, tk, n_heads):
    for new_ref, buf in ((kn_ref, knew), (vn_ref, vnew)):
        buf[0:t, :] = new_ref[...]
        buf[t:, :] = jnp.zeros((tk - t, buf.shape[1]), BF16)

    def new_kv(h):
        cols = slice(h * HEAD_DIM, (h + 1) * HEAD_DIM)
        return knew[:, cols], vnew[:, cols]

    def cache_kv(j, h):
        rows = pl.ds(pl.multiple_of(j * tk, tk), tk)
        return ck_ref[rows, h, :].astype(BF16), cv_ref[rows, h, :].astype(BF16)

    _sb_core(q_ref, new_kv, cache_kv, m2_ref, o_ref, acc_scr, past // tk, n_heads, tk)


def _sb_sample(qb, kb, vb, cache_k, cache_v):
    b, t, wb = qb.shape
    past, n_heads = cache_k.shape[1], cache_k.shape[2]
    tk = 128
    assert past % tk == 0 and t <= tk
    row = lambda bb: (bb, 0, 0)
    cache = lambda bb: (bb, 0, 0, 0)
    return pl.pallas_call(
        functools.partial(_sb_sample_kernel, past=past, t=t, tk=tk, n_heads=n_heads),
        grid=(b,),
        in_specs=[
            pl.BlockSpec((None, t, wb), row),
            pl.BlockSpec((None, t, wb), row),
            pl.BlockSpec((None, t, wb), row),
            pl.BlockSpec((None, past, n_heads, HEAD_DIM), cache),
            pl.BlockSpec((None, past, n_heads, HEAD_DIM), cache),
            pl.BlockSpec((2 * tk, tk), lambda bb: (0, 0)),
        ],
        out_specs=pl.BlockSpec((None, t, wb), row),
        out_shape=jax.ShapeDtypeStruct((b, t, wb), BF16),
        scratch_shapes=[pltpu.VMEM((tk, wb), BF16), pltpu.VMEM((tk, wb), BF16), pltpu.VMEM((t, wb), F32)],
        compiler_params=_params("parallel"),
        name="sb_sample",
    )(qb, kb, vb, cache_k, cache_v, _tri_after(tk))


def _out_proj_kernel(y_ref, oa_ref, ob_ref, wa_ref, wb_ref, o_ref):
    acc = jnp.dot(oa_ref[...], wa_ref[...], preferred_element_type=F32)
    acc = acc + jnp.dot(ob_ref[...], wb_ref[...], preferred_element_type=F32)
    o_ref[...] = y_ref[...] + acc


def _out_proj(y, oa, ob, w_a, w_b):
    m, d = y.shape
    tm = min(512, m)
    row = lambda i: (i, 0)
    const = lambda i: (0, 0)
    return pl.pallas_call(
        _out_proj_kernel,
        grid=(m // tm,),
        in_specs=[
            pl.BlockSpec((tm, d), row),
            pl.BlockSpec((tm, oa.shape[1]), row),
            pl.BlockSpec((tm, ob.shape[1]), row),
            pl.BlockSpec(w_a.shape, const),
            pl.BlockSpec(w_b.shape, const),
        ],
        out_specs=pl.BlockSpec((tm, d), row),
        out_shape=jax.ShapeDtypeStruct((m, d), F32),
        compiler_params=_params("parallel"),
        name="out_proj",
    )(y, oa, ob, w_a, w_b)


def _gelu(x):
    return 0.5 * x * (1.0 + jnp.tanh(math.sqrt(2.0 / math.pi) * (x + 0.044715 * (x * x * x))))


def _c_v_kernel(x_ref, g_ref, w_ref, vg_ref, *outs):
    h = _rms(x_ref[...], g_ref[...]).astype(BF16)
    v = _gelu(jnp.dot(h, w_ref[...], preferred_element_type=F32))
    vn = _rms(v, vg_ref[...])
    outs[0][...] = vn.astype(BF16)
    if len(outs) > 1:
        outs[1][...] = vn


def _c_v(x, g, w_v, v_gain, want_f32):
    m, d = x.shape
    cw = w_v.shape[1]
    tm = min(512, m)
    row = lambda i: (i, 0)
    const = lambda i: (0, 0)
    out_specs = [pl.BlockSpec((tm, cw), row)]
    out_shape = [jax.ShapeDtypeStruct((m, cw), BF16)]
    if want_f32:
        out_specs.append(pl.BlockSpec((tm, cw), row))
        out_shape.append(jax.ShapeDtypeStruct((m, cw), F32))
    return pl.pallas_call(
        _c_v_kernel,
        grid=(m // tm,),
        in_specs=[
            pl.BlockSpec((tm, d), row),
            pl.BlockSpec((1, d), const),
            pl.BlockSpec((d, cw), const),
            pl.BlockSpec((1, cw), const),
        ],
        out_specs=out_specs,
        out_shape=out_shape,
        compiler_params=_params("parallel"),
        name="c_v",
    )(x, g.reshape(1, d), w_v, v_gain.reshape(1, cw))


def _c_mix_kernel(x_ref, g_ref, wu_ref, vn_ref, wm_ref, bias_ref, wo_ref, o_ref, h_scr, acc_scr, p_scr,
                  *, nj, gs):
    j = pl.program_id(1)

    @pl.when(j == 0)
    def _():
        h_scr[...] = _rms(x_ref[...], g_ref[...]).astype(BF16)
        acc_scr[...] = jnp.zeros_like(acc_scr)

    u = _gelu(jnp.dot(h_scr[...], wu_ref[...], preferred_element_type=F32))
    tm = u.shape[0]
    for gg in range(gs):
        cs = slice(gg * LANES, (gg + 1) * LANES)
        wm = wm_ref[j * gs + gg]
        for r in range(tm // C_CHUNK):
            rs = slice(r * C_CHUNK, (r + 1) * C_CHUNK)
            mix = jnp.dot(wm, vn_ref[rs, cs], preferred_element_type=F32) + bias_ref[:, cs]
            p_scr[rs, cs] = (u[rs, cs] * mix).astype(BF16)
    acc_scr[...] += jnp.dot(p_scr[...], wo_ref[...], preferred_element_type=F32)

    @pl.when(j == nj - 1)
    def _():
        o_ref[...] = x_ref[...] + acc_scr[...]


def _c_mix(x, g, w_u, vn, w_m, bias, w_o, gs=4):
    m, d = x.shape
    cw = w_u.shape[1]
    ng = cw // LANES
    gs = min(gs, ng)
    nj = ng // gs
    tm = min(512, m)
    row = lambda i, j: (i, 0)
    return pl.pallas_call(
        functools.partial(_c_mix_kernel, nj=nj, gs=gs),
        grid=(m // tm, nj),
        in_specs=[
            pl.BlockSpec((tm, d), row),
            pl.BlockSpec((1, d), lambda i, j: (0, 0)),
            pl.BlockSpec((d, gs * LANES), lambda i, j: (0, j)),
            pl.BlockSpec((tm, gs * LANES), lambda i, j: (i, j)),
            pl.BlockSpec(w_m.shape, lambda i, j: (0, 0, 0)),
            pl.BlockSpec((C_CHUNK, gs * LANES), lambda i, j: (0, j)),
            pl.BlockSpec((gs * LANES, d), lambda i, j: (j, 0)),
        ],
        out_specs=pl.BlockSpec((tm, d), row),
        out_shape=jax.ShapeDtypeStruct((m, d), F32),
        scratch_shapes=[pltpu.VMEM((tm, d), BF16), pltpu.VMEM((tm, d), F32),
                        pltpu.VMEM((tm, gs * LANES), BF16)],
        compiler_params=_params("parallel", "arbitrary"),
        name="c_mix",
    )(x, g.reshape(1, d), w_u, vn, w_m, bias, w_o)


def _rope_tables(pos, reps):
    pos = pos.astype(F32)[:, None]

    def tab(half, copies):
        inv = ROPE_THETA ** (-jnp.arange(half, dtype=F32) / half)
        ang = pos * inv[None, :]
        c, s = jnp.cos(ang), jnp.sin(ang)
        return jnp.tile(jnp.concatenate([c, c], axis=1), (reps, copies)), \
            jnp.tile(jnp.concatenate([-s, s], axis=1), (reps, copies))

    c128, s128 = tab(HEAD_DIM // 2, 1)
    c64, s64 = tab(32, 2)
    return c128, s128, c64, s64


def kernel(x_prompt, x_sample, cache_a_k, cache_a_v, cache_a_ik, cache_b_k, cache_b_v, norm_ff1, ff1_w1, ff1_w3, ff1_w2, norm_mix, norm_ff2, ff2_w1, ff2_w3, ff2_w2, ab_w_in, ab_w_out, c_w_in, c_v_norm, c_w_s, c_b_s, c_w_out, final_norm):
    bp, seq, d = x_prompt.shape
    bs, t, _ = x_sample.shape
    past, ha = cache_a_k.shape[2], cache_a_k.shape[3]
    hb = cache_b_k.shape[3]
    idx_dim = cache_a_ik.shape[3]
    wa, wb = ha * HEAD_DIM, hb * HEAD_DIM
    n_idx_heads = (ab_w_in.shape[2] - 3 * wa - 3 * wb - idx_dim) // (idx_dim + 1)
    wi = n_idx_heads * idx_dim
    assert idx_dim == 64 and wi % LANES == 0 and wi <= wa and n_idx_heads <= LANES - idx_dim
    depth = norm_ff1.shape[0]
    mp, ms = bp * seq, bs * t

    yp = x_prompt.reshape(mp, d)
    ys = x_sample.reshape(ms, d)
    bf = lambda w: w.astype(BF16)

    tabs_p = _rope_tables(jnp.arange(seq), 1)
    tabs_s = _rope_tables(past + jnp.arange(t), min(512, ms) // t)

    outs_p, outs_s, s_cv = [], [], []
    for layer in range(depth):
        j = layer // 2
        ys, w1, w3, w2 = _half_ffn(ys, norm_ff1[layer], ff1_w1[layer], ff1_w3[layer], ff1_w2[layer],
                                   emit_bf16=True)
        yp = _half_ffn(yp, norm_ff1[layer], w1, w3, w2)
        if layer % 2 == 0:
            w_in = ab_w_in[j]
            o = 0
            cols = []
            for width in (wa, wa, wa, wi, idx_dim, n_idx_heads, wb, wb, wb):
                cols.append(w_in[:, o:o + width])
                o += width
            w_qa, w_ka, w_va, w_iq, w_ik, w_iw, w_qb, w_kb, w_vb = cols
            pad = lambda w, width: jnp.pad(w, ((0, 0), (0, width - w.shape[1])))
            w_pa = bf(jnp.concatenate([w_qa, w_ka, w_va, pad(w_iq, wa)], axis=1))
            w_kw = bf(pad(jnp.concatenate([w_ik, w_iw], axis=1), LANES))
            w_pb = bf(jnp.concatenate([w_qb, w_kb, w_vb], axis=1))
            w_oa, w_ob = bf(ab_w_out[j][:wa]), bf(ab_w_out[j][wa:])

            tq = 128
            qt, kaf, kab, vaf, vt, iq3t, ikw, ik3, iwt = _proj_a_t(
                yp, norm_mix[layer], w_pa, w_kw, tabs_p, seq, ha, n_idx_heads, idx_dim, tq, DSA_CHUNK)
            qb, kbf, kbb, vbf, vbb = _proj_b(yp, norm_mix[layer], w_pb)
            r3 = lambda a: a.reshape(bp, seq, a.shape[-1])
            ik = ikw[:, :idx_dim]
            o_a = _dsa_prompt_t(qt, iq3t, iwt, ik3, kab, vt, bp, tq, DSA_CHUNK)
            o_b = _sb_prompt(r3(qb), r3(kbb), r3(vbb), hb)
            yp = _out_proj(yp, o_a.reshape(mp, wa), o_b.reshape(mp, wb), w_oa, w_ob)
            outs_p.append((kaf.reshape(bp, seq, ha, HEAD_DIM), vaf.reshape(bp, seq, ha, HEAD_DIM),
                           ik.reshape(bp, seq, idx_dim),
                           kbf.reshape(bp, seq, hb, HEAD_DIM), vbf.reshape(bp, seq, hb, HEAD_DIM)))

            qa, kaf, kab, vaf, vab, iq, ikw = _proj_a(
                ys, norm_mix[layer], w_pa, w_kw, tabs_s, min(512, ms), ha, n_idx_heads, idx_dim)
            qb, kbf, kbb, vbf, vbb = _proj_b(ys, norm_mix[layer], w_pb)
            r3 = lambda a: a.reshape(bs, t, a.shape[-1])
            o_a = _dsa_sample(r3(qa), r3(iq), r3(ikw), r3(kab), r3(vab),
                              cache_a_k[j], cache_a_v[j], cache_a_ik[j], n_idx_heads)
            o_b = _sb_sample(r3(qb), r3(kbb), r3(vbb), cache_b_k[j], cache_b_v[j])
            ys = _out_proj(ys, o_a.reshape(ms, wa), o_b.reshape(ms, wb), w_oa, w_ob)
            outs_s.append((kaf.reshape(bs, t, ha, HEAD_DIM), vaf.reshape(bs, t, ha, HEAD_DIM),
                           ikw[:, :idx_dim].reshape(bs, t, idx_dim),
                           kbf.reshape(bs, t, hb, HEAD_DIM), vbf.reshape(bs, t, hb, HEAD_DIM)))
        else:
            cw = c_w_in.shape[2] // 2
            ng = c_w_s.shape[1]
            w_u, w_v = bf(c_w_in[j][:, :cw]), bf(c_w_in[j][:, cw:])
            w_o = bf(c_w_out[j])
            i = jnp.arange(C_CHUNK)
            mask = (i[None, :] // CHUNK) <= (i[:, None] // CHUNK)
            w_m = jnp.where(mask[None], c_w_s[j], 0.0)
            bias_p = jnp.repeat(c_b_s[j].T, cw // ng, axis=1)
            per = C_CHUNK // t
            w_ms = jnp.einsum('ab,gij->gaibj', jnp.eye(per, dtype=F32), w_m[:, :t, :t]).reshape(ng, C_CHUNK, C_CHUNK)
            bias_s = jnp.tile(bias_p[:t], (per, 1))

            vn = _c_v(yp, norm_mix[layer], w_v, c_v_norm[j], False)[0]
            yp = _c_mix(yp, norm_mix[layer], w_u, vn, bf(w_m), bias_p, w_o)
            vn, vn_f32 = _c_v(ys, norm_mix[layer], w_v, c_v_norm[j], True)
            ys = _c_mix(ys, norm_mix[layer], w_u, vn, bf(w_ms), bias_s, w_o)
            s_cv.append(vn_f32.reshape(bs, t, cw))
        last = layer == depth - 1
        fg = final_norm if last else None
        ys, w1, w3, w2 = _half_ffn(ys, norm_ff2[layer], ff2_w1[layer], ff2_w3[layer], ff2_w2[layer], fg,
                                   emit_bf16=True)
        yp = _half_ffn(yp, norm_ff2[layer], w1, w3, w2, fg)

    stack = lambda outs, k: jnp.stack([o[k] for o in outs])
    return (yp.reshape(bp, seq, d), ys.reshape(bs, t, d),
            stack(outs_p, 0), stack(outs_p, 1), stack(outs_p, 2), stack(outs_p, 3), stack(outs_p, 4),
            stack(outs_s, 0), stack(outs_s, 1), stack(outs_s, 2), stack(outs_s, 3), stack(outs_s, 4),
            jnp.stack(s_cv))
```

```python
import functools
import math

import jax
import jax.numpy as jnp
from jax import lax
from jax.experimental import pallas as pl
from jax.experimental.pallas import tpu as pltpu

F32 = jnp.float32
BF16 = jnp.bfloat16

RMS_EPS = 1e-6
CHUNK = 64
TOPK_MAX = 256
ROPE_THETA = 10000.0
C_CHUNK = 128
LANES = 128
HEAD_DIM = 128
NEG = -1e30
INT_MIN = -2147483648
SB_DEAD = -105.0
DSA_CHUNK = 256
DSA_QUERIES = 256
FFN_SPLIT = 2
VMEM_LIMIT_BYTES = 56 * 1024 * 1024

NT_DIMS = (((1,), (1,)), ((), ()))


def _params(*sem):
    return pltpu.CompilerParams(dimension_semantics=sem, vmem_limit_bytes=VMEM_LIMIT_BYTES)


def _rms(x, g):
    return x * lax.rsqrt(jnp.mean(x * x, axis=-1, keepdims=True) + RMS_EPS) * g


def _split_bf16(x):
    hi = x.astype(BF16)
    lo = (x - hi.astype(F32)).astype(BF16)
    return hi, lo


def _ffn_kernel(x_ref, g_ref, w1_ref, w3_ref, w2_ref, *rest, nf, final_norm, emit_bf16):
    rest = list(rest)
    gf_ref = rest.pop(0) if final_norm else None
    o_ref = rest.pop(0)
    wb_refs = [rest.pop(0) for _ in range(3)] if emit_bf16 else None
    h_scr, acc_scr = rest
    f = pl.program_id(1)

    @pl.when(f == 0)
    def _():
        h_scr[...] = _rms(x_ref[...], g_ref[...]).astype(BF16)
        acc_scr[...] = jnp.zeros_like(acc_scr)

    w1, w3, w2 = w1_ref[...], w3_ref[...], w2_ref[...]
    if emit_bf16:
        w1, w3, w2 = w1.astype(BF16), w3.astype(BF16), w2.astype(BF16)
        for ref, w in zip(wb_refs, (w1, w3, w2)):
            ref[...] = w
    h = h_scr[...]
    tf = w1.shape[1]
    sub = tf // FFN_SPLIT if tf % (FFN_SPLIT * LANES) == 0 else tf
    upd = None
    for c in range(tf // sub):
        cs = slice(c * sub, (c + 1) * sub)
        a = jnp.dot(h, w1[:, cs], preferred_element_type=F32)
        b = jnp.dot(h, w3[:, cs], preferred_element_type=F32)
        p = (a * jax.nn.sigmoid(a) * b).astype(BF16)
        d = jnp.dot(p, w2[cs, :], preferred_element_type=F32)
        upd = d if upd is None else upd + d
    acc_scr[...] += upd

    @pl.when(f == nf - 1)
    def _():
        y = x_ref[...] + 0.5 * acc_scr[...]
        if final_norm:
            y = _rms(y, gf_ref[...])
        o_ref[...] = y


def _ffn_tile(dff, target):
    return max(t for t in range(LANES, min(dff, target) + 1, LANES) if dff % t == 0)


def _half_ffn(x, g, w1, w3, w2, final_g=None, emit_bf16=False):
    m, d = x.shape
    dff = w1.shape[1]
    tm = min(512, m)
    tf = _ffn_tile(dff, 256 if emit_bf16 else 704)
    assert not emit_bf16 or m == tm
    nf = dff // tf
    w13_spec = pl.BlockSpec((d, tf), lambda i, f: (0, f))
    w2_spec = pl.BlockSpec((tf, d), lambda i, f: (f, 0))
    in_specs = [
        pl.BlockSpec((tm, d), lambda i, f: (i, 0)),
        pl.BlockSpec((1, d), lambda i, f: (0, 0)),
        w13_spec, w13_spec, w2_spec,
    ]
    args = [x, g.reshape(1, d), w1, w3, w2]
    if final_g is not None:
        in_specs.append(pl.BlockSpec((1, d), lambda i, f: (0, 0)))
        args.append(final_g.reshape(1, d))
    out_specs = [pl.BlockSpec((tm, d), lambda i, f: (i, 0))]
    out_shape = [jax.ShapeDtypeStruct((m, d), F32)]
    if emit_bf16:
        out_specs += [w13_spec, w13_spec, w2_spec]
        out_shape += [jax.ShapeDtypeStruct(w.shape, BF16) for w in (w1, w3, w2)]
    outs = pl.pallas_call(
        functools.partial(_ffn_kernel, nf=nf, final_norm=final_g is not None, emit_bf16=emit_bf16),
        grid=(m // tm, nf),
        in_specs=in_specs,
        out_specs=out_specs,
        out_shape=out_shape,
        scratch_shapes=[pltpu.VMEM((tm, d), BF16), pltpu.VMEM((tm, d), F32)],
        compiler_params=_params("parallel", "arbitrary"),
        name="half_ffn",
    )(*args)
    return outs if emit_bf16 else outs[0]


def _rope_heads(z, cos, sin):
    outs = []
    for h in range(z.shape[1] // LANES):
        zh = z[:, h * LANES:(h + 1) * LANES]
        outs.append(zh * cos + pltpu.roll(zh, LANES // 2, axis=1) * sin)
    return outs


def _rope_pairs(z, cos, sin):
    lane = lax.broadcasted_iota(jnp.int32, (z.shape[0], LANES), 1)
    low = (lane % 64) < 32
    outs = []
    for h in range(z.shape[1] // LANES):
        zh = z[:, h * LANES:(h + 1) * LANES]
        partner = jnp.where(low, pltpu.roll(zh, LANES - 32, axis=1), pltpu.roll(zh, 32, axis=1))
        outs.append(zh * cos + partner * sin)
    return outs


def _proj_a_kernel(x_ref, g_ref, w_ref, wkw_ref, c128_ref, s128_ref, c64_ref, s64_ref,
                   qa_ref, kaf_ref, kab_ref, vaf_ref, vab_ref, iq_ref, ikw_ref, h_scr,
                   *, n_idx_heads, idx_dim):
    j = pl.program_id(1)

    @pl.when(j == 0)
    def _():
        h_scr[...] = _rms(x_ref[...], g_ref[...]).astype(BF16)

    z = jnp.dot(h_scr[...], w_ref[...], preferred_element_type=F32)
    wa = qa_ref.shape[1]

    @pl.when(j == 0)
    def _():
        scale = HEAD_DIM ** -0.5
        for h, r in enumerate(_rope_heads(z[:, :wa], c128_ref[...], s128_ref[...])):
            qa_ref[:, h * LANES:(h + 1) * LANES] = (r * scale).astype(BF16)

    @pl.when(j == 1)
    def _():
        for h, r in enumerate(_rope_heads(z[:, :wa], c128_ref[...], s128_ref[...])):
            kaf_ref[:, h * LANES:(h + 1) * LANES] = r
            kab_ref[:, h * LANES:(h + 1) * LANES] = r.astype(BF16)

    @pl.when(j == 2)
    def _():
        kv = z[:, :wa]
        vaf_ref[...] = kv
        vab_ref[...] = kv.astype(BF16)

    @pl.when(j == 3)
    def _():
        wi = iq_ref.shape[1]
        for h, r in enumerate(_rope_pairs(z[:, :wi], c64_ref[...], s64_ref[...])):
            iq_ref[:, h * LANES:(h + 1) * LANES] = r * (idx_dim ** -0.5)
        zz = jnp.dot(h_scr[...], wkw_ref[...], preferred_element_type=F32)
        r = _rope_pairs(zz, c64_ref[...], s64_ref[...])[0]
        lane = lax.broadcasted_iota(jnp.int32, zz.shape, 1)
        ikw_ref[...] = jnp.where(lane < idx_dim, r, zz * (n_idx_heads ** -0.5))


def _proj_a(x, g, w, w_kw, tabs, n_pos_rows, ha, n_idx_heads, idx_dim):
    m, d = x.shape
    tm = min(512, m)
    wa = ha * HEAD_DIM
    wi = n_idx_heads * idx_dim
    tn = w.shape[1] // 4
    npb = n_pos_rows // tm
    row = lambda i, j: (i, 0)
    tab = lambda i, j: (i % npb, 0)
    tab_spec = pl.BlockSpec((tm, LANES), tab)
    return pl.pallas_call(
        functools.partial(_proj_a_kernel, n_idx_heads=n_idx_heads, idx_dim=idx_dim),
        grid=(m // tm, 4),
        in_specs=[
            pl.BlockSpec((tm, d), row),
            pl.BlockSpec((1, d), lambda i, j: (0, 0)),
            pl.BlockSpec((d, tn), lambda i, j: (0, j)),
            pl.BlockSpec((d, LANES), lambda i, j: (0, 0)),
            tab_spec, tab_spec, tab_spec, tab_spec,
        ],
        out_specs=[
            pl.BlockSpec((tm, wa), row), pl.BlockSpec((tm, wa), row), pl.BlockSpec((tm, wa), row),
            pl.BlockSpec((tm, wa), row), pl.BlockSpec((tm, wa), row),
            pl.BlockSpec((tm, wi), row), pl.BlockSpec((tm, LANES), row),
        ],
        out_shape=[
            jax.ShapeDtypeStruct((m, wa), BF16), jax.ShapeDtypeStruct((m, wa), F32),
            jax.ShapeDtypeStruct((m, wa), BF16), jax.ShapeDtypeStruct((m, wa), F32),
            jax.ShapeDtypeStruct((m, wa), BF16),
            jax.ShapeDtypeStruct((m, wi), F32), jax.ShapeDtypeStruct((m, LANES), F32),
        ],
        scratch_shapes=[pltpu.VMEM((tm, d), BF16)],
        compiler_params=_params("parallel", "arbitrary"),
        name="proj_a",
    )(x, g.reshape(1, d), w, w_kw, *tabs)


def _proj_a_t_kernel(x_ref, g_ref, w_ref, wkw_ref, c128_ref, s128_ref, c64_ref, s64_ref,
                     qt_ref, kaf_ref, kab_ref, vaf_ref, vt_ref, iq3t_ref, ikw_ref, ik3_ref, iwt_ref, h_scr,
                     *, n_idx_heads, idx_dim, tq, ck):
    j = pl.program_id(1)

    @pl.when(j == 0)
    def _():
        h_scr[...] = _rms(x_ref[...], g_ref[...]).astype(BF16)

    z = jnp.dot(h_scr[...], w_ref[...], preferred_element_type=F32)
    tm = z.shape[0]
    wa = kaf_ref.shape[1]

    @pl.when(j == 0)
    def _():
        scale = HEAD_DIM ** -0.5
        for h, r in enumerate(_rope_heads(z[:, :wa], c128_ref[...], s128_ref[...])):
            qt_ref[h * LANES:(h + 1) * LANES, :] = (r * scale).T.astype(BF16)

    @pl.when(j == 1)
    def _():
        for h, r in enumerate(_rope_heads(z[:, :wa], c128_ref[...], s128_ref[...])):
            kaf_ref[:, h * LANES:(h + 1) * LANES] = r
            kab_ref[:, h * LANES:(h + 1) * LANES] = r.astype(BF16)

    @pl.when(j == 2)
    def _():
        vaf_ref[...] = z[:, :wa]
        for h in range(wa // LANES):
            for c in range(tm // ck):
                vt_ref[c, h * LANES:(h + 1) * LANES, :] = \
                    z[c * ck:(c + 1) * ck, h * LANES:(h + 1) * LANES].T.astype(BF16)

    @pl.when(j == 3)
    def _():
        wi = n_idx_heads * idx_dim
        for g, r in enumerate(_rope_pairs(z[:, :wi], c64_ref[...], s64_ref[...])):
            r = r * (idx_dim ** -0.5)
            hi = r.astype(BF16).astype(F32)
            hi_t, lo_t = hi.T.astype(BF16), (r - hi).T.astype(BF16)
            for u in range(2):
                rows = slice(u * idx_dim, (u + 1) * idx_dim)
                for qb in range(tm // tq):
                    cols = slice(qb * tq, (qb + 1) * tq)
                    dst = slice((2 * g + u) * tq, (2 * g + u + 1) * tq)
                    iq3t_ref[qb, 0:idx_dim, dst] = hi_t[rows, cols]
                    iq3t_ref[qb, idx_dim:2 * idx_dim, dst] = lo_t[rows, cols]
                    iq3t_ref[qb, 2 * idx_dim:3 * idx_dim, dst] = hi_t[rows, cols]
        zz = jnp.dot(h_scr[...], wkw_ref[...], preferred_element_type=F32)
        r = _rope_pairs(zz, c64_ref[...], s64_ref[...])[0]
        lane = lax.broadcasted_iota(jnp.int32, zz.shape, 1)
        val = jnp.where(lane < idx_dim, r, zz * (n_idx_heads ** -0.5))
        ikw_ref[...] = val
        ik = val[:, 0:idx_dim]
        ik_hi = ik.astype(BF16).astype(F32)
        ik3_ref[...] = jnp.concatenate([ik_hi, ik_hi, ik - ik_hi], axis=1).astype(BF16)
        val_t = val.T
        for qb in range(tm // tq):
            iwt_ref[qb] = val_t[idx_dim:idx_dim + n_idx_heads, qb * tq:(qb + 1) * tq]


def _proj_a_t(x, g, w, w_kw, tabs, n_pos_rows, ha, n_idx_heads, idx_dim, tq, ck):
    m, d = x.shape
    tm = min(512, m)
    wa = ha * HEAD_DIM
    assert n_idx_heads * idx_dim <= wa and n_idx_heads % 2 == 0 and tm % ck == 0 and tm % tq == 0
    tn = w.shape[1] // 4
    npb = n_pos_rows // tm
    row = lambda i, j: (i, 0)
    lead = lambda i, j: (i, 0, 0)
    tab_spec = pl.BlockSpec((tm, LANES), lambda i, j: (i % npb, 0))
    return pl.pallas_call(
        functools.partial(_proj_a_t_kernel, n_idx_heads=n_idx_heads, idx_dim=idx_dim, tq=tq, ck=ck),
        grid=(m // tm, 4),
        in_specs=[
            pl.BlockSpec((tm, d), row),
            pl.BlockSpec((1, d), lambda i, j: (0, 0)),
            pl.BlockSpec((d, tn), lambda i, j: (0, j)),
            pl.BlockSpec((d, LANES), lambda i, j: (0, 0)),
            tab_spec, tab_spec, tab_spec, tab_spec,
        ],
        out_specs=[
            pl.BlockSpec((wa, tm), lambda i, j: (0, i)),
            pl.BlockSpec((tm, wa), row), pl.BlockSpec((tm, wa), row), pl.BlockSpec((tm, wa), row),
            pl.BlockSpec((tm // ck, wa, ck), lead),
            pl.BlockSpec((tm // tq, 3 * idx_dim, n_idx_heads * tq), lead),
            pl.BlockSpec((tm, LANES), row), pl.BlockSpec((tm, 3 * idx_dim), row),
            pl.BlockSpec((tm // tq, n_idx_heads, tq), lead),
        ],
        out_shape=[
            jax.ShapeDtypeStruct((wa, m), BF16),
            jax.ShapeDtypeStruct((m, wa), F32), jax.ShapeDtypeStruct((m, wa), BF16),
            jax.ShapeDtypeStruct((m, wa), F32),
            jax.ShapeDtypeStruct((m // ck, wa, ck), BF16),
            jax.ShapeDtypeStruct((m // tq, 3 * idx_dim, n_idx_heads * tq), BF16),
            jax.ShapeDtypeStruct((m, LANES), F32), jax.ShapeDtypeStruct((m, 3 * idx_dim), BF16),
            jax.ShapeDtypeStruct((m // tq, n_idx_heads, tq), F32),
        ],
        scratch_shapes=[pltpu.VMEM((tm, d), BF16)],
        compiler_params=_params("parallel", "arbitrary"),
        name="proj_a_t",
    )(x, g.reshape(1, d), w, w_kw, *tabs)


def _proj_b_kernel(x_ref, g_ref, w_ref, qb_ref, kbf_ref, kbb_ref, vbf_ref, vbb_ref, h_scr):
    j = pl.program_id(1)

    @pl.when(j == 0)
    def _():
        h_scr[...] = _rms(x_ref[...], g_ref[...]).astype(BF16)

    z = jnp.dot(h_scr[...], w_ref[...], preferred_element_type=F32)

    @pl.when(j == 0)
    def _():
        qb_ref[...] = (z * (HEAD_DIM ** -0.5)).astype(BF16)

    @pl.when(j == 1)
    def _():
        kbf_ref[...] = z
        kbb_ref[...] = z.astype(BF16)

    @pl.when(j == 2)
    def _():
        vbf_ref[...] = z
        vbb_ref[...] = z.astype(BF16)


def _proj_b(x, g, w):
    m, d = x.shape
    tm = min(512, m)
    wb = w.shape[1] // 3
    row = lambda i, j: (i, 0)
    return pl.pallas_call(
        _proj_b_kernel,
        grid=(m // tm, 3),
        in_specs=[
            pl.BlockSpec((tm, d), row),
            pl.BlockSpec((1, d), lambda i, j: (0, 0)),
            pl.BlockSpec((d, wb), lambda i, j: (0, j)),
        ],
        out_specs=[pl.BlockSpec((tm, wb), row)] * 5,
        out_shape=[
            jax.ShapeDtypeStruct((m, wb), BF16), jax.ShapeDtypeStruct((m, wb), F32),
            jax.ShapeDtypeStruct((m, wb), BF16), jax.ShapeDtypeStruct((m, wb), F32),
            jax.ShapeDtypeStruct((m, wb), BF16),
        ],
        scratch_shapes=[pltpu.VMEM((tm, d), BF16)],
        compiler_params=_params("parallel", "arbitrary"),
        name="proj_b",
    )(x, g.reshape(1, d), w)


def _key_to_f32(key):
    bits = jnp.where(key >= 0, key, key ^ jnp.int32(0x7FFFFFFF))
    return lax.bitcast_convert_type(bits, F32)


def _tree(parts, op):
    while len(parts) > 1:
        parts = [op(parts[k], parts[k + 1]) if k + 1 < len(parts) else parts[k] for k in range(0, len(parts), 2)]
    return parts[0]


def _fold_lanes(x, op):
    return _tree([x[:, t * LANES:(t + 1) * LANES] for t in range(x.shape[1] // LANES)], op)


def _dsa_mask(iq, iw, ik3_main, ik3_tail, tri_ref, s_scr, iq3_scr, limit, n_main, ck, n_top,
              n_idx_heads, idx_dim):
    tq = iq.shape[0]
    n_chunks = n_main + 1

    for h in range(n_idx_heads):
        x = iq[:, h * idx_dim:(h + 1) * idx_dim]
        hi = x.astype(BF16).astype(F32)
        iq3_scr[h * tq:(h + 1) * tq, :] = jnp.concatenate([hi, x - hi, hi], axis=1).astype(BF16)

    col0 = lax.broadcasted_iota(jnp.int32, (tq, ck), 1)

    def score_chunk(c, ikc):
        rel = lax.dot_general(iq3_scr[...], ikc, NT_DIMS, preferred_element_type=F32)
        acc = iw[:, 0:1] * jnp.maximum(rel[0:tq], 0.0)
        for h in range(1, n_idx_heads):
            acc = acc + iw[:, h:h + 1] * jnp.maximum(rel[h * tq:(h + 1) * tq], 0.0)
        s_scr[c] = jnp.where(col0 + c * ck < limit, acc, -jnp.inf)

    def score_main(c, _):
        score_chunk(c, ik3_main(c))
        return 0

    lax.fori_loop(0, n_main, score_main, 0)
    score_chunk(n_main, ik3_tail())

    def count(pred_fn):
        def body(c, acc):
            return acc + _fold_lanes(jnp.where(pred_fn(s_scr[c]), 1.0, 0.0), jnp.add)
        acc = lax.fori_loop(0, n_chunks, body, jnp.zeros((tq, LANES), F32))
        return jnp.sum(acc, axis=1, keepdims=True)

    kf = jnp.float32(n_top)
    cnt = count(lambda s: s >= 0.0)
    key = jnp.where(cnt >= kf, jnp.int32(0), jnp.int32(INT_MIN))

    def bit_step(b, key):
        cand = key | jnp.left_shift(jnp.int32(1), 30 - b)
        cand_f = _key_to_f32(cand)
        cnt = count(lambda s: s >= cand_f)
        return jnp.where(cnt >= kf, cand, key)

    key = lax.fori_loop(0, 31, bit_step, key)
    thr = _key_to_f32(key)
    need = kf - count(lambda s: s > thr)
    take_all = limit <= n_top

    def bias_chunk(c, carry):
        s = s_scr[c]
        eq = s == thr
        pre = jnp.dot(jnp.where(eq, 1.0, 0.0).astype(BF16), tri_ref[...], preferred_element_type=F32)
        tied = jnp.where((carry + pre) <= need, 0.0, NEG)
        bias = jnp.where(eq, tied, jnp.where(s > thr, 0.0, NEG))
        s_scr[c] = jnp.where(take_all, jnp.where(s > -jnp.inf, 0.0, NEG), bias)
        return carry + pre[:, ck - 1:ck]

    lax.fori_loop(0, n_chunks, bias_chunk, jnp.zeros((tq, 1), F32))


def _tri_incl(n):
    r = jnp.arange(n)
    return (r[:, None] <= r[None, :]).astype(BF16)


def _fold_rows(x, op):
    return _tree([x[r * 8:(r + 1) * 8] for r in range(x.shape[0] // 8)], op)


def _topk_bias_t(s_scr, n_pairs, limit, n_top, ltri_ref):
    ck, nl = s_scr.shape[1], s_scr.shape[2]

    def count(pred_fn):
        def body(pr, acc):
            a = _fold_rows(jnp.where(pred_fn(s_scr[2 * pr]), 1.0, 0.0), jnp.add)
            b = _fold_rows(jnp.where(pred_fn(s_scr[2 * pr + 1]), 1.0, 0.0), jnp.add)
            return acc + (a + b)
        acc = lax.fori_loop(0, n_pairs, body, jnp.zeros((8, nl), F32))
        return jnp.sum(acc, axis=0, keepdims=True)

    kf = jnp.float32(n_top)
    cnt = count(lambda s: s >= 0.0)
    state = (jnp.where(cnt >= kf, jnp.int32(0), jnp.int32(INT_MIN)), jnp.where(cnt >= kf, cnt, jnp.float32(2.0 ** 30)))

    def bit_step(b, state):
        key, n_ge = state
        cand = key | jnp.left_shift(jnp.int32(1), 30 - b)
        cand_f = _key_to_f32(cand)
        cnt = count(lambda s: s >= cand_f)
        return jnp.where(cnt >= kf, cand, key), jnp.where(cnt >= kf, cnt, n_ge)

    key, n_ge = lax.fori_loop(0, 31, bit_step, state)
    thr = _key_to_f32(key)
    take_all = limit <= n_top
    all_mask = lambda s: jnp.where(s > -jnp.inf, 0.0, NEG)

    def every_tie_taken():
        def chunk(c, _):
            s = s_scr[c]
            s_scr[c] = jnp.where(take_all, all_mask(s), jnp.where(s >= thr, 0.0, NEG))
            return 0
        lax.fori_loop(0, 2 * n_pairs, chunk, 0)

    def ties_by_index():
        need = kf - count(lambda s: s > thr)

        def chunk(c, carry):
            s = s_scr[c]
            eq = s == thr
            pre = jnp.dot(ltri_ref[...], jnp.where(eq, 1.0, 0.0).astype(BF16), preferred_element_type=F32)
            tied = jnp.where((carry + pre) <= need, 0.0, NEG)
            bias = jnp.where(eq, tied, jnp.where(s > thr, 0.0, NEG))
            s_scr[c] = jnp.where(take_all, all_mask(s), bias)
            return carry + pre[ck - 1:ck, :]
        lax.fori_loop(0, 2 * n_pairs, chunk, jnp.zeros((1, nl), F32))

    surplus = jnp.max(jnp.where(take_all, 0.0, n_ge - kf))
    lax.cond(surplus > 0.0, ties_by_index, every_tie_taken)


def _dsa_prompt_t_kernel(qt_ref, iq3t_ref, iwt_ref, ik3_ref, k_ref, vt_ref, ltri_ref, o_ref,
                         s_scr, acc_scr, lg_a, lg_b,
                         *, tq, ck, n_top, n_heads, n_idx_heads):
    i = pl.program_id(1)
    pos = i * tq + lax.broadcasted_iota(jnp.int32, (1, tq), 1)
    limit = (pos // CHUNK + 1) * CHUNK
    n_chunks = ((i + 1) * tq + ck - 1) // ck
    row0 = lax.broadcasted_iota(jnp.int32, (ck, tq), 0)
    iwt = iwt_ref[...]

    def rows(c):
        return pl.ds(pl.multiple_of(c * ck, ck), ck)

    def score_chunk(c, _):
        ikc = ik3_ref[rows(c), :]
        acc = None
        for g in range(n_idx_heads // 2):
            rel = jnp.dot(ikc, iq3t_ref[:, 2 * g * tq:2 * (g + 1) * tq], preferred_element_type=F32)
            for u in range(2):
                h = 2 * g + u
                term = iwt[h:h + 1, :] * jnp.maximum(rel[:, u * tq:(u + 1) * tq], 0.0)
                acc = term if acc is None else acc + term
        s_scr[c] = jnp.where(row0 + c * ck < limit, acc, -jnp.inf)
        return 0

    lax.fori_loop(0, n_chunks, score_chunk, 0)

    n_pairs = (n_chunks + 1) // 2
    last = 2 * n_pairs - 1

    @pl.when(n_chunks % 2 == 1)
    def _():
        s_scr[n_chunks] = jnp.full((ck, tq), -jnp.inf, F32)

    _topk_bias_t(s_scr, n_pairs, limit, n_top, ltri_ref)

    heads = [slice(h * HEAD_DIM, (h + 1) * HEAD_DIM) for h in range(n_heads)]
    acc_scr[...] = jnp.zeros_like(acc_scr)

    def logits(c, buf):
        for h, hs in enumerate(heads):
            buf[h] = jnp.dot(k_ref[rows(c), hs], qt_ref[hs, :], preferred_element_type=F32)

    def attend(c, buf, ms, ls):
        bias = s_scr[c]
        new_m, new_l = [], []
        for h, hs in enumerate(heads):
            lg = buf[h] + bias
            m_new = jnp.maximum(ms[h], jnp.max(_fold_rows(lg, jnp.maximum), axis=0, keepdims=True))
            alpha = jnp.exp(ms[h] - m_new)
            p = jnp.exp(lg - m_new)
            new_l.append(alpha * ls[h] + _fold_rows(p, jnp.add))
            acc_scr[hs, :] = alpha * acc_scr[hs, :] + jnp.dot(vt_ref[c, hs, :], p.astype(BF16),
                                                              preferred_element_type=F32)
            new_m.append(m_new)
        return tuple(new_m), tuple(new_l)

    def att_pair(pr, state):
        c0 = 2 * pr
        logits(c0 + 1, lg_b)
        state = attend(c0, lg_a, *state)
        logits(jnp.minimum(c0 + 2, last), lg_a)
        return attend(c0 + 1, lg_b, *state)

    logits(0, lg_a)
    init = (tuple(jnp.full((1, tq), NEG, F32) for _ in heads), tuple(jnp.zeros((8, tq), F32) for _ in heads))
    _, ls = lax.fori_loop(0, n_pairs, att_pair, init)
    for h, hs in enumerate(heads):
        o_t = acc_scr[hs, :] / jnp.sum(ls[h], axis=0, keepdims=True)
        o_ref[:, hs] = o_t.T.astype(o_ref.dtype)


def _dsa_prompt_t(qt, iq3t, iwt, ik3, ka, vt, b, tq, ck):
    wa, m = qt.shape
    s = m // b
    nq = s // tq
    n_heads = wa // HEAD_DIM
    n_idx_heads = iwt.shape[1]
    idx_dim = ik3.shape[1] // 3
    n_top = min(TOPK_MAX, s // 4)
    assert s % (2 * ck) == 0
    ik3 = ik3.reshape(b, s, 3 * idx_dim)
    ka = ka.reshape(b, s, wa)
    vt = vt.reshape(b, s // ck, wa, ck)
    r = jnp.arange(ck)
    ltri = (r[None, :] <= r[:, None]).astype(BF16)

    return pl.pallas_call(
        functools.partial(_dsa_prompt_t_kernel, tq=tq, ck=ck, n_top=n_top, n_heads=n_heads,
                          n_idx_heads=n_idx_heads),
        grid=(b, nq),
        in_specs=[
            pl.BlockSpec((wa, tq), lambda bb, i: (0, bb * nq + i)),
            pl.BlockSpec((None, 3 * idx_dim, n_idx_heads * tq), lambda bb, i: (bb * nq + i, 0, 0)),
            pl.BlockSpec((None, n_idx_heads, tq), lambda bb, i: (bb * nq + i, 0, 0)),
            pl.BlockSpec((None, s, 3 * idx_dim), lambda bb, i: (bb, 0, 0)),
            pl.BlockSpec((None, s, wa), lambda bb, i: (bb, 0, 0)),
            pl.BlockSpec((None, s // ck, wa, ck), lambda bb, i: (bb, 0, 0, 0)),
            pl.BlockSpec((ck, ck), lambda bb, i: (0, 0)),
        ],
        out_specs=pl.BlockSpec((None, tq, wa), lambda bb, i: (bb, i, 0)),
        out_shape=jax.ShapeDtypeStruct((b, s, wa), BF16),
        scratch_shapes=[pltpu.VMEM((s // ck, ck, tq), F32), pltpu.VMEM((wa, tq), F32),
                        pltpu.VMEM((n_heads, ck, tq), F32), pltpu.VMEM((n_heads, ck, tq), F32)],
        compiler_params=_params("parallel", "arbitrary"),
        name="dsa_prompt",
    )(qt, iq3t, iwt, ik3, ka, vt, ltri)


def _sb_tile(q_h, kt, vt, m2, carry, vis):
    z = lax.dot_general(q_h, kt, NT_DIMS, preferred_element_type=F32)
    sp = jnp.maximum(z, 0.0) + jnp.log1p(jnp.exp(-jnp.abs(z)))
    lk = -sp if vis is None else jnp.where(vis, -sp, 0.0)
    hi, lo = _split_bf16(lk)
    after = jnp.dot(jnp.concatenate([hi, lo], axis=1), m2, preferred_element_type=F32)
    a = jnp.exp(z - sp + after + carry)
    if vis is not None:
        a = jnp.where(vis, a, 0.0)
    return jnp.sum(lk, axis=1, keepdims=True), jnp.dot(a.astype(BF16), vt, preferred_element_type=F32)


def _sb_core(q_ref, diag_kv, past_kv, m2_ref, o_ref, acc_scr, n_past, n_heads, tk):
    tq = q_ref.shape[0]
    vis = lax.broadcasted_iota(jnp.int32, (tq, tk), 1) < lax.broadcasted_iota(jnp.int32, (tq, tk), 0)
    m2 = m2_ref[...]
    heads = [slice(h * HEAD_DIM, (h + 1) * HEAD_DIM) for h in range(n_heads)]

    carries = []
    for h, hs in enumerate(heads):
        kt, vt = diag_kv(h)
        dc, contrib = _sb_tile(q_ref[:, hs], kt, vt, m2, jnp.zeros((tq, 1), F32), vis)
        acc_scr[:, hs] = contrib
        carries.append(dc)

    def alive(cs):
        m = cs[0]
        for c in cs[1:]:
            m = jnp.maximum(m, c)
        return jnp.max(m)

    def cond(state):
        step, top, _ = state
        return jnp.logical_and(step < n_past, top > SB_DEAD)

    def body(state):
        step, _, cs = state
        j = n_past - 1 - step
        new = []
        for h, hs in enumerate(heads):
            kt, vt = past_kv(j, h)
            dc, contrib = _sb_tile(q_ref[:, hs], kt, vt, m2, cs[h], None)
            acc_scr[:, hs] += contrib
            new.append(cs[h] + dc)
        return step + 1, alive(new), tuple(new)

    lax.while_loop(cond, body, (jnp.int32(0), alive(carries), tuple(carries)))
    o_ref[...] = acc_scr[...].astype(o_ref.dtype)


def _tri_after(n):
    r = jnp.arange(n)
    m = (r[:, None] > r[None, :]).astype(BF16)
    return jnp.concatenate([m, m], axis=0)


def _sb_prompt_kernel(q_ref, k_ref, v_ref, m2_ref, o_ref, acc_scr, *, n_heads, tk):
    i = pl.program_id(1)

    def tile(j, h):
        rows = pl.ds(pl.multiple_of(j * tk, tk), tk)
        cols = slice(h * HEAD_DIM, (h + 1) * HEAD_DIM)
        return k_ref[rows, cols], v_ref[rows, cols]

    _sb_core(q_ref, lambda h: tile(i, h), tile, m2_ref, o_ref, acc_scr, i, n_heads, tk)


def _sb_prompt(qb, kb, vb, n_heads):
    b, s, wb = qb.shape
    tq = tk = 128
    qblk = lambda bb, i: (bb, i, 0)
    full = lambda bb, i: (bb, 0, 0)
    return pl.pallas_call(
        functools.partial(_sb_prompt_kernel, n_heads=n_heads, tk=tk),
        grid=(b, s // tq),
        in_specs=[
            pl.BlockSpec((None, tq, wb), qblk),
            pl.BlockSpec((None, s, wb), full),
            pl.BlockSpec((None, s, wb), full),
            pl.BlockSpec((2 * tk, tk), lambda bb, i: (0, 0)),
        ],
        out_specs=pl.BlockSpec((None, tq, wb), qblk),
        out_shape=jax.ShapeDtypeStruct((b, s, wb), BF16),
        scratch_shapes=[pltpu.VMEM((tq, wb), F32)],
        compiler_params=_params("parallel", "arbitrary"),
        name="sb_prompt",
    )(qb, kb, vb, _tri_after(tk))


def _pad_rows(new_ref, buf):
    t = new_ref.shape[0]
    buf[0:t, :] = new_ref[...].astype(buf.dtype)
    buf[t:, :] = jnp.zeros((buf.shape[0] - t, buf.shape[1]), buf.dtype)


def _ik3(x):
    hi, lo = _split_bf16(x)
    return jnp.concatenate([hi, hi, lo], axis=1)


def _dsa_sample_kernel(q_ref, iq_ref, ikw_ref, kn_ref, vn_ref, cik_ref, tri_ref, ck_hbm, cv_hbm,
                       o_ref, knew, vnew, iknew, s_scr, iq3_scr, kbuf, vbuf, sem,
                       *, past, t, ck, n_top, n_heads, n_idx_heads, idx_dim):
    b = pl.program_id(0)

    def head_copies(h):
        return (pltpu.make_async_copy(ck_hbm.at[b, :, h, :], kbuf.at[h], sem.at[0, h]),
                pltpu.make_async_copy(cv_hbm.at[b, :, h, :], vbuf.at[h], sem.at[1, h]))

    for h in range(n_heads):
        for cp in head_copies(h):
            cp.start()
    tn = knew.shape[0]
    _pad_rows(kn_ref, knew)
    _pad_rows(vn_ref, vnew)
    iknew[0:t, :] = ikw_ref[:, 0:idx_dim]
    iknew[t:, :] = jnp.zeros((ck - t, idx_dim), F32)
    limit = jnp.full((t, 1), past + t, jnp.int32)
    iw = ikw_ref[:, idx_dim:idx_dim + n_idx_heads]
    n_main = past // ck

    def ik3_main(c):
        return _ik3(cik_ref[pl.ds(pl.multiple_of(c * ck, ck), ck), :])

    _dsa_mask(iq_ref[...], iw, ik3_main, lambda: _ik3(iknew[...]), tri_ref, s_scr, iq3_scr,
              limit, n_main, ck, n_top, n_idx_heads, idx_dim)

    bias_past = jnp.concatenate([s_scr[c] for c in range(n_main)], axis=1)
    bias_new = s_scr[n_main][:, 0:tn]
    for h in range(n_heads):
        hs = slice(h * HEAD_DIM, (h + 1) * HEAD_DIM)
        q_h = q_ref[:, hs]
        for cp in head_copies(h):
            cp.wait()
        lg_p = lax.dot_general(q_h, kbuf[h].astype(BF16), NT_DIMS, preferred_element_type=F32) + bias_past
        lg_n = lax.dot_general(q_h, knew[:, hs], NT_DIMS, preferred_element_type=F32) + bias_new
        m = jnp.maximum(jnp.max(_fold_lanes(lg_p, jnp.maximum), axis=1, keepdims=True),
                        jnp.max(lg_n, axis=1, keepdims=True))
        p_p = jnp.exp(lg_p - m)
        p_n = jnp.exp(lg_n - m)
        l = jnp.sum(_fold_lanes(p_p, jnp.add), axis=1, keepdims=True) + jnp.sum(p_n, axis=1, keepdims=True)
        o = jnp.dot(p_p.astype(BF16), vbuf[h].astype(BF16), preferred_element_type=F32)
        o = o + jnp.dot(p_n.astype(BF16), vnew[:, hs], preferred_element_type=F32)
        o_ref[:, hs] = (o / l).astype(o_ref.dtype)


def _dsa_sample(qa, iq, ikw, ka, va, cache_k, cache_v, cache_ik, n_idx_heads):
    b, t, wa = qa.shape
    past, n_heads = cache_k.shape[1], cache_k.shape[2]
    idx_dim = cache_ik.shape[2]
    assert (past // CHUNK + 1) * CHUNK >= past + t, "new frames must sit in one open chunk"
    ck = DSA_CHUNK
    tn = LANES
    assert past % ck == 0 and t <= tn <= ck
    n_top = min(TOPK_MAX, (past + t) // 4)
    row = lambda bb: (bb, 0, 0)
    hbm = pl.BlockSpec(memory_space=pl.ANY)
    return pl.pallas_call(
        functools.partial(_dsa_sample_kernel, past=past, t=t, ck=ck, n_top=n_top, n_heads=n_heads,
                          n_idx_heads=n_idx_heads, idx_dim=idx_dim),
        grid=(b,),
        in_specs=[
            pl.BlockSpec((None, t, wa), row),
            pl.BlockSpec((None, t, iq.shape[2]), row),
            pl.BlockSpec((None, t, LANES), row),
            pl.BlockSpec((None, t, wa), row),
            pl.BlockSpec((None, t, wa), row),
            pl.BlockSpec((None, past, idx_dim), row),
            pl.BlockSpec((ck, ck), lambda bb: (0, 0)),
            hbm, hbm,
        ],
        out_specs=pl.BlockSpec((None, t, wa), row),
        out_shape=jax.ShapeDtypeStruct((b, t, wa), BF16),
        scratch_shapes=[
            pltpu.VMEM((tn, wa), BF16), pltpu.VMEM((tn, wa), BF16), pltpu.VMEM((ck, idx_dim), F32),
            pltpu.VMEM((past // ck + 1, t, ck), F32), pltpu.VMEM((n_idx_heads * t, 3 * idx_dim), BF16),
            pltpu.VMEM((n_heads, past, HEAD_DIM), F32), pltpu.VMEM((n_heads, past, HEAD_DIM), F32),
            pltpu.SemaphoreType.DMA((2, n_heads)),
        ],
        compiler_params=_params("arbitrary"),
        name="dsa_sample",
    )(qa, iq, ikw, ka, va, cache_ik, _tri_incl(ck), cache_k, cache_v)


def _sb_sample_kernel(q_ref, kn_ref, vn_ref, ck_ref, cv_ref, m2_ref, o_ref, knew, vnew, acc_scr,
                      *, past, t, tk, n_heads):
    for new_ref, buf in ((kn_ref, knew), (vn_ref, vnew)):
        buf[0:t, :] = new_ref[...]
        buf[t:, :] = jnp.zeros((tk - t, buf.shape[1]), BF16)

    def new_kv(h):
        cols = slice(h * HEAD_DIM, (h + 1) * HEAD_DIM)
        return knew[:, cols], vnew[:, cols]

    def cache_kv(j, h):
        rows = pl.ds(pl.multiple_of(j * tk, tk), tk)
        return ck_ref[rows, h, :].astype(BF16), cv_ref[rows, h, :].astype(BF16)

    _sb_core(q_ref, new_kv, cache_kv, m2_ref, o_ref, acc_scr, past // tk, n_heads, tk)


def _sb_sample(qb, kb, vb, cache_k, cache_v):
    b, t, wb = qb.shape
    past, n_heads = cache_k.shape[1], cache_k.shape[2]
    tk = 128
    assert past % tk == 0 and t <= tk
    row = lambda bb: (bb, 0, 0)
    cache = lambda bb: (bb, 0, 0, 0)
    return pl.pallas_call(
        functools.partial(_sb_sample_kernel, past=past, t=t, tk=tk, n_heads=n_heads),
        grid=(b,),
        in_specs=[
            pl.BlockSpec((None, t, wb), row),
            pl.BlockSpec((None, t, wb), row),
            pl.BlockSpec((None, t, wb), row),
            pl.BlockSpec((None, past, n_heads, HEAD_DIM), cache),
            pl.BlockSpec((None, past, n_heads, HEAD_DIM), cache),
            pl.BlockSpec((2 * tk, tk), lambda bb: (0, 0)),
        ],
        out_specs=pl.BlockSpec((None, t, wb), row),
        out_shape=jax.ShapeDtypeStruct((b, t, wb), BF16),
        scratch_shapes=[pltpu.VMEM((tk, wb), BF16), pltpu.VMEM((tk, wb), BF16), pltpu.VMEM((t, wb), F32)],
        compiler_params=_params("parallel"),
        name="sb_sample",
    )(qb, kb, vb, cache_k, cache_v, _tri_after(tk))


def _out_proj_kernel(y_ref, oa_ref, ob_ref, wa_ref, wb_ref, o_ref):
    acc = jnp.dot(oa_ref[...], wa_ref[...], preferred_element_type=F32)
    acc = acc + jnp.dot(ob_ref[...], wb_ref[...], preferred_element_type=F32)
    o_ref[...] = y_ref[...] + acc


def _out_proj(y, oa, ob, w_a, w_b):
    m, d = y.shape
    tm = min(512, m)
    row = lambda i: (i, 0)
    const = lambda i: (0, 0)
    return pl.pallas_call(
        _out_proj_kernel,
        grid=(m // tm,),
        in_specs=[
            pl.BlockSpec((tm, d), row),
            pl.BlockSpec((tm, oa.shape[1]), row),
            pl.BlockSpec((tm, ob.shape[1]), row),
            pl.BlockSpec(w_a.shape, const),
            pl.BlockSpec(w_b.shape, const),
        ],
        out_specs=pl.BlockSpec((tm, d), row),
        out_shape=jax.ShapeDtypeStruct((m, d), F32),
        compiler_params=_params("parallel"),
        name="out_proj",
    )(y, oa, ob, w_a, w_b)


def _gelu(x):
    return 0.5 * x * (1.0 + jnp.tanh(math.sqrt(2.0 / math.pi) * (x + 0.044715 * (x * x * x))))


def _c_v_kernel(x_ref, g_ref, w_ref, vg_ref, *outs):
    h = _rms(x_ref[...], g_ref[...]).astype(BF16)
    v = _gelu(jnp.dot(h, w_ref[...], preferred_element_type=F32))
    vn = _rms(v, vg_ref[...])
    outs[0][...] = vn.astype(BF16)
    if len(outs) > 1:
        outs[1][...] = vn


def _c_v(x, g, w_v, v_gain, want_f32):
    m, d = x.shape
    cw = w_v.shape[1]
    tm = min(512, m)
    row = lambda i: (i, 0)
    const = lambda i: (0, 0)
    out_specs = [pl.BlockSpec((tm, cw), row)]
    out_shape = [jax.ShapeDtypeStruct((m, cw), BF16)]
    if want_f32:
        out_specs.append(pl.BlockSpec((tm, cw), row))
        out_shape.append(jax.ShapeDtypeStruct((m, cw), F32))
    return pl.pallas_call(
        _c_v_kernel,
        grid=(m // tm,),
        in_specs=[
            pl.BlockSpec((tm, d), row),
            pl.BlockSpec((1, d), const),
            pl.BlockSpec((d, cw), const),
            pl.BlockSpec((1, cw), const),
        ],
        out_specs=out_specs,
        out_shape=out_shape,
        compiler_params=_params("parallel"),
        name="c_v",
    )(x, g.reshape(1, d), w_v, v_gain.reshape(1, cw))


def _c_mix_kernel(x_ref, g_ref, wu_ref, vn_ref, wm_ref, bias_ref, wo_ref, o_ref, h_scr, acc_scr, p_scr,
                  *, nj, gs):
    j = pl.program_id(1)

    @pl.when(j == 0)
    def _():
        h_scr[...] = _rms(x_ref[...], g_ref[...]).astype(BF16)
        acc_scr[...] = jnp.zeros_like(acc_scr)

    u = _gelu(jnp.dot(h_scr[...], wu_ref[...], preferred_element_type=F32))
    tm = u.shape[0]
    for gg in range(gs):
        cs = slice(gg * LANES, (gg + 1) * LANES)
        wm = wm_ref[j * gs + gg]
        for r in range(tm // C_CHUNK):
            rs = slice(r * C_CHUNK, (r + 1) * C_CHUNK)
            mix = jnp.dot(wm, vn_ref[rs, cs], preferred_element_type=F32) + bias_ref[:, cs]
            p_scr[rs, cs] = (u[rs, cs] * mix).astype(BF16)
    acc_scr[...] += jnp.dot(p_scr[...], wo_ref[...], preferred_element_type=F32)

    @pl.when(j == nj - 1)
    def _():
        o_ref[...] = x_ref[...] + acc_scr[...]


def _c_mix(x, g, w_u, vn, w_m, bias, w_o, gs=4):
    m, d = x.shape
    cw = w_u.shape[1]
    ng = cw // LANES
    gs = min(gs, ng)
    nj = ng // gs
    tm = min(512, m)
    row = lambda i, j: (i, 0)
    return pl.pallas_call(
        functools.partial(_c_mix_kernel, nj=nj, gs=gs),
        grid=(m // tm, nj),
        in_specs=[
            pl.BlockSpec((tm, d), row),
            pl.BlockSpec((1, d), lambda i, j: (0, 0)),
            pl.BlockSpec((d, gs * LANES), lambda i, j: (0, j)),
            pl.BlockSpec((tm, gs * LANES), lambda i, j: (i, j)),
            pl.BlockSpec(w_m.shape, lambda i, j: (0, 0, 0)),
            pl.BlockSpec((C_CHUNK, gs * LANES), lambda i, j: (0, j)),
            pl.BlockSpec((gs * LANES, d), lambda i, j: (j, 0)),
        ],
        out_specs=pl.BlockSpec((tm, d), row),
        out_shape=jax.ShapeDtypeStruct((m, d), F32),
        scratch_shapes=[pltpu.VMEM((tm, d), BF16), pltpu.VMEM((tm, d), F32),
                        pltpu.VMEM((tm, gs * LANES), BF16)],
        compiler_params=_params("parallel", "arbitrary"),
        name="c_mix",
    )(x, g.reshape(1, d), w_u, vn, w_m, bias, w_o)


def _rope_tables(pos, reps):
    pos = pos.astype(F32)[:, None]

    def tab(half, copies):
        inv = ROPE_THETA ** (-jnp.arange(half, dtype=F32) / half)
        ang = pos * inv[None, :]
        c, s = jnp.cos(ang), jnp.sin(ang)
        return jnp.tile(jnp.concatenate([c, c], axis=1), (reps, copies)), \
            jnp.tile(jnp.concatenate([-s, s], axis=1), (reps, copies))

    c128, s128 = tab(HEAD_DIM // 2, 1)
    c64, s64 = tab(32, 2)
    return c128, s128, c64, s64


def kernel(x_prompt, x_sample, cache_a_k, cache_a_v, cache_a_ik, cache_b_k, cache_b_v, norm_ff1, ff1_w1, ff1_w3, ff1_w2, norm_mix, norm_ff2, ff2_w1, ff2_w3, ff2_w2, ab_w_in, ab_w_out, c_w_in, c_v_norm, c_w_s, c_b_s, c_w_out, final_norm):
    bp, seq, d = x_prompt.shape
    bs, t, _ = x_sample.shape
    past, ha = cache_a_k.shape[2], cache_a_k.shape[3]
    hb = cache_b_k.shape[3]
    idx_dim = cache_a_ik.shape[3]
    wa, wb = ha * HEAD_DIM, hb * HEAD_DIM
    n_idx_heads = (ab_w_in.shape[2] - 3 * wa - 3 * wb - idx_dim) // (idx_dim + 1)
    wi = n_idx_heads * idx_dim
    assert idx_dim == 64 and wi % LANES == 0 and wi <= wa and n_idx_heads <= LANES - idx_dim
    depth = norm_ff1.shape[0]
    mp, ms = bp * seq, bs * t

    yp = x_prompt.reshape(mp, d)
    ys = x_sample.reshape(ms, d)
    bf = lambda w: w.astype(BF16)

    tabs_p = _rope_tables(jnp.arange(seq), 1)
    tabs_s = _rope_tables(past + jnp.arange(t), min(512, ms) // t)

    outs_p, outs_s, s_cv = [], [], []
    for layer in range(depth):
        j = layer // 2
        ys, w1, w3, w2 = _half_ffn(ys, norm_ff1[layer], ff1_w1[layer], ff1_w3[layer], ff1_w2[layer],
                                   emit_bf16=True)
        yp = _half_ffn(yp, norm_ff1[layer], w1, w3, w2)
        if layer % 2 == 0:
            w_in = ab_w_in[j]
            o = 0
            cols = []
            for width in (wa, wa, wa, wi, idx_dim, n_idx_heads, wb, wb, wb):
                cols.append(w_in[:, o:o + width])
                o += width
            w_qa, w_ka, w_va, w_iq, w_ik, w_iw, w_qb, w_kb, w_vb = cols
            pad = lambda w, width: jnp.pad(w, ((0, 0), (0, width - w.shape[1])))
            w_pa = bf(jnp.concatenate([w_qa, w_ka, w_va, pad(w_iq, wa)], axis=1))
            w_kw = bf(pad(jnp.concatenate([w_ik, w_iw], axis=1), LANES))
            w_pb = bf(jnp.concatenate([w_qb, w_kb, w_vb], axis=1))
            w_oa, w_ob = bf(ab_w_out[j][:wa]), bf(ab_w_out[j][wa:])

            tq = DSA_QUERIES
            qt, kaf, kab, vaf, vt, iq3t, ikw, ik3, iwt = _proj_a_t(
                yp, norm_mix[layer], w_pa, w_kw, tabs_p, seq, ha, n_idx_heads, idx_dim, tq, DSA_CHUNK)
            qb, kbf, kbb, vbf, vbb = _proj_b(yp, norm_mix[layer], w_pb)
            r3 = lambda a: a.reshape(bp, seq, a.shape[-1])
            ik = ikw[:, :idx_dim]
            o_a = _dsa_prompt_t(qt, iq3t, iwt, ik3, kab, vt, bp, tq, DSA_CHUNK)
            o_b = _sb_prompt(r3(qb), r3(kbb), r3(vbb), hb)
            yp = _out_proj(yp, o_a.reshape(mp, wa), o_b.reshape(mp, wb), w_oa, w_ob)
            outs_p.append((kaf.reshape(bp, seq, ha, HEAD_DIM), vaf.reshape(bp, seq, ha, HEAD_DIM),
                           ik.reshape(bp, seq, idx_dim),
                           kbf.reshape(bp, seq, hb, HEAD_DIM), vbf.reshape(bp, seq, hb, HEAD_DIM)))

            qa, kaf, kab, vaf, vab, iq, ikw = _proj_a(
                ys, norm_mix[layer], w_pa, w_kw, tabs_s, min(512, ms), ha, n_idx_heads, idx_dim)
            qb, kbf, kbb, vbf, vbb = _proj_b(ys, norm_mix[layer], w_pb)
            r3 = lambda a: a.reshape(bs, t, a.shape[-1])
            o_a = _dsa_sample(r3(qa), r3(iq), r3(ikw), r3(kab), r3(vab),
                              cache_a_k[j], cache_a_v[j], cache_a_ik[j], n_idx_heads)
            o_b = _sb_sample(r3(qb), r3(kbb), r3(vbb), cache_b_k[j], cache_b_v[j])
            ys = _out_proj(ys, o_a.reshape(ms, wa), o_b.reshape(ms, wb), w_oa, w_ob)
            outs_s.append((kaf.reshape(bs, t, ha, HEAD_DIM), vaf.reshape(bs, t, ha, HEAD_DIM),
                           ikw[:, :idx_dim].reshape(bs, t, idx_dim),
                           kbf.reshape(bs, t, hb, HEAD_DIM), vbf.reshape(bs, t, hb, HEAD_DIM)))
        else:
            cw = c_w_in.shape[2] // 2
            ng = c_w_s.shape[1]
            w_u, w_v = bf(c_w_in[j][:, :cw]), bf(c_w_in[j][:, cw:])
            w_o = bf(c_w_out[j])
            i = jnp.arange(C_CHUNK)
            mask = (i[None, :] // CHUNK) <= (i[:, None] // CHUNK)
            w_m = jnp.where(mask[None], c_w_s[j], 0.0)
            bias_p = jnp.repeat(c_b_s[j].T, cw // ng, axis=1)
            per = C_CHUNK // t
            w_ms = jnp.einsum('ab,gij->gaibj', jnp.eye(per, dtype=F32), w_m[:, :t, :t]).reshape(ng, C_CHUNK, C_CHUNK)
            bias_s = jnp.tile(bias_p[:t], (per, 1))

            vn = _c_v(yp, norm_mix[layer], w_v, c_v_norm[j], False)[0]
            yp = _c_mix(yp, norm_mix[layer], w_u, vn, bf(w_m), bias_p, w_o)
            vn, vn_f32 = _c_v(ys, norm_mix[layer], w_v, c_v_norm[j], True)
            ys = _c_mix(ys, norm_mix[layer], w_u, vn, bf(w_ms), bias_s, w_o)
            s_cv.append(vn_f32.reshape(bs, t, cw))
        last = layer == depth - 1
        fg = final_norm if last else None
        ys, w1, w3, w2 = _half_ffn(ys, norm_ff2[layer], ff2_w1[layer], ff2_w3[layer], ff2_w2[layer], fg,
                                   emit_bf16=True)
        yp = _half_ffn(yp, norm_ff2[layer], w1, w3, w2, fg)

    stack = lambda outs, k: jnp.stack([o[k] for o in outs])
    return (yp.reshape(bp, seq, d), ys.reshape(bs, t, d),
            stack(outs_p, 0), stack(outs_p, 1), stack(outs_p, 2), stack(outs_p, 3), stack(outs_p, 4),
            stack(outs_s, 0), stack(outs_s, 1), stack(outs_s, 2), stack(outs_s, 3), stack(outs_s, 4),
            jnp.stack(s_cv))
```

```python
import functools
import math

import jax
import jax.numpy as jnp
from jax import lax
from jax.experimental import pallas as pl
from jax.experimental.pallas import tpu as pltpu

F32 = jnp.float32
BF16 = jnp.bfloat16

RMS_EPS = 1e-6
CHUNK = 64
TOPK_MAX = 256
ROPE_THETA = 10000.0
C_CHUNK = 128
LANES = 128
HEAD_DIM = 128
NEG = -1e30
INT_MIN = -2147483648
SB_DEAD = -105.0
DSA_CHUNK = 256
DSA_QUERIES = 256
FFN_SPLIT = 2
VMEM_LIMIT_BYTES = 56 * 1024 * 1024

NT_DIMS = (((1,), (1,)), ((), ()))


def _params(*sem):
    return pltpu.CompilerParams(dimension_semantics=sem, vmem_limit_bytes=VMEM_LIMIT_BYTES)


def _rms(x, g):
    return x * lax.rsqrt(jnp.mean(x * x, axis=-1, keepdims=True) + RMS_EPS) * g


def _split_bf16(x):
    hi = x.astype(BF16)
    lo = (x - hi.astype(F32)).astype(BF16)
    return hi, lo


def _ffn_kernel(x_ref, g_ref, w1_ref, w3_ref, w2_ref, *rest, nf, final_norm, emit_bf16):
    rest = list(rest)
    gf_ref = rest.pop(0) if final_norm else None
    o_ref = rest.pop(0)
    wb_refs = [rest.pop(0) for _ in range(3)] if emit_bf16 else None
    h_scr, acc_scr = rest
    f = pl.program_id(1)

    @pl.when(f == 0)
    def _():
        h_scr[...] = _rms(x_ref[...], g_ref[...]).astype(BF16)
        acc_scr[...] = jnp.zeros_like(acc_scr)

    w1, w3, w2 = w1_ref[...], w3_ref[...], w2_ref[...]
    if emit_bf16:
        w1, w3, w2 = w1.astype(BF16), w3.astype(BF16), w2.astype(BF16)
        for ref, w in zip(wb_refs, (w1, w3, w2)):
            ref[...] = w
    h = h_scr[...]
    tf = w1.shape[1]
    sub = tf // FFN_SPLIT if tf % (FFN_SPLIT * LANES) == 0 else tf
    cols = [slice(c * sub, (c + 1) * sub) for c in range(tf // sub)]
    ab = [(jnp.dot(h, w1[:, cs], preferred_element_type=F32), jnp.dot(h, w3[:, cs], preferred_element_type=F32))
          for cs in cols]
    upd = None
    for (a, b), cs in zip(ab, cols):
        p = (a * jax.nn.sigmoid(a) * b).astype(BF16)
        d = jnp.dot(p, w2[cs, :], preferred_element_type=F32)
        upd = d if upd is None else upd + d
    acc_scr[...] += upd

    @pl.when(f == nf - 1)
    def _():
        y = x_ref[...] + 0.5 * acc_scr[...]
        if final_norm:
            y = _rms(y, gf_ref[...])
        o_ref[...] = y


def _ffn_tile(dff, target):
    return max(t for t in range(LANES, min(dff, target) + 1, LANES) if dff % t == 0)


def _half_ffn(x, g, w1, w3, w2, final_g=None, emit_bf16=False):
    m, d = x.shape
    dff = w1.shape[1]
    tm = min(512, m)
    tf = _ffn_tile(dff, 256 if emit_bf16 else 704)
    assert not emit_bf16 or m == tm
    nf = dff // tf
    w13_spec = pl.BlockSpec((d, tf), lambda i, f: (0, f))
    w2_spec = pl.BlockSpec((tf, d), lambda i, f: (f, 0))
    in_specs = [
        pl.BlockSpec((tm, d), lambda i, f: (i, 0)),
        pl.BlockSpec((1, d), lambda i, f: (0, 0)),
        w13_spec, w13_spec, w2_spec,
    ]
    args = [x, g.reshape(1, d), w1, w3, w2]
    if final_g is not None:
        in_specs.append(pl.BlockSpec((1, d), lambda i, f: (0, 0)))
        args.append(final_g.reshape(1, d))
    out_specs = [pl.BlockSpec((tm, d), lambda i, f: (i, 0))]
    out_shape = [jax.ShapeDtypeStruct((m, d), F32)]
    if emit_bf16:
        out_specs += [w13_spec, w13_spec, w2_spec]
        out_shape += [jax.ShapeDtypeStruct(w.shape, BF16) for w in (w1, w3, w2)]
    outs = pl.pallas_call(
        functools.partial(_ffn_kernel, nf=nf, final_norm=final_g is not None, emit_bf16=emit_bf16),
        grid=(m // tm, nf),
        in_specs=in_specs,
        out_specs=out_specs,
        out_shape=out_shape,
        scratch_shapes=[pltpu.VMEM((tm, d), BF16), pltpu.VMEM((tm, d), F32)],
        compiler_params=_params("parallel", "arbitrary"),
        name="half_ffn",
    )(*args)
    return outs if emit_bf16 else outs[0]


def _rope_heads(z, cos, sin):
    outs = []
    for h in range(z.shape[1] // LANES):
        zh = z[:, h * LANES:(h + 1) * LANES]
        outs.append(zh * cos + pltpu.roll(zh, LANES // 2, axis=1) * sin)
    return outs


def _rope_pairs(z, cos, sin):
    lane = lax.broadcasted_iota(jnp.int32, (z.shape[0], LANES), 1)
    low = (lane % 64) < 32
    outs = []
    for h in range(z.shape[1] // LANES):
        zh = z[:, h * LANES:(h + 1) * LANES]
        partner = jnp.where(low, pltpu.roll(zh, LANES - 32, axis=1), pltpu.roll(zh, 32, axis=1))
        outs.append(zh * cos + partner * sin)
    return outs


def _proj_a_kernel(x_ref, g_ref, w_ref, wkw_ref, c128_ref, s128_ref, c64_ref, s64_ref,
                   qa_ref, kaf_ref, kab_ref, vaf_ref, vab_ref, iq_ref, ikw_ref, h_scr,
                   *, n_idx_heads, idx_dim):
    j = pl.program_id(1)

    @pl.when(j == 0)
    def _():
        h_scr[...] = _rms(x_ref[...], g_ref[...]).astype(BF16)

    z = jnp.dot(h_scr[...], w_ref[...], preferred_element_type=F32)
    wa = qa_ref.shape[1]

    @pl.when(j == 0)
    def _():
        scale = HEAD_DIM ** -0.5
        for h, r in enumerate(_rope_heads(z[:, :wa], c128_ref[...], s128_ref[...])):
            qa_ref[:, h * LANES:(h + 1) * LANES] = (r * scale).astype(BF16)

    @pl.when(j == 1)
    def _():
        for h, r in enumerate(_rope_heads(z[:, :wa], c128_ref[...], s128_ref[...])):
            kaf_ref[:, h * LANES:(h + 1) * LANES] = r
            kab_ref[:, h * LANES:(h + 1) * LANES] = r.astype(BF16)

    @pl.when(j == 2)
    def _():
        kv = z[:, :wa]
        vaf_ref[...] = kv
        vab_ref[...] = kv.astype(BF16)

    @pl.when(j == 3)
    def _():
        wi = iq_ref.shape[1]
        for h, r in enumerate(_rope_pairs(z[:, :wi], c64_ref[...], s64_ref[...])):
            iq_ref[:, h * LANES:(h + 1) * LANES] = r * (idx_dim ** -0.5)
        zz = jnp.dot(h_scr[...], wkw_ref[...], preferred_element_type=F32)
        r = _rope_pairs(zz, c64_ref[...], s64_ref[...])[0]
        lane = lax.broadcasted_iota(jnp.int32, zz.shape, 1)
        ikw_ref[...] = jnp.where(lane < idx_dim, r, zz * (n_idx_heads ** -0.5))


def _proj_a(x, g, w, w_kw, tabs, n_pos_rows, ha, n_idx_heads, idx_dim):
    m, d = x.shape
    tm = min(512, m)
    wa = ha * HEAD_DIM
    wi = n_idx_heads * idx_dim
    tn = w.shape[1] // 4
    npb = n_pos_rows // tm
    row = lambda i, j: (i, 0)
    tab = lambda i, j: (i % npb, 0)
    tab_spec = pl.BlockSpec((tm, LANES), tab)
    return pl.pallas_call(
        functools.partial(_proj_a_kernel, n_idx_heads=n_idx_heads, idx_dim=idx_dim),
        grid=(m // tm, 4),
        in_specs=[
            pl.BlockSpec((tm, d), row),
            pl.BlockSpec((1, d), lambda i, j: (0, 0)),
            pl.BlockSpec((d, tn), lambda i, j: (0, j)),
            pl.BlockSpec((d, LANES), lambda i, j: (0, 0)),
            tab_spec, tab_spec, tab_spec, tab_spec,
        ],
        out_specs=[
            pl.BlockSpec((tm, wa), row), pl.BlockSpec((tm, wa), row), pl.BlockSpec((tm, wa), row),
            pl.BlockSpec((tm, wa), row), pl.BlockSpec((tm, wa), row),
            pl.BlockSpec((tm, wi), row), pl.BlockSpec((tm, LANES), row),
        ],
        out_shape=[
            jax.ShapeDtypeStruct((m, wa), BF16), jax.ShapeDtypeStruct((m, wa), F32),
            jax.ShapeDtypeStruct((m, wa), BF16), jax.ShapeDtypeStruct((m, wa), F32),
            jax.ShapeDtypeStruct((m, wa), BF16),
            jax.ShapeDtypeStruct((m, wi), F32), jax.ShapeDtypeStruct((m, LANES), F32),
        ],
        scratch_shapes=[pltpu.VMEM((tm, d), BF16)],
        compiler_params=_params("parallel", "arbitrary"),
        name="proj_a",
    )(x, g.reshape(1, d), w, w_kw, *tabs)


def _proj_a_t_kernel(x_ref, g_ref, w_ref, wkw_ref, c128_ref, s128_ref, c64_ref, s64_ref,
                     qt_ref, kaf_ref, kab_ref, vaf_ref, vt_ref, iq3t_ref, ikw_ref, ik3_ref, iwt_ref, h_scr,
                     *, n_idx_heads, idx_dim, tq, ck):
    j = pl.program_id(1)

    @pl.when(j == 0)
    def _():
        h_scr[...] = _rms(x_ref[...], g_ref[...]).astype(BF16)

    z = jnp.dot(h_scr[...], w_ref[...], preferred_element_type=F32)
    tm = z.shape[0]
    wa = kaf_ref.shape[1]

    @pl.when(j == 0)
    def _():
        scale = HEAD_DIM ** -0.5
        for h, r in enumerate(_rope_heads(z[:, :wa], c128_ref[...], s128_ref[...])):
            qt_ref[h * LANES:(h + 1) * LANES, :] = (r * scale).T.astype(BF16)

    @pl.when(j == 1)
    def _():
        for h, r in enumerate(_rope_heads(z[:, :wa], c128_ref[...], s128_ref[...])):
            kaf_ref[:, h * LANES:(h + 1) * LANES] = r
            kab_ref[:, h * LANES:(h + 1) * LANES] = r.astype(BF16)

    @pl.when(j == 2)
    def _():
        vaf_ref[...] = z[:, :wa]
        for h in range(wa // LANES):
            for c in range(tm // ck):
                vt_ref[c, h * LANES:(h + 1) * LANES, :] = \
                    z[c * ck:(c + 1) * ck, h * LANES:(h + 1) * LANES].T.astype(BF16)

    @pl.when(j == 3)
    def _():
        wi = n_idx_heads * idx_dim
        for g, r in enumerate(_rope_pairs(z[:, :wi], c64_ref[...], s64_ref[...])):
            r = r * (idx_dim ** -0.5)
            hi = r.astype(BF16).astype(F32)
            hi_t, lo_t = hi.T.astype(BF16), (r - hi).T.astype(BF16)
            for u in range(2):
                rows = slice(u * idx_dim, (u + 1) * idx_dim)
                for qb in range(tm // tq):
                    cols = slice(qb * tq, (qb + 1) * tq)
                    dst = slice((2 * g + u) * tq, (2 * g + u + 1) * tq)
                    iq3t_ref[qb, 0:idx_dim, dst] = hi_t[rows, cols]
                    iq3t_ref[qb, idx_dim:2 * idx_dim, dst] = lo_t[rows, cols]
                    iq3t_ref[qb, 2 * idx_dim:3 * idx_dim, dst] = hi_t[rows, cols]
        zz = jnp.dot(h_scr[...], wkw_ref[...], preferred_element_type=F32)
        r = _rope_pairs(zz, c64_ref[...], s64_ref[...])[0]
        lane = lax.broadcasted_iota(jnp.int32, zz.shape, 1)
        val = jnp.where(lane < idx_dim, r, zz * (n_idx_heads ** -0.5))
        ikw_ref[...] = val
        ik = val[:, 0:idx_dim]
        ik_hi = ik.astype(BF16).astype(F32)
        ik3_ref[...] = jnp.concatenate([ik_hi, ik_hi, ik - ik_hi], axis=1).astype(BF16)
        val_t = val.T
        for qb in range(tm // tq):
            iwt_ref[qb] = val_t[idx_dim:idx_dim + n_idx_heads, qb * tq:(qb + 1) * tq]


def _proj_a_t(x, g, w, w_kw, tabs, n_pos_rows, ha, n_idx_heads, idx_dim, tq, ck):
    m, d = x.shape
    tm = min(512, m)
    wa = ha * HEAD_DIM
    assert n_idx_heads * idx_dim <= wa and n_idx_heads % 2 == 0 and tm % ck == 0 and tm % tq == 0
    tn = w.shape[1] // 4
    npb = n_pos_rows // tm
    row = lambda i, j: (i, 0)
    lead = lambda i, j: (i, 0, 0)
    tab_spec = pl.BlockSpec((tm, LANES), lambda i, j: (i % npb, 0))
    return pl.pallas_call(
        functools.partial(_proj_a_t_kernel, n_idx_heads=n_idx_heads, idx_dim=idx_dim, tq=tq, ck=ck),
        grid=(m // tm, 4),
        in_specs=[
            pl.BlockSpec((tm, d), row),
            pl.BlockSpec((1, d), lambda i, j: (0, 0)),
            pl.BlockSpec((d, tn), lambda i, j: (0, j)),
            pl.BlockSpec((d, LANES), lambda i, j: (0, 0)),
            tab_spec, tab_spec, tab_spec, tab_spec,
        ],
        out_specs=[
            pl.BlockSpec((wa, tm), lambda i, j: (0, i)),
            pl.BlockSpec((tm, wa), row), pl.BlockSpec((tm, wa), row), pl.BlockSpec((tm, wa), row),
            pl.BlockSpec((tm // ck, wa, ck), lead),
            pl.BlockSpec((tm // tq, 3 * idx_dim, n_idx_heads * tq), lead),
            pl.BlockSpec((tm, LANES), row), pl.BlockSpec((tm, 3 * idx_dim), row),
            pl.BlockSpec((tm // tq, n_idx_heads, tq), lead),
        ],
        out_shape=[
            jax.ShapeDtypeStruct((wa, m), BF16),
            jax.ShapeDtypeStruct((m, wa), F32), jax.ShapeDtypeStruct((m, wa), BF16),
            jax.ShapeDtypeStruct((m, wa), F32),
            jax.ShapeDtypeStruct((m // ck, wa, ck), BF16),
            jax.ShapeDtypeStruct((m // tq, 3 * idx_dim, n_idx_heads * tq), BF16),
            jax.ShapeDtypeStruct((m, LANES), F32), jax.ShapeDtypeStruct((m, 3 * idx_dim), BF16),
            jax.ShapeDtypeStruct((m // tq, n_idx_heads, tq), F32),
        ],
        scratch_shapes=[pltpu.VMEM((tm, d), BF16)],
        compiler_params=_params("parallel", "arbitrary"),
        name="proj_a_t",
    )(x, g.reshape(1, d), w, w_kw, *tabs)


def _proj_b_kernel(x_ref, g_ref, w_ref, qb_ref, kbf_ref, kbb_ref, vbf_ref, vbb_ref, h_scr):
    j = pl.program_id(1)

    @pl.when(j == 0)
    def _():
        h_scr[...] = _rms(x_ref[...], g_ref[...]).astype(BF16)

    z = jnp.dot(h_scr[...], w_ref[...], preferred_element_type=F32)

    @pl.when(j == 0)
    def _():
        qb_ref[...] = (z * (HEAD_DIM ** -0.5)).astype(BF16)

    @pl.when(j == 1)
    def _():
        kbf_ref[...] = z
        kbb_ref[...] = z.astype(BF16)

    @pl.when(j == 2)
    def _():
        vbf_ref[...] = z
        vbb_ref[...] = z.astype(BF16)


def _proj_b(x, g, w):
    m, d = x.shape
    tm = min(512, m)
    wb = w.shape[1] // 3
    row = lambda i, j: (i, 0)
    return pl.pallas_call(
        _proj_b_kernel,
        grid=(m // tm, 3),
        in_specs=[
            pl.BlockSpec((tm, d), row),
            pl.BlockSpec((1, d), lambda i, j: (0, 0)),
            pl.BlockSpec((d, wb), lambda i, j: (0, j)),
        ],
        out_specs=[pl.BlockSpec((tm, wb), row)] * 5,
        out_shape=[
            jax.ShapeDtypeStruct((m, wb), BF16), jax.ShapeDtypeStruct((m, wb), F32),
            jax.ShapeDtypeStruct((m, wb), BF16), jax.ShapeDtypeStruct((m, wb), F32),
            jax.ShapeDtypeStruct((m, wb), BF16),
        ],
        scratch_shapes=[pltpu.VMEM((tm, d), BF16)],
        compiler_params=_params("parallel", "arbitrary"),
        name="proj_b",
    )(x, g.reshape(1, d), w)


def _key_to_f32(key):
    bits = jnp.where(key >= 0, key, key ^ jnp.int32(0x7FFFFFFF))
    return lax.bitcast_convert_type(bits, F32)


def _tree(parts, op):
    while len(parts) > 1:
        parts = [op(parts[k], parts[k + 1]) if k + 1 < len(parts) else parts[k] for k in range(0, len(parts), 2)]
    return parts[0]


def _fold_lanes(x, op):
    return _tree([x[:, t * LANES:(t + 1) * LANES] for t in range(x.shape[1] // LANES)], op)


def _dsa_mask(iq, iw, ik3_main, ik3_tail, tri_ref, s_scr, iq3_scr, limit, n_main, ck, n_top,
              n_idx_heads, idx_dim):
    tq = iq.shape[0]
    n_chunks = n_main + 1

    for h in range(n_idx_heads):
        x = iq[:, h * idx_dim:(h + 1) * idx_dim]
        hi = x.astype(BF16).astype(F32)
        iq3_scr[h * tq:(h + 1) * tq, :] = jnp.concatenate([hi, x - hi, hi], axis=1).astype(BF16)

    col0 = lax.broadcasted_iota(jnp.int32, (tq, ck), 1)

    def score_chunk(c, ikc):
        rel = lax.dot_general(iq3_scr[...], ikc, NT_DIMS, preferred_element_type=F32)
        acc = iw[:, 0:1] * jnp.maximum(rel[0:tq], 0.0)
        for h in range(1, n_idx_heads):
            acc = acc + iw[:, h:h + 1] * jnp.maximum(rel[h * tq:(h + 1) * tq], 0.0)
        s_scr[c] = jnp.where(col0 + c * ck < limit, acc, -jnp.inf)

    def score_main(c, _):
        score_chunk(c, ik3_main(c))
        return 0

    lax.fori_loop(0, n_main, score_main, 0)
    score_chunk(n_main, ik3_tail())

    def count(pred_fn):
        def body(c, acc):
            return acc + _fold_lanes(jnp.where(pred_fn(s_scr[c]), 1.0, 0.0), jnp.add)
        acc = lax.fori_loop(0, n_chunks, body, jnp.zeros((tq, LANES), F32))
        return jnp.sum(acc, axis=1, keepdims=True)

    kf = jnp.float32(n_top)
    cnt = count(lambda s: s >= 0.0)
    key = jnp.where(cnt >= kf, jnp.int32(0), jnp.int32(INT_MIN))

    def bit_step(b, key):
        cand = key | jnp.left_shift(jnp.int32(1), 30 - b)
        cand_f = _key_to_f32(cand)
        cnt = count(lambda s: s >= cand_f)
        return jnp.where(cnt >= kf, cand, key)

    key = lax.fori_loop(0, 31, bit_step, key)
    thr = _key_to_f32(key)
    need = kf - count(lambda s: s > thr)
    take_all = limit <= n_top

    def bias_chunk(c, carry):
        s = s_scr[c]
        eq = s == thr
        pre = jnp.dot(jnp.where(eq, 1.0, 0.0).astype(BF16), tri_ref[...], preferred_element_type=F32)
        tied = jnp.where((carry + pre) <= need, 0.0, NEG)
        bias = jnp.where(eq, tied, jnp.where(s > thr, 0.0, NEG))
        s_scr[c] = jnp.where(take_all, jnp.where(s > -jnp.inf, 0.0, NEG), bias)
        return carry + pre[:, ck - 1:ck]

    lax.fori_loop(0, n_chunks, bias_chunk, jnp.zeros((tq, 1), F32))


def _tri_incl(n):
    r = jnp.arange(n)
    return (r[:, None] <= r[None, :]).astype(BF16)


def _fold_rows(x, op):
    return _tree([x[r * 8:(r + 1) * 8] for r in range(x.shape[0] // 8)], op)


def _topk_bias_t(s_scr, n_pairs, limit, n_top, ltri_ref):
    ck, nl = s_scr.shape[1], s_scr.shape[2]

    def count(pred_fn):
        def body(pr, acc):
            a = _fold_rows(jnp.where(pred_fn(s_scr[2 * pr]), 1.0, 0.0), jnp.add)
            b = _fold_rows(jnp.where(pred_fn(s_scr[2 * pr + 1]), 1.0, 0.0), jnp.add)
            return acc + (a + b)
        acc = lax.fori_loop(0, n_pairs, body, jnp.zeros((8, nl), F32))
        return jnp.sum(acc, axis=0, keepdims=True)

    kf = jnp.float32(n_top)
    cnt = count(lambda s: s >= 0.0)
    state = (jnp.where(cnt >= kf, jnp.int32(0), jnp.int32(INT_MIN)), jnp.where(cnt >= kf, cnt, jnp.float32(2.0 ** 30)))

    def bit_step(b, state):
        key, n_ge = state
        cand = key | jnp.left_shift(jnp.int32(1), 30 - b)
        cand_f = _key_to_f32(cand)
        cnt = count(lambda s: s >= cand_f)
        return jnp.where(cnt >= kf, cand, key), jnp.where(cnt >= kf, cnt, n_ge)

    key, n_ge = lax.fori_loop(0, 31, bit_step, state)
    thr = _key_to_f32(key)
    take_all = limit <= n_top
    all_mask = lambda s: jnp.where(s > -jnp.inf, 0.0, NEG)

    def every_tie_taken():
        def chunk(c, _):
            s = s_scr[c]
            s_scr[c] = jnp.where(take_all, all_mask(s), jnp.where(s >= thr, 0.0, NEG))
            return 0
        lax.fori_loop(0, 2 * n_pairs, chunk, 0)

    def ties_by_index():
        need = kf - count(lambda s: s > thr)

        def chunk(c, carry):
            s = s_scr[c]
            eq = s == thr
            pre = jnp.dot(ltri_ref[...], jnp.where(eq, 1.0, 0.0).astype(BF16), preferred_element_type=F32)
            tied = jnp.where((carry + pre) <= need, 0.0, NEG)
            bias = jnp.where(eq, tied, jnp.where(s > thr, 0.0, NEG))
            s_scr[c] = jnp.where(take_all, all_mask(s), bias)
            return carry + pre[ck - 1:ck, :]
        lax.fori_loop(0, 2 * n_pairs, chunk, jnp.zeros((1, nl), F32))

    surplus = jnp.max(jnp.where(take_all, 0.0, n_ge - kf))
    lax.cond(surplus > 0.0, ties_by_index, every_tie_taken)


def _dsa_prompt_t_kernel(qt_ref, iq3t_ref, iwt_ref, ik3_ref, k_ref, vt_ref, ltri_ref, o_ref,
                         s_scr, acc_scr, lg_a, lg_b,
                         *, tq, ck, n_top, n_heads, n_idx_heads):
    i = pl.program_id(1)
    pos = i * tq + lax.broadcasted_iota(jnp.int32, (1, tq), 1)
    limit = (pos // CHUNK + 1) * CHUNK
    n_chunks = ((i + 1) * tq + ck - 1) // ck
    row0 = lax.broadcasted_iota(jnp.int32, (ck, tq), 0)
    iwt = iwt_ref[...]

    def rows(c):
        return pl.ds(pl.multiple_of(c * ck, ck), ck)

    def score_chunk(c, _):
        ikc = ik3_ref[rows(c), :]
        acc = None
        for g in range(n_idx_heads // 2):
            rel = jnp.dot(ikc, iq3t_ref[:, 2 * g * tq:2 * (g + 1) * tq], preferred_element_type=F32)
            for u in range(2):
                h = 2 * g + u
                term = iwt[h:h + 1, :] * jnp.maximum(rel[:, u * tq:(u + 1) * tq], 0.0)
                acc = term if acc is None else acc + term
        s_scr[c] = jnp.where(row0 + c * ck < limit, acc, -jnp.inf)
        return 0

    lax.fori_loop(0, n_chunks, score_chunk, 0)

    n_pairs = (n_chunks + 1) // 2
    last = 2 * n_pairs - 1

    @pl.when(n_chunks % 2 == 1)
    def _():
        s_scr[n_chunks] = jnp.full((ck, tq), -jnp.inf, F32)

    _topk_bias_t(s_scr, n_pairs, limit, n_top, ltri_ref)

    heads = [slice(h * HEAD_DIM, (h + 1) * HEAD_DIM) for h in range(n_heads)]
    acc_scr[...] = jnp.zeros_like(acc_scr)

    def logits(c, buf):
        for h, hs in enumerate(heads):
            buf[h] = jnp.dot(k_ref[rows(c), hs], qt_ref[hs, :], preferred_element_type=F32)

    def attend(c, buf, ms, ls):
        bias = s_scr[c]
        new_m, new_l, pend = [], [], []
        for h, hs in enumerate(heads):
            lg = buf[h] + bias
            m_new = jnp.maximum(ms[h], jnp.max(_fold_rows(lg, jnp.maximum), axis=0, keepdims=True))
            alpha = jnp.exp(ms[h] - m_new)
            p = jnp.exp(lg - m_new)
            new_l.append(alpha * ls[h] + _fold_rows(p, jnp.add))
            new_m.append(m_new)
            pend.append((alpha, p.astype(BF16)))
        for hs, (alpha, p) in zip(heads, pend):
            acc_scr[hs, :] = alpha * acc_scr[hs, :] + jnp.dot(vt_ref[c, hs, :], p, preferred_element_type=F32)
        return tuple(new_m), tuple(new_l)

    def att_pair(pr, state):
        c0 = 2 * pr
        logits(c0 + 1, lg_b)
        state = attend(c0, lg_a, *state)
        logits(jnp.minimum(c0 + 2, last), lg_a)
        return attend(c0 + 1, lg_b, *state)

    logits(0, lg_a)
    init = (tuple(jnp.full((1, tq), NEG, F32) for _ in heads), tuple(jnp.zeros((8, tq), F32) for _ in heads))
    _, ls = lax.fori_loop(0, n_pairs, att_pair, init)
    for h, hs in enumerate(heads):
        o_t = acc_scr[hs, :] / jnp.sum(ls[h], axis=0, keepdims=True)
        o_ref[:, hs] = o_t.T.astype(o_ref.dtype)


def _dsa_prompt_t(qt, iq3t, iwt, ik3, ka, vt, b, tq, ck):
    wa, m = qt.shape
    s = m // b
    nq = s // tq
    n_heads = wa // HEAD_DIM
    n_idx_heads = iwt.shape[1]
    idx_dim = ik3.shape[1] // 3
    n_top = min(TOPK_MAX, s // 4)
    assert s % (2 * ck) == 0
    ik3 = ik3.reshape(b, s, 3 * idx_dim)
    ka = ka.reshape(b, s, wa)
    vt = vt.reshape(b, s // ck, wa, ck)
    r = jnp.arange(ck)
    ltri = (r[None, :] <= r[:, None]).astype(BF16)

    return pl.pallas_call(
        functools.partial(_dsa_prompt_t_kernel, tq=tq, ck=ck, n_top=n_top, n_heads=n_heads,
                          n_idx_heads=n_idx_heads),
        grid=(b, nq),
        in_specs=[
            pl.BlockSpec((wa, tq), lambda bb, i: (0, bb * nq + i)),
            pl.BlockSpec((None, 3 * idx_dim, n_idx_heads * tq), lambda bb, i: (bb * nq + i, 0, 0)),
            pl.BlockSpec((None, n_idx_heads, tq), lambda bb, i: (bb * nq + i, 0, 0)),
            pl.BlockSpec((None, s, 3 * idx_dim), lambda bb, i: (bb, 0, 0)),
            pl.BlockSpec((None, s, wa), lambda bb, i: (bb, 0, 0)),
            pl.BlockSpec((None, s // ck, wa, ck), lambda bb, i: (bb, 0, 0, 0)),
            pl.BlockSpec((ck, ck), lambda bb, i: (0, 0)),
        ],
        out_specs=pl.BlockSpec((None, tq, wa), lambda bb, i: (bb, i, 0)),
        out_shape=jax.ShapeDtypeStruct((b, s, wa), BF16),
        scratch_shapes=[pltpu.VMEM((s // ck, ck, tq), F32), pltpu.VMEM((wa, tq), F32),
                        pltpu.VMEM((n_heads, ck, tq), F32), pltpu.VMEM((n_heads, ck, tq), F32)],
        compiler_params=_params("parallel", "arbitrary"),
        name="dsa_prompt",
    )(qt, iq3t, iwt, ik3, ka, vt, ltri)


def _sb_tiles(qs, kvs, m2, carries, vis):
    zs = [lax.dot_general(q_h, kt, NT_DIMS, preferred_element_type=F32) for q_h, (kt, _) in zip(qs, kvs)]
    mids = []
    for z in zs:
        sp = jnp.maximum(z, 0.0) + jnp.log(1.0 + jnp.exp(-jnp.abs(z)))
        lk = -sp if vis is None else jnp.where(vis, -sp, 0.0)
        hi, lo = _split_bf16(lk)
        after = jnp.dot(jnp.concatenate([hi, lo], axis=1), m2, preferred_element_type=F32)
        mids.append((z - sp, lk, after))
    outs = []
    for (lsig, lk, after), (_, vt), carry in zip(mids, kvs, carries):
        a = jnp.exp(lsig + after + carry)
        if vis is not None:
            a = jnp.where(vis, a, 0.0)
        outs.append((after[:, 0:1] + lk[:, 0:1], jnp.dot(a.astype(BF16), vt, preferred_element_type=F32)))
    return outs


def _sb_core(q_ref, diag_kv, past_kv, m2_ref, o_ref, acc_scr, n_past, n_heads, tk):
    tq = q_ref.shape[0]
    vis = lax.broadcasted_iota(jnp.int32, (tq, tk), 1) < lax.broadcasted_iota(jnp.int32, (tq, tk), 0)
    m2 = m2_ref[...]
    heads = [slice(h * HEAD_DIM, (h + 1) * HEAD_DIM) for h in range(n_heads)]

    qs = [q_ref[:, hs] for hs in heads]
    zero = jnp.zeros((tq, 1), F32)
    carries = []
    for hs, (dc, contrib) in zip(heads, _sb_tiles(qs, [diag_kv(h) for h in range(n_heads)], m2,
                                                  [zero] * n_heads, vis)):
        acc_scr[:, hs] = contrib
        carries.append(dc)

    def alive(cs):
        m = cs[0]
        for c in cs[1:]:
            m = jnp.maximum(m, c)
        return jnp.max(m)

    def cond(state):
        step, top, _ = state
        return jnp.logical_and(step < n_past, top > SB_DEAD)

    def body(state):
        step, _, cs = state
        j = n_past - 1 - step
        new = []
        tiles = _sb_tiles([q_ref[:, hs] for hs in heads], [past_kv(j, h) for h in range(n_heads)], m2, cs, None)
        for hs, c, (dc, contrib) in zip(heads, cs, tiles):
            acc_scr[:, hs] += contrib
            new.append(c + dc)
        return step + 1, alive(new), tuple(new)

    lax.while_loop(cond, body, (jnp.int32(0), alive(carries), tuple(carries)))
    o_ref[...] = acc_scr[...].astype(o_ref.dtype)


def _tri_after(n):
    r = jnp.arange(n)
    m = (r[:, None] > r[None, :]).astype(BF16)
    return jnp.concatenate([m, m], axis=0)


def _sb_prompt_kernel(q_ref, k_ref, v_ref, m2_ref, o_ref, acc_scr, *, n_heads, tk):
    i = pl.program_id(1)

    def tile(j, h):
        rows = pl.ds(pl.multiple_of(j * tk, tk), tk)
        cols = slice(h * HEAD_DIM, (h + 1) * HEAD_DIM)
        return k_ref[rows, cols], v_ref[rows, cols]

    _sb_core(q_ref, lambda h: tile(i, h), tile, m2_ref, o_ref, acc_scr, i, n_heads, tk)


def _sb_prompt(qb, kb, vb, n_heads):
    b, s, wb = qb.shape
    tq = tk = 128
    qblk = lambda bb, i: (bb, i, 0)
    full = lambda bb, i: (bb, 0, 0)
    return pl.pallas_call(
        functools.partial(_sb_prompt_kernel, n_heads=n_heads, tk=tk),
        grid=(b, s // tq),
        in_specs=[
            pl.BlockSpec((None, tq, wb), qblk),
            pl.BlockSpec((None, s, wb), full),
            pl.BlockSpec((None, s, wb), full),
            pl.BlockSpec((2 * tk, tk), lambda bb, i: (0, 0)),
        ],
        out_specs=pl.BlockSpec((None, tq, wb), qblk),
        out_shape=jax.ShapeDtypeStruct((b, s, wb), BF16),
        scratch_shapes=[pltpu.VMEM((tq, wb), F32)],
        compiler_params=_params("parallel", "arbitrary"),
        name="sb_prompt",
    )(qb, kb, vb, _tri_after(tk))


def _pad_rows(new_ref, buf):
    t = new_ref.shape[0]
    buf[0:t, :] = new_ref[...].astype(buf.dtype)
    buf[t:, :] = jnp.zeros((buf.shape[0] - t, buf.shape[1]), buf.dtype)


def _ik3(x):
    hi, lo = _split_bf16(x)
    return jnp.concatenate([hi, hi, lo], axis=1)


def _dsa_sample_kernel(q_ref, iq_ref, ikw_ref, kn_ref, vn_ref, cik_ref, tri_ref, ck_hbm, cv_hbm,
                       o_ref, knew, vnew, iknew, s_scr, iq3_scr, kbuf, vbuf, sem,
                       *, past, t, ck, n_top, n_heads, n_idx_heads, idx_dim):
    b = pl.program_id(0)

    def head_copies(h):
        return (pltpu.make_async_copy(ck_hbm.at[b, :, h, :], kbuf.at[h], sem.at[0, h]),
                pltpu.make_async_copy(cv_hbm.at[b, :, h, :], vbuf.at[h], sem.at[1, h]))

    for h in range(n_heads):
        for cp in head_copies(h):
            cp.start()
    tn = knew.shape[0]
    _pad_rows(kn_ref, knew)
    _pad_rows(vn_ref, vnew)
    iknew[0:t, :] = ikw_ref[:, 0:idx_dim]
    iknew[t:, :] = jnp.zeros((ck - t, idx_dim), F32)
    limit = jnp.full((t, 1), past + t, jnp.int32)
    iw = ikw_ref[:, idx_dim:idx_dim + n_idx_heads]
    n_main = past // ck

    def ik3_main(c):
        return _ik3(cik_ref[pl.ds(pl.multiple_of(c * ck, ck), ck), :])

    _dsa_mask(iq_ref[...], iw, ik3_main, lambda: _ik3(iknew[...]), tri_ref, s_scr, iq3_scr,
              limit, n_main, ck, n_top, n_idx_heads, idx_dim)

    bias_past = jnp.concatenate([s_scr[c] for c in range(n_main)], axis=1)
    bias_new = s_scr[n_main][:, 0:tn]
    for h in range(n_heads):
        hs = slice(h * HEAD_DIM, (h + 1) * HEAD_DIM)
        q_h = q_ref[:, hs]
        for cp in head_copies(h):
            cp.wait()
        lg_p = lax.dot_general(q_h, kbuf[h].astype(BF16), NT_DIMS, preferred_element_type=F32) + bias_past
        lg_n = lax.dot_general(q_h, knew[:, hs], NT_DIMS, preferred_element_type=F32) + bias_new
        m = jnp.maximum(jnp.max(_fold_lanes(lg_p, jnp.maximum), axis=1, keepdims=True),
                        jnp.max(lg_n, axis=1, keepdims=True))
        p_p = jnp.exp(lg_p - m)
        p_n = jnp.exp(lg_n - m)
        l = jnp.sum(_fold_lanes(p_p, jnp.add), axis=1, keepdims=True) + jnp.sum(p_n, axis=1, keepdims=True)
        o = jnp.dot(p_p.astype(BF16), vbuf[h].astype(BF16), preferred_element_type=F32)
        o = o + jnp.dot(p_n.astype(BF16), vnew[:, hs], preferred_element_type=F32)
        o_ref[:, hs] = (o / l).astype(o_ref.dtype)


def _dsa_sample(qa, iq, ikw, ka, va, cache_k, cache_v, cache_ik, n_idx_heads):
    b, t, wa = qa.shape
    past, n_heads = cache_k.shape[1], cache_k.shape[2]
    idx_dim = cache_ik.shape[2]
    assert (past // CHUNK + 1) * CHUNK >= past + t, "new frames must sit in one open chunk"
    ck = DSA_CHUNK
    tn = LANES
    assert past % ck == 0 and t <= tn <= ck
    n_top = min(TOPK_MAX, (past + t) // 4)
    row = lambda bb: (bb, 0, 0)
    hbm = pl.BlockSpec(memory_space=pl.ANY)
    return pl.pallas_call(
        functools.partial(_dsa_sample_kernel, past=past, t=t, ck=ck, n_top=n_top, n_heads=n_heads,
                          n_idx_heads=n_idx_heads, idx_dim=idx_dim),
        grid=(b,),
        in_specs=[
            pl.BlockSpec((None, t, wa), row),
            pl.BlockSpec((None, t, iq.shape[2]), row),
            pl.BlockSpec((None, t, LANES), row),
            pl.BlockSpec((None, t, wa), row),
            pl.BlockSpec((None, t, wa), row),
            pl.BlockSpec((None, past, idx_dim), row),
            pl.BlockSpec((ck, ck), lambda bb: (0, 0)),
            hbm, hbm,
        ],
        out_specs=pl.BlockSpec((None, t, wa), row),
        out_shape=jax.ShapeDtypeStruct((b, t, wa), BF16),
        scratch_shapes=[
            pltpu.VMEM((tn, wa), BF16), pltpu.VMEM((tn, wa), BF16), pltpu.VMEM((ck, idx_dim), F32),
            pltpu.VMEM((past // ck + 1, t, ck), F32), pltpu.VMEM((n_idx_heads * t, 3 * idx_dim), BF16),
            pltpu.VMEM((n_heads, past, HEAD_DIM), F32), pltpu.VMEM((n_heads, past, HEAD_DIM), F32),
            pltpu.SemaphoreType.DMA((2, n_heads)),
        ],
        compiler_params=_params("arbitrary"),
        name="dsa_sample",
    )(qa, iq, ikw, ka, va, cache_ik, _tri_incl(ck), cache_k, cache_v)


def _sb_sample_kernel(q_ref, kn_ref, vn_ref, ck_ref, cv_ref, m2_ref, o_ref, knew, vnew, acc_scr,
                      *, past, t, tk, n_heads):
    for new_ref, buf in ((kn_ref, knew), (vn_ref, vnew)):
        buf[0:t, :] = new_ref[...]
        buf[t:, :] = jnp.zeros((tk - t, buf.shape[1]), BF16)

    def new_kv(h):
        cols = slice(h * HEAD_DIM, (h + 1) * HEAD_DIM)
        return knew[:, cols], vnew[:, cols]

    def cache_kv(j, h):
        rows = pl.ds(pl.multiple_of(j * tk, tk), tk)
        return ck_ref[rows, h, :].astype(BF16), cv_ref[rows, h, :].astype(BF16)

    _sb_core(q_ref, new_kv, cache_kv, m2_ref, o_ref, acc_scr, past // tk, n_heads, tk)


def _sb_sample(qb, kb, vb, cache_k, cache_v):
    b, t, wb = qb.shape
    past, n_heads = cache_k.shape[1], cache_k.shape[2]
    tk = 128
    assert past % tk == 0 and t <= tk
    row = lambda bb: (bb, 0, 0)
    cache = lambda bb: (bb, 0, 0, 0)
    return pl.pallas_call(
        functools.partial(_sb_sample_kernel, past=past, t=t, tk=tk, n_heads=n_heads),
        grid=(b,),
        in_specs=[
            pl.BlockSpec((None, t, wb), row),
            pl.BlockSpec((None, t, wb), row),
            pl.BlockSpec((None, t, wb), row),
            pl.BlockSpec((None, past, n_heads, HEAD_DIM), cache),
            pl.BlockSpec((None, past, n_heads, HEAD_DIM), cache),
            pl.BlockSpec((2 * tk, tk), lambda bb: (0, 0)),
        ],
        out_specs=pl.BlockSpec((None, t, wb), row),
        out_shape=jax.ShapeDtypeStruct((b, t, wb), BF16),
        scratch_shapes=[pltpu.VMEM((tk, wb), BF16), pltpu.VMEM((tk, wb), BF16), pltpu.VMEM((t, wb), F32)],
        compiler_params=_params("parallel"),
        name="sb_sample",
    )(qb, kb, vb, cache_k, cache_v, _tri_after(tk))


def _out_proj_kernel(y_ref, oa_ref, ob_ref, wa_ref, wb_ref, o_ref):
    acc = jnp.dot(oa_ref[...], wa_ref[...], preferred_element_type=F32)
    acc = acc + jnp.dot(ob_ref[...], wb_ref[...], preferred_element_type=F32)
    o_ref[...] = y_ref[...] + acc


def _out_proj(y, oa, ob, w_a, w_b):
    m, d = y.shape
    tm = min(512, m)
    row = lambda i: (i, 0)
    const = lambda i: (0, 0)
    return pl.pallas_call(
        _out_proj_kernel,
        grid=(m // tm,),
        in_specs=[
            pl.BlockSpec((tm, d), row),
            pl.BlockSpec((tm, oa.shape[1]), row),
            pl.BlockSpec((tm, ob.shape[1]), row),
            pl.BlockSpec(w_a.shape, const),
            pl.BlockSpec(w_b.shape, const),
        ],
        out_specs=pl.BlockSpec((tm, d), row),
        out_shape=jax.ShapeDtypeStruct((m, d), F32),
        compiler_params=_params("parallel"),
        name="out_proj",
    )(y, oa, ob, w_a, w_b)


def _gelu(x):
    return 0.5 * x * (1.0 + jnp.tanh(math.sqrt(2.0 / math.pi) * (x + 0.044715 * (x * x * x))))


def _c_v_kernel(x_ref, g_ref, w_ref, vg_ref, *outs):
    h = _rms(x_ref[...], g_ref[...]).astype(BF16)
    v = _gelu(jnp.dot(h, w_ref[...], preferred_element_type=F32))
    vn = _rms(v, vg_ref[...])
    outs[0][...] = vn.astype(BF16)
    if len(outs) > 1:
        outs[1][...] = vn


def _c_v(x, g, w_v, v_gain, want_f32):
    m, d = x.shape
    cw = w_v.shape[1]
    tm = min(512, m)
    row = lambda i: (i, 0)
    const = lambda i: (0, 0)
    out_specs = [pl.BlockSpec((tm, cw), row)]
    out_shape = [jax.ShapeDtypeStruct((m, cw), BF16)]
    if want_f32:
        out_specs.append(pl.BlockSpec((tm, cw), row))
        out_shape.append(jax.ShapeDtypeStruct((m, cw), F32))
    return pl.pallas_call(
        _c_v_kernel,
        grid=(m // tm,),
        in_specs=[
            pl.BlockSpec((tm, d), row),
            pl.BlockSpec((1, d), const),
            pl.BlockSpec((d, cw), const),
            pl.BlockSpec((1, cw), const),
        ],
        out_specs=out_specs,
        out_shape=out_shape,
        compiler_params=_params("parallel"),
        name="c_v",
    )(x, g.reshape(1, d), w_v, v_gain.reshape(1, cw))


def _c_mix_kernel(x_ref, g_ref, wu_ref, vn_ref, wm_ref, bias_ref, wo_ref, o_ref, h_scr, acc_scr, p_scr,
                  *, nj, gs):
    j = pl.program_id(1)

    @pl.when(j == 0)
    def _():
        h_scr[...] = _rms(x_ref[...], g_ref[...]).astype(BF16)
        acc_scr[...] = jnp.zeros_like(acc_scr)

    u = _gelu(jnp.dot(h_scr[...], wu_ref[...], preferred_element_type=F32))
    tm = u.shape[0]
    for gg in range(gs):
        cs = slice(gg * LANES, (gg + 1) * LANES)
        wm = wm_ref[j * gs + gg]
        for r in range(tm // C_CHUNK):
            rs = slice(r * C_CHUNK, (r + 1) * C_CHUNK)
            mix = jnp.dot(wm, vn_ref[rs, cs], preferred_element_type=F32) + bias_ref[:, cs]
            p_scr[rs, cs] = (u[rs, cs] * mix).astype(BF16)
    acc_scr[...] += jnp.dot(p_scr[...], wo_ref[...], preferred_element_type=F32)

    @pl.when(j == nj - 1)
    def _():
        o_ref[...] = x_ref[...] + acc_scr[...]


def _c_mix(x, g, w_u, vn, w_m, bias, w_o, gs=4):
    m, d = x.shape
    cw = w_u.shape[1]
    ng = cw // LANES
    gs = min(gs, ng)
    nj = ng // gs
    tm = min(512, m)
    row = lambda i, j: (i, 0)
    return pl.pallas_call(
        functools.partial(_c_mix_kernel, nj=nj, gs=gs),
        grid=(m // tm, nj),
        in_specs=[
            pl.BlockSpec((tm, d), row),
            pl.BlockSpec((1, d), lambda i, j: (0, 0)),
            pl.BlockSpec((d, gs * LANES), lambda i, j: (0, j)),
            pl.BlockSpec((tm, gs * LANES), lambda i, j: (i, j)),
            pl.BlockSpec(w_m.shape, lambda i, j: (0, 0, 0)),
            pl.BlockSpec((C_CHUNK, gs * LANES), lambda i, j: (0, j)),
            pl.BlockSpec((gs * LANES, d), lambda i, j: (j, 0)),
        ],
        out_specs=pl.BlockSpec((tm, d), row),
        out_shape=jax.ShapeDtypeStruct((m, d), F32),
        scratch_shapes=[pltpu.VMEM((tm, d), BF16), pltpu.VMEM((tm, d), F32),
                        pltpu.VMEM((tm, gs * LANES), BF16)],
        compiler_params=_params("parallel", "arbitrary"),
        name="c_mix",
    )(x, g.reshape(1, d), w_u, vn, w_m, bias, w_o)


def _rope_tables(pos, reps):
    pos = pos.astype(F32)[:, None]

    def tab(half, copies):
        inv = ROPE_THETA ** (-jnp.arange(half, dtype=F32) / half)
        ang = pos * inv[None, :]
        c, s = jnp.cos(ang), jnp.sin(ang)
        return jnp.tile(jnp.concatenate([c, c], axis=1), (reps, copies)), \
            jnp.tile(jnp.concatenate([-s, s], axis=1), (reps, copies))

    c128, s128 = tab(HEAD_DIM // 2, 1)
    c64, s64 = tab(32, 2)
    return c128, s128, c64, s64


def kernel(x_prompt, x_sample, cache_a_k, cache_a_v, cache_a_ik, cache_b_k, cache_b_v, norm_ff1, ff1_w1, ff1_w3, ff1_w2, norm_mix, norm_ff2, ff2_w1, ff2_w3, ff2_w2, ab_w_in, ab_w_out, c_w_in, c_v_norm, c_w_s, c_b_s, c_w_out, final_norm):
    bp, seq, d = x_prompt.shape
    bs, t, _ = x_sample.shape
    past, ha = cache_a_k.shape[2], cache_a_k.shape[3]
    hb = cache_b_k.shape[3]
    idx_dim = cache_a_ik.shape[3]
    wa, wb = ha * HEAD_DIM, hb * HEAD_DIM
    n_idx_heads = (ab_w_in.shape[2] - 3 * wa - 3 * wb - idx_dim) // (idx_dim + 1)
    wi = n_idx_heads * idx_dim
    assert idx_dim == 64 and wi % LANES == 0 and wi <= wa and n_idx_heads <= LANES - idx_dim
    depth = norm_ff1.shape[0]
    mp, ms = bp * seq, bs * t

    yp = x_prompt.reshape(mp, d)
    ys = x_sample.reshape(ms, d)
    bf = lambda w: w.astype(BF16)

    tabs_p = _rope_tables(jnp.arange(seq), 1)
    tabs_s = _rope_tables(past + jnp.arange(t), min(512, ms) // t)

    outs_p, outs_s, s_cv = [], [], []
    for layer in range(depth):
        j = layer // 2
        ys, w1, w3, w2 = _half_ffn(ys, norm_ff1[layer], ff1_w1[layer], ff1_w3[layer], ff1_w2[layer],
                                   emit_bf16=True)
        yp = _half_ffn(yp, norm_ff1[layer], w1, w3, w2)
        if layer % 2 == 0:
            w_in = ab_w_in[j]
            o = 0
            cols = []
            for width in (wa, wa, wa, wi, idx_dim, n_idx_heads, wb, wb, wb):
                cols.append(w_in[:, o:o + width])
                o += width
            w_qa, w_ka, w_va, w_iq, w_ik, w_iw, w_qb, w_kb, w_vb = cols
            pad = lambda w, width: jnp.pad(w, ((0, 0), (0, width - w.shape[1])))
            w_pa = bf(jnp.concatenate([w_qa, w_ka, w_va, pad(w_iq, wa)], axis=1))
            w_kw = bf(pad(jnp.concatenate([w_ik, w_iw], axis=1), LANES))
            w_pb = bf(jnp.concatenate([w_qb, w_kb, w_vb], axis=1))
            w_oa, w_ob = bf(ab_w_out[j][:wa]), bf(ab_w_out[j][wa:])

            tq = DSA_QUERIES
            qt, kaf, kab, vaf, vt, iq3t, ikw, ik3, iwt = _proj_a_t(
                yp, norm_mix[layer], w_pa, w_kw, tabs_p, seq, ha, n_idx_heads, idx_dim, tq, DSA_CHUNK)
            qb, kbf, kbb, vbf, vbb = _proj_b(yp, norm_mix[layer], w_pb)
            r3 = lambda a: a.reshape(bp, seq, a.shape[-1])
            ik = ikw[:, :idx_dim]
            o_a = _dsa_prompt_t(qt, iq3t, iwt, ik3, kab, vt, bp, tq, DSA_CHUNK)
            o_b = _sb_prompt(r3(qb), r3(kbb), r3(vbb), hb)
            yp = _out_proj(yp, o_a.reshape(mp, wa), o_b.reshape(mp, wb), w_oa, w_ob)
            outs_p.append((kaf.reshape(bp, seq, ha, HEAD_DIM), vaf.reshape(bp, seq, ha, HEAD_DIM),
                           ik.reshape(bp, seq, idx_dim),
                           kbf.reshape(bp, seq, hb, HEAD_DIM), vbf.reshape(bp, seq, hb, HEAD_DIM)))

            qa, kaf, kab, vaf, vab, iq, ikw = _proj_a(
                ys, norm_mix[layer], w_pa, w_kw, tabs_s, min(512, ms), ha, n_idx_heads, idx_dim)
            qb, kbf, kbb, vbf, vbb = _proj_b(ys, norm_mix[layer], w_pb)
            r3 = lambda a: a.reshape(bs, t, a.shape[-1])
            o_a = _dsa_sample(r3(qa), r3(iq), r3(ikw), r3(kab), r3(vab),
                              cache_a_k[j], cache_a_v[j], cache_a_ik[j], n_idx_heads)
            o_b = _sb_sample(r3(qb), r3(kbb), r3(vbb), cache_b_k[j], cache_b_v[j])
            ys = _out_proj(ys, o_a.reshape(ms, wa), o_b.reshape(ms, wb), w_oa, w_ob)
            outs_s.append((kaf.reshape(bs, t, ha, HEAD_DIM), vaf.reshape(bs, t, ha, HEAD_DIM),
                           ikw[:, :idx_dim].reshape(bs, t, idx_dim),
                           kbf.reshape(bs, t, hb, HEAD_DIM), vbf.reshape(bs, t, hb, HEAD_DIM)))
        else:
            cw = c_w_in.shape[2] // 2
            ng = c_w_s.shape[1]
            w_u, w_v = bf(c_w_in[j][:, :cw]), bf(c_w_in[j][:, cw:])
            w_o = bf(c_w_out[j])
            i = jnp.arange(C_CHUNK)
            mask = (i[None, :] // CHUNK) <= (i[:, None] // CHUNK)
            w_m = jnp.where(mask[None], c_w_s[j], 0.0)
            bias_p = jnp.repeat(c_b_s[j].T, cw // ng, axis=1)
            per = C_CHUNK // t
            w_ms = jnp.einsum('ab,gij->gaibj', jnp.eye(per, dtype=F32), w_m[:, :t, :t]).reshape(ng, C_CHUNK, C_CHUNK)
            bias_s = jnp.tile(bias_p[:t], (per, 1))

            vn = _c_v(yp, norm_mix[layer], w_v, c_v_norm[j], False)[0]
            yp = _c_mix(yp, norm_mix[layer], w_u, vn, bf(w_m), bias_p, w_o)
            vn, vn_f32 = _c_v(ys, norm_mix[layer], w_v, c_v_norm[j], True)
            ys = _c_mix(ys, norm_mix[layer], w_u, vn, bf(w_ms), bias_s, w_o)
            s_cv.append(vn_f32.reshape(bs, t, cw))
        last = layer == depth - 1
        fg = final_norm if last else None
        ys, w1, w3, w2 = _half_ffn(ys, norm_ff2[layer], ff2_w1[layer], ff2_w3[layer], ff2_w2[layer], fg,
                                   emit_bf16=True)
        yp = _half_ffn(yp, norm_ff2[layer], w1, w3, w2, fg)

    stack = lambda outs, k: jnp.stack([o[k] for o in outs])
    return (yp.reshape(bp, seq, d), ys.reshape(bs, t, d),
            stack(outs_p, 0), stack(outs_p, 1), stack(outs_p, 2), stack(outs_p, 3), stack(outs_p, 4),
            stack(outs_s, 0), stack(outs_s, 1), stack(outs_s, 2), stack(outs_s, 3), stack(outs_s, 4),
            jnp.stack(s_cv))
```

```python
import functools
import math

import jax
import jax.numpy as jnp
from jax import lax
from jax.experimental import pallas as pl
from jax.experimental.pallas import tpu as pltpu

F32 = jnp.float32
BF16 = jnp.bfloat16

RMS_EPS = 1e-6
CHUNK = 64
TOPK_MAX = 256
ROPE_THETA = 10000.0
C_CHUNK = 128
LANES = 128
HEAD_DIM = 128
NEG = -1e30
INT_MIN = -2147483648
SB_DEAD = -105.0
DSA_CHUNK = 256
DSA_QUERIES = 256
IDX_SPLIT = 3
FFN_SPLIT = 2
VMEM_LIMIT_BYTES = 56 * 1024 * 1024

NT_DIMS = (((1,), (1,)), ((), ()))


def _params(*sem):
    return pltpu.CompilerParams(dimension_semantics=sem, vmem_limit_bytes=VMEM_LIMIT_BYTES)


def _rms(x, g):
    return x * lax.rsqrt(jnp.mean(x * x, axis=-1, keepdims=True) + RMS_EPS) * g


def _split_bf16(x):
    hi = x.astype(BF16)
    lo = (x - hi.astype(F32)).astype(BF16)
    return hi, lo


def _ffn_kernel(x_ref, g_ref, w1_ref, w3_ref, w2_ref, *rest, nf, final_norm, emit_bf16):
    rest = list(rest)
    gf_ref = rest.pop(0) if final_norm else None
    o_ref = rest.pop(0)
    wb_refs = [rest.pop(0) for _ in range(3)] if emit_bf16 else None
    h_scr, acc_scr = rest
    f = pl.program_id(1)

    @pl.when(f == 0)
    def _():
        h_scr[...] = _rms(x_ref[...], g_ref[...]).astype(BF16)
        acc_scr[...] = jnp.zeros_like(acc_scr)

    w1, w3, w2 = w1_ref[...], w3_ref[...], w2_ref[...]
    if emit_bf16:
        w1, w3, w2 = w1.astype(BF16), w3.astype(BF16), w2.astype(BF16)
        for ref, w in zip(wb_refs, (w1, w3, w2)):
            ref[...] = w
    h = h_scr[...]
    tf = w1.shape[1]
    sub = tf // FFN_SPLIT if tf % (FFN_SPLIT * LANES) == 0 else tf
    cols = [slice(c * sub, (c + 1) * sub) for c in range(tf // sub)]
    ab = [(jnp.dot(h, w1[:, cs], preferred_element_type=F32), jnp.dot(h, w3[:, cs], preferred_element_type=F32))
          for cs in cols]
    upd = None
    for (a, b), cs in zip(ab, cols):
        p = (a * jax.nn.sigmoid(a) * b).astype(BF16)
        d = jnp.dot(p, w2[cs, :], preferred_element_type=F32)
        upd = d if upd is None else upd + d
    acc_scr[...] += upd

    @pl.when(f == nf - 1)
    def _():
        y = x_ref[...] + 0.5 * acc_scr[...]
        if final_norm:
            y = _rms(y, gf_ref[...])
        o_ref[...] = y


def _ffn_tile(dff, target):
    return max(t for t in range(LANES, min(dff, target) + 1, LANES) if dff % t == 0)


def _half_ffn(x, g, w1, w3, w2, final_g=None, emit_bf16=False):
    m, d = x.shape
    dff = w1.shape[1]
    tm = min(512, m)
    tf = _ffn_tile(dff, 256 if emit_bf16 else 704)
    assert not emit_bf16 or m == tm
    nf = dff // tf
    w13_spec = pl.BlockSpec((d, tf), lambda i, f: (0, f))
    w2_spec = pl.BlockSpec((tf, d), lambda i, f: (f, 0))
    in_specs = [
        pl.BlockSpec((tm, d), lambda i, f: (i, 0)),
        pl.BlockSpec((1, d), lambda i, f: (0, 0)),
        w13_spec, w13_spec, w2_spec,
    ]
    args = [x, g.reshape(1, d), w1, w3, w2]
    if final_g is not None:
        in_specs.append(pl.BlockSpec((1, d), lambda i, f: (0, 0)))
        args.append(final_g.reshape(1, d))
    out_specs = [pl.BlockSpec((tm, d), lambda i, f: (i, 0))]
    out_shape = [jax.ShapeDtypeStruct((m, d), F32)]
    if emit_bf16:
        out_specs += [w13_spec, w13_spec, w2_spec]
        out_shape += [jax.ShapeDtypeStruct(w.shape, BF16) for w in (w1, w3, w2)]
    outs = pl.pallas_call(
        functools.partial(_ffn_kernel, nf=nf, final_norm=final_g is not None, emit_bf16=emit_bf16),
        grid=(m // tm, nf),
        in_specs=in_specs,
        out_specs=out_specs,
        out_shape=out_shape,
        scratch_shapes=[pltpu.VMEM((tm, d), BF16), pltpu.VMEM((tm, d), F32)],
        compiler_params=_params("parallel", "arbitrary"),
        name="half_ffn",
    )(*args)
    return outs if emit_bf16 else outs[0]


def _rope_heads(z, cos, sin):
    outs = []
    for h in range(z.shape[1] // LANES):
        zh = z[:, h * LANES:(h + 1) * LANES]
        outs.append(zh * cos + pltpu.roll(zh, LANES // 2, axis=1) * sin)
    return outs


def _rope_pairs(z, cos, sin):
    lane = lax.broadcasted_iota(jnp.int32, (z.shape[0], LANES), 1)
    low = (lane % 64) < 32
    outs = []
    for h in range(z.shape[1] // LANES):
        zh = z[:, h * LANES:(h + 1) * LANES]
        partner = jnp.where(low, pltpu.roll(zh, LANES - 32, axis=1), pltpu.roll(zh, 32, axis=1))
        outs.append(zh * cos + partner * sin)
    return outs


def _proj_a_kernel(x_ref, g_ref, w_ref, wkw_ref, c128_ref, s128_ref, c64_ref, s64_ref,
                   qa_ref, kaf_ref, kab_ref, vaf_ref, vab_ref, iq_ref, ikw_ref, h_scr,
                   *, n_idx_heads, idx_dim):
    j = pl.program_id(1)

    @pl.when(j == 0)
    def _():
        h_scr[...] = _rms(x_ref[...], g_ref[...]).astype(BF16)

    z = jnp.dot(h_scr[...], w_ref[...], preferred_element_type=F32)
    wa = qa_ref.shape[1]

    @pl.when(j == 0)
    def _():
        scale = HEAD_DIM ** -0.5
        for h, r in enumerate(_rope_heads(z[:, :wa], c128_ref[...], s128_ref[...])):
            qa_ref[:, h * LANES:(h + 1) * LANES] = (r * scale).astype(BF16)

    @pl.when(j == 1)
    def _():
        for h, r in enumerate(_rope_heads(z[:, :wa], c128_ref[...], s128_ref[...])):
            kaf_ref[:, h * LANES:(h + 1) * LANES] = r
            kab_ref[:, h * LANES:(h + 1) * LANES] = r.astype(BF16)

    @pl.when(j == 2)
    def _():
        kv = z[:, :wa]
        vaf_ref[...] = kv
        vab_ref[...] = kv.astype(BF16)

    @pl.when(j == 3)
    def _():
        wi = iq_ref.shape[1]
        for h, r in enumerate(_rope_pairs(z[:, :wi], c64_ref[...], s64_ref[...])):
            iq_ref[:, h * LANES:(h + 1) * LANES] = r * (idx_dim ** -0.5)
        zz = jnp.dot(h_scr[...], wkw_ref[...], preferred_element_type=F32)
        r = _rope_pairs(zz, c64_ref[...], s64_ref[...])[0]
        lane = lax.broadcasted_iota(jnp.int32, zz.shape, 1)
        ikw_ref[...] = jnp.where(lane < idx_dim, r, zz * (n_idx_heads ** -0.5))


def _proj_a(x, g, w, w_kw, tabs, n_pos_rows, ha, n_idx_heads, idx_dim):
    m, d = x.shape
    tm = min(512, m)
    wa = ha * HEAD_DIM
    wi = n_idx_heads * idx_dim
    tn = w.shape[1] // 4
    npb = n_pos_rows // tm
    row = lambda i, j: (i, 0)
    tab = lambda i, j: (i % npb, 0)
    tab_spec = pl.BlockSpec((tm, LANES), tab)
    return pl.pallas_call(
        functools.partial(_proj_a_kernel, n_idx_heads=n_idx_heads, idx_dim=idx_dim),
        grid=(m // tm, 4),
        in_specs=[
            pl.BlockSpec((tm, d), row),
            pl.BlockSpec((1, d), lambda i, j: (0, 0)),
            pl.BlockSpec((d, tn), lambda i, j: (0, j)),
            pl.BlockSpec((d, LANES), lambda i, j: (0, 0)),
            tab_spec, tab_spec, tab_spec, tab_spec,
        ],
        out_specs=[
            pl.BlockSpec((tm, wa), row), pl.BlockSpec((tm, wa), row), pl.BlockSpec((tm, wa), row),
            pl.BlockSpec((tm, wa), row), pl.BlockSpec((tm, wa), row),
            pl.BlockSpec((tm, wi), row), pl.BlockSpec((tm, LANES), row),
        ],
        out_shape=[
            jax.ShapeDtypeStruct((m, wa), BF16), jax.ShapeDtypeStruct((m, wa), F32),
            jax.ShapeDtypeStruct((m, wa), BF16), jax.ShapeDtypeStruct((m, wa), F32),
            jax.ShapeDtypeStruct((m, wa), BF16),
            jax.ShapeDtypeStruct((m, wi), F32), jax.ShapeDtypeStruct((m, LANES), F32),
        ],
        scratch_shapes=[pltpu.VMEM((tm, d), BF16)],
        compiler_params=_params("parallel", "arbitrary"),
        name="proj_a",
    )(x, g.reshape(1, d), w, w_kw, *tabs)


def _proj_a_t_kernel(x_ref, g_ref, w_ref, wkw_ref, c128_ref, s128_ref, c64_ref, s64_ref,
                     qt_ref, kaf_ref, kab_ref, vaf_ref, vt_ref, iq3t_ref, ikw_ref, ik3_ref, iwt_ref,
                     h_scr, zk, zv, sem, *, n_idx_heads, idx_dim, tq, ck):
    j = pl.program_id(1)

    @pl.when(jnp.logical_and(j == 3, pl.program_id(0) == pl.num_programs(0) - 1))
    def _():
        _heads_drain([(zk, kaf_ref, sem.at[0]), (zv, vaf_ref, sem.at[1])])

    @pl.when(j == 0)
    def _():
        h_scr[...] = _rms(x_ref[...], g_ref[...]).astype(BF16)

    z = jnp.dot(h_scr[...], w_ref[...], preferred_element_type=F32)
    tm = z.shape[0]
    wa = kab_ref.shape[1]

    @pl.when(j == 0)
    def _():
        scale = HEAD_DIM ** -0.5
        for h, r in enumerate(_rope_heads(z[:, :wa], c128_ref[...], s128_ref[...])):
            qt_ref[h * LANES:(h + 1) * LANES, :] = (r * scale).T.astype(BF16)

    @pl.when(j == 1)
    def _():
        rot = _rope_heads(z[:, :wa], c128_ref[...], s128_ref[...])

        def fill(buf):
            for h, r in enumerate(rot):
                buf[:, h * LANES:(h + 1) * LANES] = r

        _heads_writeback(zk, kaf_ref, sem.at[0], fill)
        for h, r in enumerate(rot):
            kab_ref[:, h * LANES:(h + 1) * LANES] = r.astype(BF16)

    @pl.when(j == 2)
    def _():
        def fill(buf):
            buf[...] = z[:, :wa]

        _heads_writeback(zv, vaf_ref, sem.at[1], fill)
        for h in range(wa // LANES):
            for c in range(tm // ck):
                vt_ref[c, h * LANES:(h + 1) * LANES, :] = \
                    z[c * ck:(c + 1) * ck, h * LANES:(h + 1) * LANES].T.astype(BF16)

    @pl.when(j == 3)
    def _():
        wi = n_idx_heads * idx_dim
        for g, r in enumerate(_rope_pairs(z[:, :wi], c64_ref[...], s64_ref[...])):
            r = r * (idx_dim ** -0.5)
            hi = r.astype(BF16).astype(F32)
            hi_t, lo_t = hi.T.astype(BF16), (r - hi).T.astype(BF16)
            for u in range(2):
                rows = slice(u * idx_dim, (u + 1) * idx_dim)
                for qb in range(tm // tq):
                    cols = slice(qb * tq, (qb + 1) * tq)
                    dst = slice((2 * g + u) * tq, (2 * g + u + 1) * tq)
                    iq3t_ref[qb, 0:idx_dim, dst] = hi_t[rows, cols]
                    iq3t_ref[qb, idx_dim:2 * idx_dim, dst] = lo_t[rows, cols]
                    iq3t_ref[qb, 2 * idx_dim:3 * idx_dim, dst] = hi_t[rows, cols]
        zz = jnp.dot(h_scr[...], wkw_ref[...], preferred_element_type=F32)
        r = _rope_pairs(zz, c64_ref[...], s64_ref[...])[0]
        lane = lax.broadcasted_iota(jnp.int32, zz.shape, 1)
        val = jnp.where(lane < idx_dim, r, zz * (n_idx_heads ** -0.5))
        ikw_ref[...] = val
        ik = val[:, 0:idx_dim]
        ik_hi = ik.astype(BF16).astype(F32)
        ik3_ref[...] = jnp.concatenate([ik_hi, ik_hi, ik - ik_hi], axis=1).astype(BF16)
        val_t = val.T
        for qb in range(tm // tq):
            iwt_ref[qb] = val_t[idx_dim:idx_dim + n_idx_heads, qb * tq:(qb + 1) * tq]


def _proj_a_t(x, g, w, w_kw, tabs, n_pos_rows, ha, n_idx_heads, idx_dim, tq, ck):
    m, d = x.shape
    tm = min(512, m)
    wa = ha * HEAD_DIM
    assert n_idx_heads * idx_dim <= wa and n_idx_heads % 2 == 0 and tm % ck == 0 and tm % tq == 0
    tn = w.shape[1] // 4
    npb = n_pos_rows // tm
    row = lambda i, j: (i, 0)
    lead = lambda i, j: (i, 0, 0)
    tab_spec = pl.BlockSpec((tm, LANES), lambda i, j: (i % npb, 0))
    return pl.pallas_call(
        functools.partial(_proj_a_t_kernel, n_idx_heads=n_idx_heads, idx_dim=idx_dim, tq=tq, ck=ck),
        grid=(m // tm, 4),
        in_specs=[
            pl.BlockSpec((tm, d), row),
            pl.BlockSpec((1, d), lambda i, j: (0, 0)),
            pl.BlockSpec((d, tn), lambda i, j: (0, j)),
            pl.BlockSpec((d, LANES), lambda i, j: (0, 0)),
            tab_spec, tab_spec, tab_spec, tab_spec,
        ],
        out_specs=[
            pl.BlockSpec((wa, tm), lambda i, j: (0, i)),
            pl.BlockSpec(memory_space=pl.ANY), pl.BlockSpec((tm, wa), row), pl.BlockSpec(memory_space=pl.ANY),
            pl.BlockSpec((tm // ck, wa, ck), lead),
            pl.BlockSpec((tm // tq, IDX_SPLIT * idx_dim, n_idx_heads * tq), lead),
            pl.BlockSpec((tm, LANES), row), pl.BlockSpec((tm, IDX_SPLIT * idx_dim), row),
            pl.BlockSpec((tm // tq, n_idx_heads, tq), lead),
        ],
        out_shape=[
            jax.ShapeDtypeStruct((wa, m), BF16),
            jax.ShapeDtypeStruct((m, ha, HEAD_DIM), F32), jax.ShapeDtypeStruct((m, wa), BF16),
            jax.ShapeDtypeStruct((m, ha, HEAD_DIM), F32),
            jax.ShapeDtypeStruct((m // ck, wa, ck), BF16),
            jax.ShapeDtypeStruct((m // tq, IDX_SPLIT * idx_dim, n_idx_heads * tq), BF16),
            jax.ShapeDtypeStruct((m, LANES), F32), jax.ShapeDtypeStruct((m, IDX_SPLIT * idx_dim), BF16),
            jax.ShapeDtypeStruct((m // tq, n_idx_heads, tq), F32),
        ],
        scratch_shapes=[pltpu.VMEM((tm, d), BF16), pltpu.VMEM((tm, wa), F32), pltpu.VMEM((tm, wa), F32),
                        pltpu.SemaphoreType.DMA((2, ha))],
        compiler_params=_params("arbitrary", "arbitrary"),
        name="proj_a_t",
    )(x, g.reshape(1, d), w, w_kw, *tabs)


def _head_copies(z_scr, out_hbm, sem, blk):
    tm = z_scr.shape[0]
    return [pltpu.make_async_copy(z_scr.at[:, pl.ds(h * HEAD_DIM, HEAD_DIM)],
                                  out_hbm.at[pl.ds(blk * tm, tm), h, :], sem.at[h])
            for h in range(out_hbm.shape[1])]


def _heads_writeback(z_scr, out_hbm, sem, fill):
    i = pl.program_id(0)

    @pl.when(i > 0)
    def _():
        for cp in _head_copies(z_scr, out_hbm, sem, i - 1):
            cp.wait()

    fill(z_scr)
    for cp in _head_copies(z_scr, out_hbm, sem, i):
        cp.start()


def _heads_drain(pairs):
    i = pl.program_id(0)
    for z_scr, out_hbm, sem in pairs:
        for cp in _head_copies(z_scr, out_hbm, sem, i):
            cp.wait()


def _proj_b_kernel(x_ref, g_ref, w_ref, qb_ref, kbf_ref, kbb_ref, vbf_ref, vbb_ref, h_scr, *dma,
                   heads_layout):
    j = pl.program_id(1)

    @pl.when(j == 0)
    def _():
        h_scr[...] = _rms(x_ref[...], g_ref[...]).astype(BF16)

    z = jnp.dot(h_scr[...], w_ref[...], preferred_element_type=F32)

    def fill(buf):
        buf[...] = z

    @pl.when(j == 0)
    def _():
        qb_ref[...] = (z * (HEAD_DIM ** -0.5)).astype(BF16)

    @pl.when(j == 1)
    def _():
        if heads_layout:
            _heads_writeback(dma[0], kbf_ref, dma[2].at[0], fill)
        else:
            kbf_ref[...] = z
        kbb_ref[...] = z.astype(BF16)

    @pl.when(j == 2)
    def _():
        if heads_layout:
            _heads_writeback(dma[1], vbf_ref, dma[2].at[1], fill)
        else:
            vbf_ref[...] = z
        vbb_ref[...] = z.astype(BF16)

    if heads_layout:
        @pl.when(jnp.logical_and(j == 2, pl.program_id(0) == pl.num_programs(0) - 1))
        def _():
            _heads_drain([(dma[0], kbf_ref, dma[2].at[0]), (dma[1], vbf_ref, dma[2].at[1])])


def _proj_b(x, g, w, heads_layout=False):
    m, d = x.shape
    tm = min(512, m)
    wb = w.shape[1] // 3
    hb = wb // HEAD_DIM
    row = lambda i, j: (i, 0)
    blk = pl.BlockSpec((tm, wb), row)
    if heads_layout:
        f32_spec, f32_shape = pl.BlockSpec(memory_space=pl.ANY), jax.ShapeDtypeStruct((m, hb, HEAD_DIM), F32)
        dma = [pltpu.VMEM((tm, wb), F32), pltpu.VMEM((tm, wb), F32), pltpu.SemaphoreType.DMA((2, hb))]
    else:
        f32_spec, f32_shape = blk, jax.ShapeDtypeStruct((m, wb), F32)
        dma = []
    return pl.pallas_call(
        functools.partial(_proj_b_kernel, heads_layout=heads_layout),
        grid=(m // tm, 3),
        in_specs=[
            pl.BlockSpec((tm, d), row),
            pl.BlockSpec((1, d), lambda i, j: (0, 0)),
            pl.BlockSpec((d, wb), lambda i, j: (0, j)),
        ],
        out_specs=[blk, f32_spec, blk, f32_spec, blk],
        out_shape=[jax.ShapeDtypeStruct((m, wb), BF16), f32_shape, jax.ShapeDtypeStruct((m, wb), BF16),
                   f32_shape, jax.ShapeDtypeStruct((m, wb), BF16)],
        scratch_shapes=[pltpu.VMEM((tm, d), BF16)] + dma,
        compiler_params=_params("arbitrary", "arbitrary"),
        name="proj_b",
    )(x, g.reshape(1, d), w)


def _key_to_f32(key):
    bits = jnp.where(key >= 0, key, key ^ jnp.int32(0x7FFFFFFF))
    return lax.bitcast_convert_type(bits, F32)


def _tree(parts, op):
    while len(parts) > 1:
        parts = [op(parts[k], parts[k + 1]) if k + 1 < len(parts) else parts[k] for k in range(0, len(parts), 2)]
    return parts[0]


def _fold_lanes(x, op):
    return _tree([x[:, t * LANES:(t + 1) * LANES] for t in range(x.shape[1] // LANES)], op)


def _dsa_mask(iq, iw, ik3_main, ik3_tail, tri_ref, s_scr, iq3_scr, limit, n_main, ck, n_top,
              n_idx_heads, idx_dim):
    tq = iq.shape[0]
    n_chunks = n_main + 1

    for h in range(n_idx_heads):
        x = iq[:, h * idx_dim:(h + 1) * idx_dim]
        hi = x.astype(BF16).astype(F32)
        iq3_scr[h * tq:(h + 1) * tq, :] = jnp.concatenate([hi, x - hi, hi], axis=1).astype(BF16)

    col0 = lax.broadcasted_iota(jnp.int32, (tq, ck), 1)

    def score_chunk(c, ikc):
        rel = lax.dot_general(iq3_scr[...], ikc, NT_DIMS, preferred_element_type=F32)
        acc = iw[:, 0:1] * jnp.maximum(rel[0:tq], 0.0)
        for h in range(1, n_idx_heads):
            acc = acc + iw[:, h:h + 1] * jnp.maximum(rel[h * tq:(h + 1) * tq], 0.0)
        s_scr[c] = jnp.where(col0 + c * ck < limit, acc, -jnp.inf)

    def score_main(c, _):
        score_chunk(c, ik3_main(c))
        return 0

    lax.fori_loop(0, n_main, score_main, 0)
    score_chunk(n_main, ik3_tail())

    def count(pred_fn):
        def body(c, acc):
            return acc + _fold_lanes(jnp.where(pred_fn(s_scr[c]), 1.0, 0.0), jnp.add)
        acc = lax.fori_loop(0, n_chunks, body, jnp.zeros((tq, LANES), F32))
        return jnp.sum(acc, axis=1, keepdims=True)

    kf = jnp.float32(n_top)
    cnt = count(lambda s: s >= 0.0)
    key = jnp.where(cnt >= kf, jnp.int32(0), jnp.int32(INT_MIN))

    def bit_step(b, key):
        cand = key | jnp.left_shift(jnp.int32(1), 30 - b)
        cand_f = _key_to_f32(cand)
        cnt = count(lambda s: s >= cand_f)
        return jnp.where(cnt >= kf, cand, key)

    key = lax.fori_loop(0, 31, bit_step, key)
    thr = _key_to_f32(key)
    need = kf - count(lambda s: s > thr)
    take_all = limit <= n_top

    def bias_chunk(c, carry):
        s = s_scr[c]
        eq = s == thr
        pre = jnp.dot(jnp.where(eq, 1.0, 0.0).astype(BF16), tri_ref[...], preferred_element_type=F32)
        tied = jnp.where((carry + pre) <= need, 0.0, NEG)
        bias = jnp.where(eq, tied, jnp.where(s > thr, 0.0, NEG))
        s_scr[c] = jnp.where(take_all, jnp.where(s > -jnp.inf, 0.0, NEG), bias)
        return carry + pre[:, ck - 1:ck]

    lax.fori_loop(0, n_chunks, bias_chunk, jnp.zeros((tq, 1), F32))


def _tri_incl(n):
    r = jnp.arange(n)
    return (r[:, None] <= r[None, :]).astype(BF16)


def _fold_rows(x, op):
    return _tree([x[r * 8:(r + 1) * 8] for r in range(x.shape[0] // 8)], op)


def _topk_bias_t(s_scr, n_pairs, limit, n_top, ltri_ref):
    ck, nl = s_scr.shape[1], s_scr.shape[2]

    def count(pred_fn):
        def body(pr, acc):
            a = _fold_rows(jnp.where(pred_fn(s_scr[2 * pr]), 1.0, 0.0), jnp.add)
            b = _fold_rows(jnp.where(pred_fn(s_scr[2 * pr + 1]), 1.0, 0.0), jnp.add)
            return acc + (a + b)
        acc = lax.fori_loop(0, n_pairs, body, jnp.zeros((8, nl), F32))
        return jnp.sum(acc, axis=0, keepdims=True)

    kf = jnp.float32(n_top)
    cnt = count(lambda s: s >= 0.0)
    state = (jnp.where(cnt >= kf, jnp.int32(0), jnp.int32(INT_MIN)), jnp.where(cnt >= kf, cnt, jnp.float32(2.0 ** 30)))

    def bit_step(b, state):
        key, n_ge = state
        cand = key | jnp.left_shift(jnp.int32(1), 30 - b)
        cand_f = _key_to_f32(cand)
        cnt = count(lambda s: s >= cand_f)
        return jnp.where(cnt >= kf, cand, key), jnp.where(cnt >= kf, cnt, n_ge)

    key, n_ge = lax.fori_loop(0, 31, bit_step, state)
    thr = _key_to_f32(key)
    take_all = limit <= n_top
    all_mask = lambda s: jnp.where(s > -jnp.inf, 0.0, NEG)

    def every_tie_taken():
        def chunk(c, _):
            s = s_scr[c]
            s_scr[c] = jnp.where(take_all, all_mask(s), jnp.where(s >= thr, 0.0, NEG))
            return 0
        lax.fori_loop(0, 2 * n_pairs, chunk, 0)

    def ties_by_index():
        need = kf - count(lambda s: s > thr)

        def chunk(c, carry):
            s = s_scr[c]
            eq = s == thr
            pre = jnp.dot(ltri_ref[...], jnp.where(eq, 1.0, 0.0).astype(BF16), preferred_element_type=F32)
            tied = jnp.where((carry + pre) <= need, 0.0, NEG)
            bias = jnp.where(eq, tied, jnp.where(s > thr, 0.0, NEG))
            s_scr[c] = jnp.where(take_all, all_mask(s), bias)
            return carry + pre[ck - 1:ck, :]
        lax.fori_loop(0, 2 * n_pairs, chunk, jnp.zeros((1, nl), F32))

    surplus = jnp.max(jnp.where(take_all, 0.0, n_ge - kf))
    lax.cond(surplus > 0.0, ties_by_index, every_tie_taken)


def _dsa_prompt_t_kernel(qt_ref, iq3t_ref, iwt_ref, ik3_ref, k_ref, vt_ref, ltri_ref, o_ref,
                         s_scr, acc_scr, lg_a, lg_b,
                         *, tq, ck, n_top, n_heads, n_idx_heads):
    i = pl.program_id(1)
    pos = i * tq + lax.broadcasted_iota(jnp.int32, (1, tq), 1)
    limit = (pos // CHUNK + 1) * CHUNK
    n_chunks = ((i + 1) * tq + ck - 1) // ck
    row0 = lax.broadcasted_iota(jnp.int32, (ck, tq), 0)
    iwt = iwt_ref[...]

    def rows(c):
        return pl.ds(pl.multiple_of(c * ck, ck), ck)

    def score_chunk(c, _):
        ikc = ik3_ref[rows(c), :]
        acc = None
        for g in range(n_idx_heads // 2):
            rel = jnp.dot(ikc, iq3t_ref[:, 2 * g * tq:2 * (g + 1) * tq], preferred_element_type=F32)
            for u in range(2):
                h = 2 * g + u
                term = iwt[h:h + 1, :] * jnp.maximum(rel[:, u * tq:(u + 1) * tq], 0.0)
                acc = term if acc is None else acc + term
        s_scr[c] = jnp.where(row0 + c * ck < limit, acc, -jnp.inf)
        return 0

    lax.fori_loop(0, n_chunks, score_chunk, 0)

    n_pairs = (n_chunks + 1) // 2
    last = 2 * n_pairs - 1

    @pl.when(n_chunks % 2 == 1)
    def _():
        s_scr[n_chunks] = jnp.full((ck, tq), -jnp.inf, F32)

    _topk_bias_t(s_scr, n_pairs, limit, n_top, ltri_ref)

    heads = [slice(h * HEAD_DIM, (h + 1) * HEAD_DIM) for h in range(n_heads)]
    acc_scr[...] = jnp.zeros_like(acc_scr)

    def logits(c, buf):
        for h, hs in enumerate(heads):
            buf[h] = jnp.dot(k_ref[rows(c), hs], qt_ref[hs, :], preferred_element_type=F32)

    def attend(c, buf, ms, ls):
        bias = s_scr[c]
        new_m, new_l, pend = [], [], []
        for h, hs in enumerate(heads):
            lg = buf[h] + bias
            m_new = jnp.maximum(ms[h], jnp.max(_fold_rows(lg, jnp.maximum), axis=0, keepdims=True))
            alpha = jnp.exp(ms[h] - m_new)
            p = jnp.exp(lg - m_new)
            new_l.append(alpha * ls[h] + _fold_rows(p, jnp.add))
            new_m.append(m_new)
            pend.append((alpha, p.astype(BF16)))
        for hs, (alpha, p) in zip(heads, pend):
            acc_scr[hs, :] = alpha * acc_scr[hs, :] + jnp.dot(vt_ref[c, hs, :], p, preferred_element_type=F32)
        return tuple(new_m), tuple(new_l)

    def att_pair(pr, state):
        c0 = 2 * pr
        logits(c0 + 1, lg_b)
        state = attend(c0, lg_a, *state)
        logits(jnp.minimum(c0 + 2, last), lg_a)
        return attend(c0 + 1, lg_b, *state)

    logits(0, lg_a)
    init = (tuple(jnp.full((1, tq), NEG, F32) for _ in heads), tuple(jnp.zeros((8, tq), F32) for _ in heads))
    _, ls = lax.fori_loop(0, n_pairs, att_pair, init)
    for h, hs in enumerate(heads):
        o_t = acc_scr[hs, :] / jnp.sum(ls[h], axis=0, keepdims=True)
        o_ref[:, hs] = o_t.T.astype(o_ref.dtype)


def _dsa_prompt_t(qt, iq3t, iwt, ik3, ka, vt, b, tq, ck):
    wa, m = qt.shape
    s = m // b
    nq = s // tq
    n_heads = wa // HEAD_DIM
    n_idx_heads = iwt.shape[1]
    idx_dim = ik3.shape[1] // IDX_SPLIT
    n_top = min(TOPK_MAX, s // 4)
    assert s % (2 * ck) == 0
    ik3 = ik3.reshape(b, s, IDX_SPLIT * idx_dim)
    ka = ka.reshape(b, s, wa)
    vt = vt.reshape(b, s // ck, wa, ck)
    r = jnp.arange(ck)
    ltri = (r[None, :] <= r[:, None]).astype(BF16)

    return pl.pallas_call(
        functools.partial(_dsa_prompt_t_kernel, tq=tq, ck=ck, n_top=n_top, n_heads=n_heads,
                          n_idx_heads=n_idx_heads),
        grid=(b, nq),
        in_specs=[
            pl.BlockSpec((wa, tq), lambda bb, i: (0, bb * nq + i)),
            pl.BlockSpec((None, IDX_SPLIT * idx_dim, n_idx_heads * tq), lambda bb, i: (bb * nq + i, 0, 0)),
            pl.BlockSpec((None, n_idx_heads, tq), lambda bb, i: (bb * nq + i, 0, 0)),
            pl.BlockSpec((None, s, IDX_SPLIT * idx_dim), lambda bb, i: (bb, 0, 0)),
            pl.BlockSpec((None, s, wa), lambda bb, i: (bb, 0, 0)),
            pl.BlockSpec((None, s // ck, wa, ck), lambda bb, i: (bb, 0, 0, 0)),
            pl.BlockSpec((ck, ck), lambda bb, i: (0, 0)),
        ],
        out_specs=pl.BlockSpec((None, tq, wa), lambda bb, i: (bb, i, 0)),
        out_shape=jax.ShapeDtypeStruct((b, s, wa), BF16),
        scratch_shapes=[pltpu.VMEM((s // ck, ck, tq), F32), pltpu.VMEM((wa, tq), F32),
                        pltpu.VMEM((n_heads, ck, tq), F32), pltpu.VMEM((n_heads, ck, tq), F32)],
        compiler_params=_params("parallel", "arbitrary"),
        name="dsa_prompt",
    )(qt, iq3t, iwt, ik3, ka, vt, ltri)


def _sb_tiles(qs, kvs, m2, carries, vis):
    zs = [lax.dot_general(q_h, kt, NT_DIMS, preferred_element_type=F32) for q_h, (kt, _) in zip(qs, kvs)]
    mids = []
    for z in zs:
        sp = jnp.maximum(z, 0.0) + jnp.log(1.0 + jnp.exp(-jnp.abs(z)))
        lk = -sp if vis is None else jnp.where(vis, -sp, 0.0)
        hi, lo = _split_bf16(lk)
        after = jnp.dot(jnp.concatenate([hi, lo], axis=1), m2, preferred_element_type=F32)
        mids.append((z - sp, lk, after))
    outs = []
    for (lsig, lk, after), (_, vt), carry in zip(mids, kvs, carries):
        a = jnp.exp(lsig + after + carry)
        if vis is not None:
            a = jnp.where(vis, a, 0.0)
        outs.append((after[:, 0:1] + lk[:, 0:1], jnp.dot(a.astype(BF16), vt, preferred_element_type=F32)))
    return outs


def _sb_core(q_ref, diag_kv, past_kv, m2_ref, o_ref, acc_scr, n_past, n_heads, tk):
    tq = q_ref.shape[0]
    vis = lax.broadcasted_iota(jnp.int32, (tq, tk), 1) < lax.broadcasted_iota(jnp.int32, (tq, tk), 0)
    m2 = m2_ref[...]
    heads = [slice(h * HEAD_DIM, (h + 1) * HEAD_DIM) for h in range(n_heads)]

    qs = [q_ref[:, hs] for hs in heads]
    zero = jnp.zeros((tq, 1), F32)
    carries = []
    for hs, (dc, contrib) in zip(heads, _sb_tiles(qs, [diag_kv(h) for h in range(n_heads)], m2,
                                                  [zero] * n_heads, vis)):
        acc_scr[:, hs] = contrib
        carries.append(dc)

    def alive(cs):
        m = cs[0]
        for c in cs[1:]:
            m = jnp.maximum(m, c)
        return jnp.max(m)

    def cond(state):
        step, top, _ = state
        return jnp.logical_and(step < n_past, top > SB_DEAD)

    def body(state):
        step, _, cs = state
        j = n_past - 1 - step
        new = []
        tiles = _sb_tiles([q_ref[:, hs] for hs in heads], [past_kv(j, h) for h in range(n_heads)], m2, cs, None)
        for hs, c, (dc, contrib) in zip(heads, cs, tiles):
            acc_scr[:, hs] += contrib
            new.append(c + dc)
        return step + 1, alive(new), tuple(new)

    lax.while_loop(cond, body, (jnp.int32(0), alive(carries), tuple(carries)))
    o_ref[...] = acc_scr[...].astype(o_ref.dtype)


def _tri_after(n):
    r = jnp.arange(n)
    m = (r[:, None] > r[None, :]).astype(BF16)
    return jnp.concatenate([m, m], axis=0)


def _sb_prompt_kernel(q_ref, k_ref, v_ref, m2_ref, o_ref, acc_scr, *, n_heads, tk):
    i = pl.program_id(1)

    def tile(j, h):
        rows = pl.ds(pl.multiple_of(j * tk, tk), tk)
        cols = slice(h * HEAD_DIM, (h + 1) * HEAD_DIM)
        return k_ref[rows, cols], v_ref[rows, cols]

    _sb_core(q_ref, lambda h: tile(i, h), tile, m2_ref, o_ref, acc_scr, i, n_heads, tk)


def _sb_prompt(qb, kb, vb, n_heads):
    b, s, wb = qb.shape
    tq = tk = 128
    qblk = lambda bb, i: (bb, i, 0)
    full = lambda bb, i: (bb, 0, 0)
    return pl.pallas_call(
        functools.partial(_sb_prompt_kernel, n_heads=n_heads, tk=tk),
        grid=(b, s // tq),
        in_specs=[
            pl.BlockSpec((None, tq, wb), qblk),
            pl.BlockSpec((None, s, wb), full),
            pl.BlockSpec((None, s, wb), full),
            pl.BlockSpec((2 * tk, tk), lambda bb, i: (0, 0)),
        ],
        out_specs=pl.BlockSpec((None, tq, wb), qblk),
        out_shape=jax.ShapeDtypeStruct((b, s, wb), BF16),
        scratch_shapes=[pltpu.VMEM((tq, wb), F32)],
        compiler_params=_params("parallel", "arbitrary"),
        name="sb_prompt",
    )(qb, kb, vb, _tri_after(tk))


def _pad_rows(new_ref, buf):
    t = new_ref.shape[0]
    buf[0:t, :] = new_ref[...].astype(buf.dtype)
    buf[t:, :] = jnp.zeros((buf.shape[0] - t, buf.shape[1]), buf.dtype)


def _ik3(x):
    hi, lo = _split_bf16(x)
    return jnp.concatenate([hi, hi, lo], axis=1)


def _dsa_sample_kernel(q_ref, iq_ref, ikw_ref, kn_ref, vn_ref, cik_ref, tri_ref, ck_hbm, cv_hbm,
                       o_ref, knew, vnew, iknew, s_scr, iq3_scr, kbuf, vbuf, sem,
                       *, past, t, ck, n_top, n_heads, n_idx_heads, idx_dim):
    b = pl.program_id(0)

    def head_copies(h):
        return (pltpu.make_async_copy(ck_hbm.at[b, :, h, :], kbuf.at[h], sem.at[0, h]),
                pltpu.make_async_copy(cv_hbm.at[b, :, h, :], vbuf.at[h], sem.at[1, h]))

    for h in range(n_heads):
        for cp in head_copies(h):
            cp.start()
    tn = knew.shape[0]
    _pad_rows(kn_ref, knew)
    _pad_rows(vn_ref, vnew)
    iknew[0:t, :] = ikw_ref[:, 0:idx_dim]
    iknew[t:, :] = jnp.zeros((ck - t, idx_dim), F32)
    limit = jnp.full((t, 1), past + t, jnp.int32)
    iw = ikw_ref[:, idx_dim:idx_dim + n_idx_heads]
    n_main = past // ck

    def ik3_main(c):
        return _ik3(cik_ref[pl.ds(pl.multiple_of(c * ck, ck), ck), :])

    _dsa_mask(iq_ref[...], iw, ik3_main, lambda: _ik3(iknew[...]), tri_ref, s_scr, iq3_scr,
              limit, n_main, ck, n_top, n_idx_heads, idx_dim)

    bias_past = jnp.concatenate([s_scr[c] for c in range(n_main)], axis=1)
    bias_new = s_scr[n_main][:, 0:tn]
    for h in range(n_heads):
        hs = slice(h * HEAD_DIM, (h + 1) * HEAD_DIM)
        q_h = q_ref[:, hs]
        for cp in head_copies(h):
            cp.wait()
        lg_p = lax.dot_general(q_h, kbuf[h].astype(BF16), NT_DIMS, preferred_element_type=F32) + bias_past
        lg_n = lax.dot_general(q_h, knew[:, hs], NT_DIMS, preferred_element_type=F32) + bias_new
        m = jnp.maximum(jnp.max(_fold_lanes(lg_p, jnp.maximum), axis=1, keepdims=True),
                        jnp.max(lg_n, axis=1, keepdims=True))
        p_p = jnp.exp(lg_p - m)
        p_n = jnp.exp(lg_n - m)
        l = jnp.sum(_fold_lanes(p_p, jnp.add), axis=1, keepdims=True) + jnp.sum(p_n, axis=1, keepdims=True)
        o = jnp.dot(p_p.astype(BF16), vbuf[h].astype(BF16), preferred_element_type=F32)
        o = o + jnp.dot(p_n.astype(BF16), vnew[:, hs], preferred_element_type=F32)
        o_ref[:, hs] = (o / l).astype(o_ref.dtype)


def _dsa_sample(qa, iq, ikw, ka, va, cache_k, cache_v, cache_ik, n_idx_heads):
    b, t, wa = qa.shape
    past, n_heads = cache_k.shape[1], cache_k.shape[2]
    idx_dim = cache_ik.shape[2]
    assert (past // CHUNK + 1) * CHUNK >= past + t, "new frames must sit in one open chunk"
    ck = DSA_CHUNK
    tn = LANES
    assert past % ck == 0 and t <= tn <= ck
    n_top = min(TOPK_MAX, (past + t) // 4)
    row = lambda bb: (bb, 0, 0)
    hbm = pl.BlockSpec(memory_space=pl.ANY)
    return pl.pallas_call(
        functools.partial(_dsa_sample_kernel, past=past, t=t, ck=ck, n_top=n_top, n_heads=n_heads,
                          n_idx_heads=n_idx_heads, idx_dim=idx_dim),
        grid=(b,),
        in_specs=[
            pl.BlockSpec((None, t, wa), row),
            pl.BlockSpec((None, t, iq.shape[2]), row),
            pl.BlockSpec((None, t, LANES), row),
            pl.BlockSpec((None, t, wa), row),
            pl.BlockSpec((None, t, wa), row),
            pl.BlockSpec((None, past, idx_dim), row),
            pl.BlockSpec((ck, ck), lambda bb: (0, 0)),
            hbm, hbm,
        ],
        out_specs=pl.BlockSpec((None, t, wa), row),
        out_shape=jax.ShapeDtypeStruct((b, t, wa), BF16),
        scratch_shapes=[
            pltpu.VMEM((tn, wa), BF16), pltpu.VMEM((tn, wa), BF16), pltpu.VMEM((ck, idx_dim), F32),
            pltpu.VMEM((past // ck + 1, t, ck), F32), pltpu.VMEM((n_idx_heads * t, IDX_SPLIT * idx_dim), BF16),
            pltpu.VMEM((n_heads, past, HEAD_DIM), F32), pltpu.VMEM((n_heads, past, HEAD_DIM), F32),
            pltpu.SemaphoreType.DMA((2, n_heads)),
        ],
        compiler_params=_params("arbitrary"),
        name="dsa_sample",
    )(qa, iq, ikw, ka, va, cache_ik, _tri_incl(ck), cache_k, cache_v)


def _sb_sample_kernel(q_ref, kn_ref, vn_ref, ck_ref, cv_ref, m2_ref, o_ref, knew, vnew, acc_scr,
                      *, past, t, tk, n_heads):
    for new_ref, buf in ((kn_ref, knew), (vn_ref, vnew)):
        buf[0:t, :] = new_ref[...]
        buf[t:, :] = jnp.zeros((tk - t, buf.shape[1]), BF16)

    def new_kv(h):
        cols = slice(h * HEAD_DIM, (h + 1) * HEAD_DIM)
        return knew[:, cols], vnew[:, cols]

    def cache_kv(j, h):
        rows = pl.ds(pl.multiple_of(j * tk, tk), tk)
        return ck_ref[rows, h, :].astype(BF16), cv_ref[rows, h, :].astype(BF16)

    _sb_core(q_ref, new_kv, cache_kv, m2_ref, o_ref, acc_scr, past // tk, n_heads, tk)


def _sb_sample(qb, kb, vb, cache_k, cache_v):
    b, t, wb = qb.shape
    past, n_heads = cache_k.shape[1], cache_k.shape[2]
    tk = 128
    assert past % tk == 0 and t <= tk
    row = lambda bb: (bb, 0, 0)
    cache = lambda bb: (bb, 0, 0, 0)
    return pl.pallas_call(
        functools.partial(_sb_sample_kernel, past=past, t=t, tk=tk, n_heads=n_heads),
        grid=(b,),
        in_specs=[
            pl.BlockSpec((None, t, wb), row),
            pl.BlockSpec((None, t, wb), row),
            pl.BlockSpec((None, t, wb), row),
            pl.BlockSpec((None, past, n_heads, HEAD_DIM), cache),
            pl.BlockSpec((None, past, n_heads, HEAD_DIM), cache),
            pl.BlockSpec((2 * tk, tk), lambda bb: (0, 0)),
        ],
        out_specs=pl.BlockSpec((None, t, wb), row),
        out_shape=jax.ShapeDtypeStruct((b, t, wb), BF16),
        scratch_shapes=[pltpu.VMEM((tk, wb), BF16), pltpu.VMEM((tk, wb), BF16), pltpu.VMEM((t, wb), F32)],
        compiler_params=_params("parallel"),
        name="sb_sample",
    )(qb, kb, vb, cache_k, cache_v, _tri_after(tk))


def _out_proj_kernel(y_ref, oa_ref, ob_ref, wa_ref, wb_ref, o_ref):
    acc = jnp.dot(oa_ref[...], wa_ref[...], preferred_element_type=F32)
    acc = acc + jnp.dot(ob_ref[...], wb_ref[...], preferred_element_type=F32)
    o_ref[...] = y_ref[...] + acc


def _out_proj(y, oa, ob, w_a, w_b):
    m, d = y.shape
    tm = min(512, m)
    row = lambda i: (i, 0)
    const = lambda i: (0, 0)
    return pl.pallas_call(
        _out_proj_kernel,
        grid=(m // tm,),
        in_specs=[
            pl.BlockSpec((tm, d), row),
            pl.BlockSpec((tm, oa.shape[1]), row),
            pl.BlockSpec((tm, ob.shape[1]), row),
            pl.BlockSpec(w_a.shape, const),
            pl.BlockSpec(w_b.shape, const),
        ],
        out_specs=pl.BlockSpec((tm, d), row),
        out_shape=jax.ShapeDtypeStruct((m, d), F32),
        compiler_params=_params("parallel"),
        name="out_proj",
    )(y, oa, ob, w_a, w_b)


def _gelu(x):
    return 0.5 * x * (1.0 + jnp.tanh(math.sqrt(2.0 / math.pi) * (x + 0.044715 * (x * x * x))))


def _c_v_kernel(x_ref, g_ref, w_ref, vg_ref, *outs):
    h = _rms(x_ref[...], g_ref[...]).astype(BF16)
    v = _gelu(jnp.dot(h, w_ref[...], preferred_element_type=F32))
    vn = _rms(v, vg_ref[...])
    outs[0][...] = vn.astype(BF16)
    if len(outs) > 1:
        outs[1][...] = vn


def _c_v(x, g, w_v, v_gain, want_f32):
    m, d = x.shape
    cw = w_v.shape[1]
    tm = min(512, m)
    row = lambda i: (i, 0)
    const = lambda i: (0, 0)
    out_specs = [pl.BlockSpec((tm, cw), row)]
    out_shape = [jax.ShapeDtypeStruct((m, cw), BF16)]
    if want_f32:
        out_specs.append(pl.BlockSpec((tm, cw), row))
        out_shape.append(jax.ShapeDtypeStruct((m, cw), F32))
    return pl.pallas_call(
        _c_v_kernel,
        grid=(m // tm,),
        in_specs=[
            pl.BlockSpec((tm, d), row),
            pl.BlockSpec((1, d), const),
            pl.BlockSpec((d, cw), const),
            pl.BlockSpec((1, cw), const),
        ],
        out_specs=out_specs,
        out_shape=out_shape,
        compiler_params=_params("parallel"),
        name="c_v",
    )(x, g.reshape(1, d), w_v, v_gain.reshape(1, cw))


def _c_mix_kernel(x_ref, g_ref, wu_ref, vn_ref, wm_ref, bias_ref, wo_ref, o_ref, h_scr, acc_scr, p_scr,
                  *, nj, gs):
    j = pl.program_id(1)

    @pl.when(j == 0)
    def _():
        h_scr[...] = _rms(x_ref[...], g_ref[...]).astype(BF16)
        acc_scr[...] = jnp.zeros_like(acc_scr)

    tm = h_scr.shape[0]
    tiles = [(slice(r * C_CHUNK, (r + 1) * C_CHUNK), slice(gg * LANES, (gg + 1) * LANES), gg)
             for gg in range(gs) for r in range(tm // C_CHUNK)]
    mixes = [jnp.dot(wm_ref[j * gs + gg], vn_ref[rs, cs], preferred_element_type=F32) for rs, cs, gg in tiles]
    u = _gelu(jnp.dot(h_scr[...], wu_ref[...], preferred_element_type=F32))
    for (rs, cs, _), mix in zip(tiles, mixes):
        p_scr[rs, cs] = (u[rs, cs] * (mix + bias_ref[:, cs])).astype(BF16)
    acc_scr[...] += jnp.dot(p_scr[...], wo_ref[...], preferred_element_type=F32)

    @pl.when(j == nj - 1)
    def _():
        o_ref[...] = x_ref[...] + acc_scr[...]


def _c_mix(x, g, w_u, vn, w_m, bias, w_o, gs=4):
    m, d = x.shape
    cw = w_u.shape[1]
    ng = cw // LANES
    gs = min(gs, ng)
    nj = ng // gs
    tm = min(512, m)
    row = lambda i, j: (i, 0)
    return pl.pallas_call(
        functools.partial(_c_mix_kernel, nj=nj, gs=gs),
        grid=(m // tm, nj),
        in_specs=[
            pl.BlockSpec((tm, d), row),
            pl.BlockSpec((1, d), lambda i, j: (0, 0)),
            pl.BlockSpec((d, gs * LANES), lambda i, j: (0, j)),
            pl.BlockSpec((tm, gs * LANES), lambda i, j: (i, j)),
            pl.BlockSpec(w_m.shape, lambda i, j: (0, 0, 0)),
            pl.BlockSpec((C_CHUNK, gs * LANES), lambda i, j: (0, j)),
            pl.BlockSpec((gs * LANES, d), lambda i, j: (j, 0)),
        ],
        out_specs=pl.BlockSpec((tm, d), row),
        out_shape=jax.ShapeDtypeStruct((m, d), F32),
        scratch_shapes=[pltpu.VMEM((tm, d), BF16), pltpu.VMEM((tm, d), F32),
                        pltpu.VMEM((tm, gs * LANES), BF16)],
        compiler_params=_params("parallel", "arbitrary"),
        name="c_mix",
    )(x, g.reshape(1, d), w_u, vn, w_m, bias, w_o)


def _rope_tables(pos, reps):
    pos = pos.astype(F32)[:, None]

    def tab(half, copies):
        inv = ROPE_THETA ** (-jnp.arange(half, dtype=F32) / half)
        ang = pos * inv[None, :]
        c, s = jnp.cos(ang), jnp.sin(ang)
        return jnp.tile(jnp.concatenate([c, c], axis=1), (reps, copies)), \
            jnp.tile(jnp.concatenate([-s, s], axis=1), (reps, copies))

    c128, s128 = tab(HEAD_DIM // 2, 1)
    c64, s64 = tab(32, 2)
    return c128, s128, c64, s64


def kernel(x_prompt, x_sample, cache_a_k, cache_a_v, cache_a_ik, cache_b_k, cache_b_v, norm_ff1, ff1_w1, ff1_w3, ff1_w2, norm_mix, norm_ff2, ff2_w1, ff2_w3, ff2_w2, ab_w_in, ab_w_out, c_w_in, c_v_norm, c_w_s, c_b_s, c_w_out, final_norm):
    bp, seq, d = x_prompt.shape
    bs, t, _ = x_sample.shape
    past, ha = cache_a_k.shape[2], cache_a_k.shape[3]
    hb = cache_b_k.shape[3]
    idx_dim = cache_a_ik.shape[3]
    wa, wb = ha * HEAD_DIM, hb * HEAD_DIM
    n_idx_heads = (ab_w_in.shape[2] - 3 * wa - 3 * wb - idx_dim) // (idx_dim + 1)
    wi = n_idx_heads * idx_dim
    assert idx_dim == 64 and wi % LANES == 0 and wi <= wa and n_idx_heads <= LANES - idx_dim
    depth = norm_ff1.shape[0]
    mp, ms = bp * seq, bs * t

    yp = x_prompt.reshape(mp, d)
    ys = x_sample.reshape(ms, d)
    bf = lambda w: w.astype(BF16)

    tabs_p = _rope_tables(jnp.arange(seq), 1)
    tabs_s = _rope_tables(past + jnp.arange(t), min(512, ms) // t)

    outs_p, outs_s, s_cv = [], [], []
    for layer in range(depth):
        j = layer // 2
        ys, w1, w3, w2 = _half_ffn(ys, norm_ff1[layer], ff1_w1[layer], ff1_w3[layer], ff1_w2[layer],
                                   emit_bf16=True)
        yp = _half_ffn(yp, norm_ff1[layer], w1, w3, w2)
        if layer % 2 == 0:
            w_in = ab_w_in[j]
            o = 0
            cols = []
            for width in (wa, wa, wa, wi, idx_dim, n_idx_heads, wb, wb, wb):
                cols.append(w_in[:, o:o + width])
                o += width
            w_qa, w_ka, w_va, w_iq, w_ik, w_iw, w_qb, w_kb, w_vb = cols
            pad = lambda w, width: jnp.pad(w, ((0, 0), (0, width - w.shape[1])))
            w_pa = bf(jnp.concatenate([w_qa, w_ka, w_va, pad(w_iq, wa)], axis=1))
            w_kw = bf(pad(jnp.concatenate([w_ik, w_iw], axis=1), LANES))
            w_pb = bf(jnp.concatenate([w_qb, w_kb, w_vb], axis=1))
            w_oa, w_ob = bf(ab_w_out[j][:wa]), bf(ab_w_out[j][wa:])

            tq = DSA_QUERIES
            qt, kaf, kab, vaf, vt, iq3t, ikw, ik3, iwt = _proj_a_t(
                yp, norm_mix[layer], w_pa, w_kw, tabs_p, seq, ha, n_idx_heads, idx_dim, tq, DSA_CHUNK)
            qb, kbf, kbb, vbf, vbb = _proj_b(yp, norm_mix[layer], w_pb, heads_layout=True)
            r3 = lambda a: a.reshape(bp, seq, a.shape[-1])
            ik = ikw[:, :idx_dim]
            o_a = _dsa_prompt_t(qt, iq3t, iwt, ik3, kab, vt, bp, tq, DSA_CHUNK)
            o_b = _sb_prompt(r3(qb), r3(kbb), r3(vbb), hb)
            yp = _out_proj(yp, o_a.reshape(mp, wa), o_b.reshape(mp, wb), w_oa, w_ob)
            outs_p.append((kaf.reshape(bp, seq, ha, HEAD_DIM), vaf.reshape(bp, seq, ha, HEAD_DIM),
                           ik.reshape(bp, seq, idx_dim),
                           kbf.reshape(bp, seq, hb, HEAD_DIM), vbf.reshape(bp, seq, hb, HEAD_DIM)))

            qa, kaf, kab, vaf, vab, iq, ikw = _proj_a(
                ys, norm_mix[layer], w_pa, w_kw, tabs_s, min(512, ms), ha, n_idx_heads, idx_dim)
            qb, kbf, kbb, vbf, vbb = _proj_b(ys, norm_mix[layer], w_pb)
            r3 = lambda a: a.reshape(bs, t, a.shape[-1])
            o_a = _dsa_sample(r3(qa), r3(iq), r3(ikw), r3(kab), r3(vab),
                              cache_a_k[j], cache_a_v[j], cache_a_ik[j], n_idx_heads)
            o_b = _sb_sample(r3(qb), r3(kbb), r3(vbb), cache_b_k[j], cache_b_v[j])
            ys = _out_proj(ys, o_a.reshape(ms, wa), o_b.reshape(ms, wb), w_oa, w_ob)
            outs_s.append((kaf.reshape(bs, t, ha, HEAD_DIM), vaf.reshape(bs, t, ha, HEAD_DIM),
                           ikw[:, :idx_dim].reshape(bs, t, idx_dim),
                           kbf.reshape(bs, t, hb, HEAD_DIM), vbf.reshape(bs, t, hb, HEAD_DIM)))
        else:
            cw = c_w_in.shape[2] // 2
            ng = c_w_s.shape[1]
            w_u, w_v = bf(c_w_in[j][:, :cw]), bf(c_w_in[j][:, cw:])
            w_o = bf(c_w_out[j])
            i = jnp.arange(C_CHUNK)
            mask = (i[None, :] // CHUNK) <= (i[:, None] // CHUNK)
            w_m = jnp.where(mask[None], c_w_s[j], 0.0)
            bias_p = jnp.repeat(c_b_s[j].T, cw // ng, axis=1)
            per = C_CHUNK // t
            w_ms = jnp.einsum('ab,gij->gaibj', jnp.eye(per, dtype=F32), w_m[:, :t, :t]).reshape(ng, C_CHUNK, C_CHUNK)
            bias_s = jnp.tile(bias_p[:t], (per, 1))

            vn = _c_v(yp, norm_mix[layer], w_v, c_v_norm[j], False)[0]
            yp = _c_mix(yp, norm_mix[layer], w_u, vn, bf(w_m), bias_p, w_o)
            vn, vn_f32 = _c_v(ys, norm_mix[layer], w_v, c_v_norm[j], True)
            ys = _c_mix(ys, norm_mix[layer], w_u, vn, bf(w_ms), bias_s, w_o)
            s_cv.append(vn_f32.reshape(bs, t, cw))
        last = layer == depth - 1
        fg = final_norm if last else None
        ys, w1, w3, w2 = _half_ffn(ys, norm_ff2[layer], ff2_w1[layer], ff2_w3[layer], ff2_w2[layer], fg,
                                   emit_bf16=True)
        yp = _half_ffn(yp, norm_ff2[layer], w1, w3, w2, fg)

    stack = lambda outs, k: jnp.stack([o[k] for o in outs])
    return (yp.reshape(bp, seq, d), ys.reshape(bs, t, d),
            stack(outs_p, 0), stack(outs_p, 1), stack(outs_p, 2), stack(outs_p, 3), stack(outs_p, 4),
            stack(outs_s, 0), stack(outs_s, 1), stack(outs_s, 2), stack(outs_s, 3), stack(outs_s, 4),
            jnp.stack(s_cv))
```

```python
import functools
import math

import jax
import jax.numpy as jnp
from jax import lax
from jax.experimental import pallas as pl
from jax.experimental.pallas import tpu as pltpu

F32 = jnp.float32
BF16 = jnp.bfloat16

RMS_EPS = 1e-6
CHUNK = 64
TOPK_MAX = 256
ROPE_THETA = 10000.0
C_CHUNK = 128
LANES = 128
HEAD_DIM = 128
NEG = -1e30
INT_MIN = -2147483648
SB_DEAD = -105.0
DSA_CHUNK = 256
DSA_QUERIES = 256
IDX_SPLIT = 3
FFN_SPLIT = 2
VMEM_LIMIT_BYTES = 56 * 1024 * 1024

NT_DIMS = (((1,), (1,)), ((), ()))


def _params(*sem):
    return pltpu.CompilerParams(dimension_semantics=sem, vmem_limit_bytes=VMEM_LIMIT_BYTES)


def _rms(x, g):
    return x * lax.rsqrt(jnp.mean(x * x, axis=-1, keepdims=True) + RMS_EPS) * g


def _split_bf16(x):
    hi = x.astype(BF16)
    lo = (x - hi.astype(F32)).astype(BF16)
    return hi, lo


def _ffn_kernel(x_ref, g_ref, w1_ref, w3_ref, w2_ref, *rest, nf, final_norm, emit_bf16):
    rest = list(rest)
    gf_ref = rest.pop(0) if final_norm else None
    o_ref = rest.pop(0)
    wb_refs = [rest.pop(0) for _ in range(3)] if emit_bf16 else None
    h_scr, acc_scr = rest
    f = pl.program_id(1)

    @pl.when(f == 0)
    def _():
        h_scr[...] = _rms(x_ref[...], g_ref[...]).astype(BF16)
        acc_scr[...] = jnp.zeros_like(acc_scr)

    w1, w3, w2 = w1_ref[...], w3_ref[...], w2_ref[...]
    if emit_bf16:
        w1, w3, w2 = w1.astype(BF16), w3.astype(BF16), w2.astype(BF16)
        for ref, w in zip(wb_refs, (w1, w3, w2)):
            ref[...] = w
    h = h_scr[...]
    tf = w1.shape[1]
    sub = tf // FFN_SPLIT if tf % (FFN_SPLIT * LANES) == 0 else tf
    cols = [slice(c * sub, (c + 1) * sub) for c in range(tf // sub)]
    ab = [(jnp.dot(h, w1[:, cs], preferred_element_type=F32), jnp.dot(h, w3[:, cs], preferred_element_type=F32))
          for cs in cols]
    upd = None
    for (a, b), cs in zip(ab, cols):
        p = (a * jax.nn.sigmoid(a) * b).astype(BF16)
        d = jnp.dot(p, w2[cs, :], preferred_element_type=F32)
        upd = d if upd is None else upd + d
    acc_scr[...] += upd

    @pl.when(f == nf - 1)
    def _():
        y = x_ref[...] + 0.5 * acc_scr[...]
        if final_norm:
            y = _rms(y, gf_ref[...])
        o_ref[...] = y


def _ffn_tile(dff, target):
    return max(t for t in range(LANES, min(dff, target) + 1, LANES) if dff % t == 0)


def _half_ffn(x, g, w1, w3, w2, final_g=None, emit_bf16=False, layer=None):
    m, d = x.shape
    dff = w1.shape[-1]
    tm = min(512, m)
    tf = _ffn_tile(dff, 256 if emit_bf16 else 704)
    assert not emit_bf16 or m == tm
    nf = dff // tf
    w13_spec = pl.BlockSpec((d, tf), lambda i, f: (0, f))
    w2_spec = pl.BlockSpec((tf, d), lambda i, f: (f, 0))
    if layer is None:
        w_specs = [w13_spec, w13_spec, w2_spec]
    else:
        w13_in = pl.BlockSpec((None, d, tf), lambda i, f: (layer, 0, f))
        w_specs = [w13_in, w13_in, pl.BlockSpec((None, tf, d), lambda i, f: (layer, f, 0))]
    in_specs = [
        pl.BlockSpec((tm, d), lambda i, f: (i, 0)),
        pl.BlockSpec((1, d), lambda i, f: (0, 0)),
    ] + w_specs
    args = [x, g.reshape(1, d), w1, w3, w2]
    if final_g is not None:
        in_specs.append(pl.BlockSpec((1, d), lambda i, f: (0, 0)))
        args.append(final_g.reshape(1, d))
    out_specs = [pl.BlockSpec((tm, d), lambda i, f: (i, 0))]
    out_shape = [jax.ShapeDtypeStruct((m, d), F32)]
    if emit_bf16:
        out_specs += [w13_spec, w13_spec, w2_spec]
        out_shape += [jax.ShapeDtypeStruct(w.shape[-2:], BF16) for w in (w1, w3, w2)]
    outs = pl.pallas_call(
        functools.partial(_ffn_kernel, nf=nf, final_norm=final_g is not None, emit_bf16=emit_bf16),
        grid=(m // tm, nf),
        in_specs=in_specs,
        out_specs=out_specs,
        out_shape=out_shape,
        scratch_shapes=[pltpu.VMEM((tm, d), BF16), pltpu.VMEM((tm, d), F32)],
        compiler_params=_params("parallel", "arbitrary"),
        name="half_ffn",
    )(*args)
    return outs if emit_bf16 else outs[0]


def _rope_heads(z, cos, sin):
    outs = []
    for h in range(z.shape[1] // LANES):
        zh = z[:, h * LANES:(h + 1) * LANES]
        outs.append(zh * cos + pltpu.roll(zh, LANES // 2, axis=1) * sin)
    return outs


def _rope_pairs(z, cos, sin):
    lane = lax.broadcasted_iota(jnp.int32, (z.shape[0], LANES), 1)
    low = (lane % 64) < 32
    outs = []
    for h in range(z.shape[1] // LANES):
        zh = z[:, h * LANES:(h + 1) * LANES]
        partner = jnp.where(low, pltpu.roll(zh, LANES - 32, axis=1), pltpu.roll(zh, 32, axis=1))
        outs.append(zh * cos + partner * sin)
    return outs


def _proj_a_kernel(x_ref, g_ref, w_ref, wkw_ref, c128_ref, s128_ref, c64_ref, s64_ref,
                   qa_ref, kaf_ref, kab_ref, vaf_ref, vab_ref, iq_ref, ikw_ref, h_scr,
                   *, n_idx_heads, idx_dim):
    j = pl.program_id(1)

    @pl.when(j == 0)
    def _():
        h_scr[...] = _rms(x_ref[...], g_ref[...]).astype(BF16)

    z = jnp.dot(h_scr[...], w_ref[...], preferred_element_type=F32)
    wa = qa_ref.shape[1]

    @pl.when(j == 0)
    def _():
        scale = HEAD_DIM ** -0.5
        for h, r in enumerate(_rope_heads(z[:, :wa], c128_ref[...], s128_ref[...])):
            qa_ref[:, h * LANES:(h + 1) * LANES] = (r * scale).astype(BF16)

    @pl.when(j == 1)
    def _():
        for h, r in enumerate(_rope_heads(z[:, :wa], c128_ref[...], s128_ref[...])):
            kaf_ref[:, h * LANES:(h + 1) * LANES] = r
            kab_ref[:, h * LANES:(h + 1) * LANES] = r.astype(BF16)

    @pl.when(j == 2)
    def _():
        kv = z[:, :wa]
        vaf_ref[...] = kv
        vab_ref[...] = kv.astype(BF16)

    @pl.when(j == 3)
    def _():
        wi = iq_ref.shape[1]
        for h, r in enumerate(_rope_pairs(z[:, :wi], c64_ref[...], s64_ref[...])):
            iq_ref[:, h * LANES:(h + 1) * LANES] = r * (idx_dim ** -0.5)
        zz = jnp.dot(h_scr[...], wkw_ref[...], preferred_element_type=F32)
        r = _rope_pairs(zz, c64_ref[...], s64_ref[...])[0]
        lane = lax.broadcasted_iota(jnp.int32, zz.shape, 1)
        ikw_ref[...] = jnp.where(lane < idx_dim, r, zz * (n_idx_heads ** -0.5))


def _proj_a(x, g, w, w_kw, tabs, n_pos_rows, ha, n_idx_heads, idx_dim):
    m, d = x.shape
    tm = min(512, m)
    wa = ha * HEAD_DIM
    wi = n_idx_heads * idx_dim
    tn = w.shape[1] // 4
    npb = n_pos_rows // tm
    row = lambda i, j: (i, 0)
    tab = lambda i, j: (i % npb, 0)
    tab_spec = pl.BlockSpec((tm, LANES), tab)
    return pl.pallas_call(
        functools.partial(_proj_a_kernel, n_idx_heads=n_idx_heads, idx_dim=idx_dim),
        grid=(m // tm, 4),
        in_specs=[
            pl.BlockSpec((tm, d), row),
            pl.BlockSpec((1, d), lambda i, j: (0, 0)),
            pl.BlockSpec((d, tn), lambda i, j: (0, j)),
            pl.BlockSpec((d, LANES), lambda i, j: (0, 0)),
            tab_spec, tab_spec, tab_spec, tab_spec,
        ],
        out_specs=[
            pl.BlockSpec((tm, wa), row), pl.BlockSpec((tm, wa), row), pl.BlockSpec((tm, wa), row),
            pl.BlockSpec((tm, wa), row), pl.BlockSpec((tm, wa), row),
            pl.BlockSpec((tm, wi), row), pl.BlockSpec((tm, LANES), row),
        ],
        out_shape=[
            jax.ShapeDtypeStruct((m, wa), BF16), jax.ShapeDtypeStruct((m, wa), F32),
            jax.ShapeDtypeStruct((m, wa), BF16), jax.ShapeDtypeStruct((m, wa), F32),
            jax.ShapeDtypeStruct((m, wa), BF16),
            jax.ShapeDtypeStruct((m, wi), F32), jax.ShapeDtypeStruct((m, LANES), F32),
        ],
        scratch_shapes=[pltpu.VMEM((tm, d), BF16)],
        compiler_params=_params("parallel", "arbitrary"),
        name="proj_a",
    )(x, g.reshape(1, d), w, w_kw, *tabs)


def _proj_a_t_kernel(x_ref, g_ref, w_ref, wkw_ref, c128_ref, s128_ref, c64_ref, s64_ref,
                     qt_ref, kaf_ref, kab_ref, vaf_ref, vt_ref, iq3t_ref, ikw_ref, ik3_ref, iwt_ref,
                     h_scr, zk, zv, sem, *, n_idx_heads, idx_dim, tq, ck):
    j = pl.program_id(1)

    @pl.when(jnp.logical_and(j == 3, pl.program_id(0) == pl.num_programs(0) - 1))
    def _():
        _heads_drain([(zk, kaf_ref, sem.at[0]), (zv, vaf_ref, sem.at[1])])

    @pl.when(j == 0)
    def _():
        h_scr[...] = _rms(x_ref[...], g_ref[...]).astype(BF16)

    z = jnp.dot(h_scr[...], w_ref[...], preferred_element_type=F32)
    tm = z.shape[0]
    wa = kab_ref.shape[1]

    @pl.when(j == 0)
    def _():
        scale = HEAD_DIM ** -0.5
        for h, r in enumerate(_rope_heads(z[:, :wa], c128_ref[...], s128_ref[...])):
            qt_ref[h * LANES:(h + 1) * LANES, :] = (r * scale).T.astype(BF16)

    @pl.when(j == 1)
    def _():
        rot = _rope_heads(z[:, :wa], c128_ref[...], s128_ref[...])

        def fill(buf):
            for h, r in enumerate(rot):
                buf[:, h * LANES:(h + 1) * LANES] = r

        _heads_writeback(zk, kaf_ref, sem.at[0], fill)
        for h, r in enumerate(rot):
            kab_ref[:, h * LANES:(h + 1) * LANES] = r.astype(BF16)

    @pl.when(j == 2)
    def _():
        def fill(buf):
            buf[...] = z[:, :wa]

        _heads_writeback(zv, vaf_ref, sem.at[1], fill)
        for h in range(wa // LANES):
            for c in range(tm // ck):
                vt_ref[c, h * LANES:(h + 1) * LANES, :] = \
                    z[c * ck:(c + 1) * ck, h * LANES:(h + 1) * LANES].T.astype(BF16)

    @pl.when(j == 3)
    def _():
        wi = n_idx_heads * idx_dim
        for g, r in enumerate(_rope_pairs(z[:, :wi], c64_ref[...], s64_ref[...])):
            r = r * (idx_dim ** -0.5)
            hi = r.astype(BF16).astype(F32)
            hi_t, lo_t = hi.T.astype(BF16), (r - hi).T.astype(BF16)
            for u in range(2):
                rows = slice(u * idx_dim, (u + 1) * idx_dim)
                for qb in range(tm // tq):
                    cols = slice(qb * tq, (qb + 1) * tq)
                    dst = slice((2 * g + u) * tq, (2 * g + u + 1) * tq)
                    iq3t_ref[qb, 0:idx_dim, dst] = hi_t[rows, cols]
                    iq3t_ref[qb, idx_dim:2 * idx_dim, dst] = lo_t[rows, cols]
                    iq3t_ref[qb, 2 * idx_dim:3 * idx_dim, dst] = hi_t[rows, cols]
        zz = jnp.dot(h_scr[...], wkw_ref[...], preferred_element_type=F32)
        r = _rope_pairs(zz, c64_ref[...], s64_ref[...])[0]
        lane = lax.broadcasted_iota(jnp.int32, zz.shape, 1)
        val = jnp.where(lane < idx_dim, r, zz * (n_idx_heads ** -0.5))
        ikw_ref[...] = val
        ik = val[:, 0:idx_dim]
        ik_hi = ik.astype(BF16).astype(F32)
        ik3_ref[...] = jnp.concatenate([ik_hi, ik_hi, ik - ik_hi], axis=1).astype(BF16)
        val_t = val.T
        for qb in range(tm // tq):
            iwt_ref[qb] = val_t[idx_dim:idx_dim + n_idx_heads, qb * tq:(qb + 1) * tq]


def _proj_a_t(x, g, w, w_kw, tabs, n_pos_rows, ha, n_idx_heads, idx_dim, tq, ck):
    m, d = x.shape
    tm = min(512, m)
    wa = ha * HEAD_DIM
    assert n_idx_heads * idx_dim <= wa and n_idx_heads % 2 == 0 and tm % ck == 0 and tm % tq == 0
    tn = w.shape[1] // 4
    npb = n_pos_rows // tm
    row = lambda i, j: (i, 0)
    lead = lambda i, j: (i, 0, 0)
    tab_spec = pl.BlockSpec((tm, LANES), lambda i, j: (i % npb, 0))
    return pl.pallas_call(
        functools.partial(_proj_a_t_kernel, n_idx_heads=n_idx_heads, idx_dim=idx_dim, tq=tq, ck=ck),
        grid=(m // tm, 4),
        in_specs=[
            pl.BlockSpec((tm, d), row),
            pl.BlockSpec((1, d), lambda i, j: (0, 0)),
            pl.BlockSpec((d, tn), lambda i, j: (0, j)),
            pl.BlockSpec((d, LANES), lambda i, j: (0, 0)),
            tab_spec, tab_spec, tab_spec, tab_spec,
        ],
        out_specs=[
            pl.BlockSpec((wa, tm), lambda i, j: (0, i)),
            pl.BlockSpec(memory_space=pl.ANY), pl.BlockSpec((tm, wa), row), pl.BlockSpec(memory_space=pl.ANY),
            pl.BlockSpec((tm // ck, wa, ck), lead),
            pl.BlockSpec((tm // tq, IDX_SPLIT * idx_dim, n_idx_heads * tq), lead),
            pl.BlockSpec((tm, LANES), row), pl.BlockSpec((tm, IDX_SPLIT * idx_dim), row),
            pl.BlockSpec((tm // tq, n_idx_heads, tq), lead),
        ],
        out_shape=[
            jax.ShapeDtypeStruct((wa, m), BF16),
            jax.ShapeDtypeStruct((m, ha, HEAD_DIM), F32), jax.ShapeDtypeStruct((m, wa), BF16),
            jax.ShapeDtypeStruct((m, ha, HEAD_DIM), F32),
            jax.ShapeDtypeStruct((m // ck, wa, ck), BF16),
            jax.ShapeDtypeStruct((m // tq, IDX_SPLIT * idx_dim, n_idx_heads * tq), BF16),
            jax.ShapeDtypeStruct((m, LANES), F32), jax.ShapeDtypeStruct((m, IDX_SPLIT * idx_dim), BF16),
            jax.ShapeDtypeStruct((m // tq, n_idx_heads, tq), F32),
        ],
        scratch_shapes=[pltpu.VMEM((tm, d), BF16), pltpu.VMEM((tm, wa), F32), pltpu.VMEM((tm, wa), F32),
                        pltpu.SemaphoreType.DMA((2, ha))],
        compiler_params=_params("arbitrary", "arbitrary"),
        name="proj_a_t",
    )(x, g.reshape(1, d), w, w_kw, *tabs)


def _head_copies(z_scr, out_hbm, sem, blk):
    tm = z_scr.shape[0]
    return [pltpu.make_async_copy(z_scr.at[:, pl.ds(h * HEAD_DIM, HEAD_DIM)],
                                  out_hbm.at[pl.ds(blk * tm, tm), h, :], sem.at[h])
            for h in range(out_hbm.shape[1])]


def _heads_writeback(z_scr, out_hbm, sem, fill):
    i = pl.program_id(0)

    @pl.when(i > 0)
    def _():
        for cp in _head_copies(z_scr, out_hbm, sem, i - 1):
            cp.wait()

    fill(z_scr)
    for cp in _head_copies(z_scr, out_hbm, sem, i):
        cp.start()


def _heads_drain(pairs):
    i = pl.program_id(0)
    for z_scr, out_hbm, sem in pairs:
        for cp in _head_copies(z_scr, out_hbm, sem, i):
            cp.wait()


def _proj_b_kernel(x_ref, g_ref, w_ref, qb_ref, kbf_ref, kbb_ref, vbf_ref, vbb_ref, h_scr, *dma,
                   heads_layout):
    j = pl.program_id(1)

    @pl.when(j == 0)
    def _():
        h_scr[...] = _rms(x_ref[...], g_ref[...]).astype(BF16)

    z = jnp.dot(h_scr[...], w_ref[...], preferred_element_type=F32)

    def fill(buf):
        buf[...] = z

    @pl.when(j == 0)
    def _():
        qb_ref[...] = (z * (HEAD_DIM ** -0.5)).astype(BF16)

    @pl.when(j == 1)
    def _():
        if heads_layout:
            _heads_writeback(dma[0], kbf_ref, dma[2].at[0], fill)
        else:
            kbf_ref[...] = z
        kbb_ref[...] = z.astype(BF16)

    @pl.when(j == 2)
    def _():
        if heads_layout:
            _heads_writeback(dma[1], vbf_ref, dma[2].at[1], fill)
        else:
            vbf_ref[...] = z
        vbb_ref[...] = z.astype(BF16)

    if heads_layout:
        @pl.when(jnp.logical_and(j == 2, pl.program_id(0) == pl.num_programs(0) - 1))
        def _():
            _heads_drain([(dma[0], kbf_ref, dma[2].at[0]), (dma[1], vbf_ref, dma[2].at[1])])


def _proj_b(x, g, w, heads_layout=False):
    m, d = x.shape
    tm = min(512, m)
    wb = w.shape[1] // 3
    hb = wb // HEAD_DIM
    row = lambda i, j: (i, 0)
    blk = pl.BlockSpec((tm, wb), row)
    if heads_layout:
        f32_spec, f32_shape = pl.BlockSpec(memory_space=pl.ANY), jax.ShapeDtypeStruct((m, hb, HEAD_DIM), F32)
        dma = [pltpu.VMEM((tm, wb), F32), pltpu.VMEM((tm, wb), F32), pltpu.SemaphoreType.DMA((2, hb))]
    else:
        f32_spec, f32_shape = blk, jax.ShapeDtypeStruct((m, wb), F32)
        dma = []
    return pl.pallas_call(
        functools.partial(_proj_b_kernel, heads_layout=heads_layout),
        grid=(m // tm, 3),
        in_specs=[
            pl.BlockSpec((tm, d), row),
            pl.BlockSpec((1, d), lambda i, j: (0, 0)),
            pl.BlockSpec((d, wb), lambda i, j: (0, j)),
        ],
        out_specs=[blk, f32_spec, blk, f32_spec, blk],
        out_shape=[jax.ShapeDtypeStruct((m, wb), BF16), f32_shape, jax.ShapeDtypeStruct((m, wb), BF16),
                   f32_shape, jax.ShapeDtypeStruct((m, wb), BF16)],
        scratch_shapes=[pltpu.VMEM((tm, d), BF16)] + dma,
        compiler_params=_params("arbitrary", "arbitrary"),
        name="proj_b",
    )(x, g.reshape(1, d), w)


def _key_to_f32(key):
    bits = jnp.where(key >= 0, key, key ^ jnp.int32(0x7FFFFFFF))
    return lax.bitcast_convert_type(bits, F32)


def _tree(parts, op):
    while len(parts) > 1:
        parts = [op(parts[k], parts[k + 1]) if k + 1 < len(parts) else parts[k] for k in range(0, len(parts), 2)]
    return parts[0]


def _fold_lanes(x, op):
    return _tree([x[:, t * LANES:(t + 1) * LANES] for t in range(x.shape[1] // LANES)], op)


def _dsa_mask(iq, iw, ik3_main, ik3_tail, tri_ref, s_scr, iq3_scr, limit, n_main, ck, n_top,
              n_idx_heads, idx_dim):
    tq = iq.shape[0]
    n_chunks = n_main + 1

    for h in range(n_idx_heads):
        x = iq[:, h * idx_dim:(h + 1) * idx_dim]
        hi = x.astype(BF16).astype(F32)
        iq3_scr[h * tq:(h + 1) * tq, :] = jnp.concatenate([hi, x - hi, hi], axis=1).astype(BF16)

    col0 = lax.broadcasted_iota(jnp.int32, (tq, ck), 1)

    def score_chunk(c, ikc):
        rel = lax.dot_general(iq3_scr[...], ikc, NT_DIMS, preferred_element_type=F32)
        acc = iw[:, 0:1] * jnp.maximum(rel[0:tq], 0.0)
        for h in range(1, n_idx_heads):
            acc = acc + iw[:, h:h + 1] * jnp.maximum(rel[h * tq:(h + 1) * tq], 0.0)
        s_scr[c] = jnp.where(col0 + c * ck < limit, acc, -jnp.inf)

    def score_main(c, _):
        score_chunk(c, ik3_main(c))
        return 0

    lax.fori_loop(0, n_main, score_main, 0)
    score_chunk(n_main, ik3_tail())

    def count(pred_fn):
        def body(c, acc):
            return acc + _fold_lanes(jnp.where(pred_fn(s_scr[c]), 1.0, 0.0), jnp.add)
        acc = lax.fori_loop(0, n_chunks, body, jnp.zeros((tq, LANES), F32))
        return jnp.sum(acc, axis=1, keepdims=True)

    kf = jnp.float32(n_top)
    cnt = count(lambda s: s >= 0.0)
    key = jnp.where(cnt >= kf, jnp.int32(0), jnp.int32(INT_MIN))

    def bit_step(b, key):
        cand = key | jnp.left_shift(jnp.int32(1), 30 - b)
        cand_f = _key_to_f32(cand)
        cnt = count(lambda s: s >= cand_f)
        return jnp.where(cnt >= kf, cand, key)

    key = lax.fori_loop(0, 31, bit_step, key)
    thr = _key_to_f32(key)
    need = kf - count(lambda s: s > thr)
    take_all = limit <= n_top

    def bias_chunk(c, carry):
        s = s_scr[c]
        eq = s == thr
        pre = jnp.dot(jnp.where(eq, 1.0, 0.0).astype(BF16), tri_ref[...], preferred_element_type=F32)
        tied = jnp.where((carry + pre) <= need, 0.0, NEG)
        bias = jnp.where(eq, tied, jnp.where(s > thr, 0.0, NEG))
        s_scr[c] = jnp.where(take_all, jnp.where(s > -jnp.inf, 0.0, NEG), bias)
        return carry + pre[:, ck - 1:ck]

    lax.fori_loop(0, n_chunks, bias_chunk, jnp.zeros((tq, 1), F32))


def _tri_incl(n):
    r = jnp.arange(n)
    return (r[:, None] <= r[None, :]).astype(BF16)


def _fold_rows(x, op):
    return _tree([x[r * 8:(r + 1) * 8] for r in range(x.shape[0] // 8)], op)


def _topk_bias_t(s_scr, n_pairs, limit, n_top, ltri_ref):
    ck, nl = s_scr.shape[1], s_scr.shape[2]

    def count(pred_fn):
        def body(pr, acc):
            a = _fold_rows(jnp.where(pred_fn(s_scr[2 * pr]), 1.0, 0.0), jnp.add)
            b = _fold_rows(jnp.where(pred_fn(s_scr[2 * pr + 1]), 1.0, 0.0), jnp.add)
            return acc + (a + b)
        acc = lax.fori_loop(0, n_pairs, body, jnp.zeros((8, nl), F32))
        return jnp.sum(acc, axis=0, keepdims=True)

    kf = jnp.float32(n_top)
    cnt = count(lambda s: s >= 0.0)
    state = (jnp.where(cnt >= kf, jnp.int32(0), jnp.int32(INT_MIN)), jnp.where(cnt >= kf, cnt, jnp.float32(2.0 ** 30)))

    def bit_step(b, state):
        key, n_ge = state
        cand = key | jnp.left_shift(jnp.int32(1), 30 - b)
        cand_f = _key_to_f32(cand)
        cnt = count(lambda s: s >= cand_f)
        return jnp.where(cnt >= kf, cand, key), jnp.where(cnt >= kf, cnt, n_ge)

    key, n_ge = lax.fori_loop(0, 31, bit_step, state)
    thr = _key_to_f32(key)
    take_all = limit <= n_top
    all_mask = lambda s: jnp.where(s > -jnp.inf, 0.0, NEG)

    def every_tie_taken():
        def chunk(c, _):
            s = s_scr[c]
            s_scr[c] = jnp.where(take_all, all_mask(s), jnp.where(s >= thr, 0.0, NEG))
            return 0
        lax.fori_loop(0, 2 * n_pairs, chunk, 0)

    def ties_by_index():
        need = kf - count(lambda s: s > thr)

        def chunk(c, carry):
            s = s_scr[c]
            eq = s == thr
            pre = jnp.dot(ltri_ref[...], jnp.where(eq, 1.0, 0.0).astype(BF16), preferred_element_type=F32)
            tied = jnp.where((carry + pre) <= need, 0.0, NEG)
            bias = jnp.where(eq, tied, jnp.where(s > thr, 0.0, NEG))
            s_scr[c] = jnp.where(take_all, all_mask(s), bias)
            return carry + pre[ck - 1:ck, :]
        lax.fori_loop(0, 2 * n_pairs, chunk, jnp.zeros((1, nl), F32))

    surplus = jnp.max(jnp.where(take_all, 0.0, n_ge - kf))
    lax.cond(surplus > 0.0, ties_by_index, every_tie_taken)


def _dsa_prompt_t_kernel(qt_ref, iq3t_ref, iwt_ref, ik3_ref, k_ref, vt_ref, ltri_ref, o_ref,
                         s_scr, acc_scr, lg_a, lg_b,
                         *, tq, ck, n_top, n_heads, n_idx_heads):
    i = pl.program_id(1)
    pos = i * tq + lax.broadcasted_iota(jnp.int32, (1, tq), 1)
    limit = (pos // CHUNK + 1) * CHUNK
    n_chunks = ((i + 1) * tq + ck - 1) // ck
    row0 = lax.broadcasted_iota(jnp.int32, (ck, tq), 0)
    iwt = iwt_ref[...]

    def rows(c):
        return pl.ds(pl.multiple_of(c * ck, ck), ck)

    def score_chunk(c, _):
        ikc = ik3_ref[rows(c), :]
        acc = None
        for g in range(n_idx_heads // 2):
            rel = jnp.dot(ikc, iq3t_ref[:, 2 * g * tq:2 * (g + 1) * tq], preferred_element_type=F32)
            for u in range(2):
                h = 2 * g + u
                term = iwt[h:h + 1, :] * jnp.maximum(rel[:, u * tq:(u + 1) * tq], 0.0)
                acc = term if acc is None else acc + term
        s_scr[c] = jnp.where(row0 + c * ck < limit, acc, -jnp.inf)
        return 0

    lax.fori_loop(0, n_chunks, score_chunk, 0)

    n_pairs = (n_chunks + 1) // 2
    last = 2 * n_pairs - 1

    @pl.when(n_chunks % 2 == 1)
    def _():
        s_scr[n_chunks] = jnp.full((ck, tq), -jnp.inf, F32)

    _topk_bias_t(s_scr, n_pairs, limit, n_top, ltri_ref)

    heads = [slice(h * HEAD_DIM, (h + 1) * HEAD_DIM) for h in range(n_heads)]
    acc_scr[...] = jnp.zeros_like(acc_scr)

    def logits(c, buf):
        for h, hs in enumerate(heads):
            buf[h] = jnp.dot(k_ref[rows(c), hs], qt_ref[hs, :], preferred_element_type=F32)

    def attend(c, buf, ms, ls):
        bias = s_scr[c]
        new_m, new_l, pend = [], [], []
        for h, hs in enumerate(heads):
            lg = buf[h] + bias
            m_new = jnp.maximum(ms[h], jnp.max(_fold_rows(lg, jnp.maximum), axis=0, keepdims=True))
            alpha = jnp.exp(ms[h] - m_new)
            p = jnp.exp(lg - m_new)
            new_l.append(alpha * ls[h] + _fold_rows(p, jnp.add))
            new_m.append(m_new)
            pend.append((alpha, p.astype(BF16)))
        for hs, (alpha, p) in zip(heads, pend):
            acc_scr[hs, :] = alpha * acc_scr[hs, :] + jnp.dot(vt_ref[c, hs, :], p, preferred_element_type=F32)
        return tuple(new_m), tuple(new_l)

    def att_pair(pr, state):
        c0 = 2 * pr
        logits(c0 + 1, lg_b)
        state = attend(c0, lg_a, *state)
        logits(jnp.minimum(c0 + 2, last), lg_a)
        return attend(c0 + 1, lg_b, *state)

    logits(0, lg_a)
    init = (tuple(jnp.full((1, tq), NEG, F32) for _ in heads), tuple(jnp.zeros((8, tq), F32) for _ in heads))
    _, ls = lax.fori_loop(0, n_pairs, att_pair, init)
    for h, hs in enumerate(heads):
        o_t = acc_scr[hs, :] / jnp.sum(ls[h], axis=0, keepdims=True)
        o_ref[:, hs] = o_t.T.astype(o_ref.dtype)


def _dsa_prompt_t(qt, iq3t, iwt, ik3, ka, vt, b, tq, ck):
    wa, m = qt.shape
    s = m // b
    nq = s // tq
    n_heads = wa // HEAD_DIM
    n_idx_heads = iwt.shape[1]
    idx_dim = ik3.shape[1] // IDX_SPLIT
    n_top = min(TOPK_MAX, s // 4)
    assert s % (2 * ck) == 0
    ik3 = ik3.reshape(b, s, IDX_SPLIT * idx_dim)
    ka = ka.reshape(b, s, wa)
    vt = vt.reshape(b, s // ck, wa, ck)
    r = jnp.arange(ck)
    ltri = (r[None, :] <= r[:, None]).astype(BF16)

    return pl.pallas_call(
        functools.partial(_dsa_prompt_t_kernel, tq=tq, ck=ck, n_top=n_top, n_heads=n_heads,
                          n_idx_heads=n_idx_heads),
        grid=(b, nq),
        in_specs=[
            pl.BlockSpec((wa, tq), lambda bb, i: (0, bb * nq + i)),
            pl.BlockSpec((None, IDX_SPLIT * idx_dim, n_idx_heads * tq), lambda bb, i: (bb * nq + i, 0, 0)),
            pl.BlockSpec((None, n_idx_heads, tq), lambda bb, i: (bb * nq + i, 0, 0)),
            pl.BlockSpec((None, s, IDX_SPLIT * idx_dim), lambda bb, i: (bb, 0, 0)),
            pl.BlockSpec((None, s, wa), lambda bb, i: (bb, 0, 0)),
            pl.BlockSpec((None, s // ck, wa, ck), lambda bb, i: (bb, 0, 0, 0)),
            pl.BlockSpec((ck, ck), lambda bb, i: (0, 0)),
        ],
        out_specs=pl.BlockSpec((None, tq, wa), lambda bb, i: (bb, i, 0)),
        out_shape=jax.ShapeDtypeStruct((b, s, wa), BF16),
        scratch_shapes=[pltpu.VMEM((s // ck, ck, tq), F32), pltpu.VMEM((wa, tq), F32),
                        pltpu.VMEM((n_heads, ck, tq), F32), pltpu.VMEM((n_heads, ck, tq), F32)],
        compiler_params=_params("parallel", "arbitrary"),
        name="dsa_prompt",
    )(qt, iq3t, iwt, ik3, ka, vt, ltri)


def _sb_tiles(qs, kvs, m2, carries, vis):
    zs = [lax.dot_general(q_h, kt, NT_DIMS, preferred_element_type=F32) for q_h, (kt, _) in zip(qs, kvs)]
    mids = []
    for z in zs:
        sp = jnp.maximum(z, 0.0) + jnp.log(1.0 + jnp.exp(-jnp.abs(z)))
        lk = -sp if vis is None else jnp.where(vis, -sp, 0.0)
        hi, lo = _split_bf16(lk)
        after = jnp.dot(jnp.concatenate([hi, lo], axis=1), m2, preferred_element_type=F32)
        mids.append((z - sp, lk, after))
    outs = []
    for (lsig, lk, after), (_, vt), carry in zip(mids, kvs, carries):
        a = jnp.exp(lsig + after + carry)
        if vis is not None:
            a = jnp.where(vis, a, 0.0)
        outs.append((after[:, 0:1] + lk[:, 0:1], jnp.dot(a.astype(BF16), vt, preferred_element_type=F32)))
    return outs


def _sb_core(q_ref, diag_kv, past_kv, m2_ref, o_ref, acc_scr, n_past, n_heads, tk):
    tq = q_ref.shape[0]
    vis = lax.broadcasted_iota(jnp.int32, (tq, tk), 1) < lax.broadcasted_iota(jnp.int32, (tq, tk), 0)
    m2 = m2_ref[...]
    heads = [slice(h * HEAD_DIM, (h + 1) * HEAD_DIM) for h in range(n_heads)]

    qs = [q_ref[:, hs] for hs in heads]
    zero = jnp.zeros((tq, 1), F32)
    carries = []
    for hs, (dc, contrib) in zip(heads, _sb_tiles(qs, [diag_kv(h) for h in range(n_heads)], m2,
                                                  [zero] * n_heads, vis)):
        acc_scr[:, hs] = contrib
        carries.append(dc)

    def alive(cs):
        m = cs[0]
        for c in cs[1:]:
            m = jnp.maximum(m, c)
        return jnp.max(m)

    def cond(state):
        step, top, _ = state
        return jnp.logical_and(step < n_past, top > SB_DEAD)

    def body(state):
        step, _, cs = state
        j = n_past - 1 - step
        new = []
        tiles = _sb_tiles([q_ref[:, hs] for hs in heads], [past_kv(j, h) for h in range(n_heads)], m2, cs, None)
        for hs, c, (dc, contrib) in zip(heads, cs, tiles):
            acc_scr[:, hs] += contrib
            new.append(c + dc)
        return step + 1, alive(new), tuple(new)

    lax.while_loop(cond, body, (jnp.int32(0), alive(carries), tuple(carries)))
    o_ref[...] = acc_scr[...].astype(o_ref.dtype)


def _tri_after(n):
    r = jnp.arange(n)
    m = (r[:, None] > r[None, :]).astype(BF16)
    return jnp.concatenate([m, m], axis=0)


def _sb_prompt_kernel(q_ref, k_ref, v_ref, m2_ref, o_ref, acc_scr, *, n_heads, tk):
    i = pl.program_id(1)

    def tile(j, h):
        rows = pl.ds(pl.multiple_of(j * tk, tk), tk)
        cols = slice(h * HEAD_DIM, (h + 1) * HEAD_DIM)
        return k_ref[rows, cols], v_ref[rows, cols]

    _sb_core(q_ref, lambda h: tile(i, h), tile, m2_ref, o_ref, acc_scr, i, n_heads, tk)


def _sb_prompt(qb, kb, vb, n_heads):
    b, s, wb = qb.shape
    tq = tk = 128
    qblk = lambda bb, i: (bb, i, 0)
    full = lambda bb, i: (bb, 0, 0)
    return pl.pallas_call(
        functools.partial(_sb_prompt_kernel, n_heads=n_heads, tk=tk),
        grid=(b, s // tq),
        in_specs=[
            pl.BlockSpec((None, tq, wb), qblk),
            pl.BlockSpec((None, s, wb), full),
            pl.BlockSpec((None, s, wb), full),
            pl.BlockSpec((2 * tk, tk), lambda bb, i: (0, 0)),
        ],
        out_specs=pl.BlockSpec((None, tq, wb), qblk),
        out_shape=jax.ShapeDtypeStruct((b, s, wb), BF16),
        scratch_shapes=[pltpu.VMEM((tq, wb), F32)],
        compiler_params=_params("parallel", "arbitrary"),
        name="sb_prompt",
    )(qb, kb, vb, _tri_after(tk))


def _pad_rows(new_ref, buf):
    t = new_ref.shape[0]
    buf[0:t, :] = new_ref[...].astype(buf.dtype)
    buf[t:, :] = jnp.zeros((buf.shape[0] - t, buf.shape[1]), buf.dtype)


def _ik3(x):
    hi, lo = _split_bf16(x)
    return jnp.concatenate([hi, hi, lo], axis=1)


def _dsa_sample_kernel(q_ref, iq_ref, ikw_ref, kn_ref, vn_ref, cik_ref, tri_ref, ck_hbm, cv_hbm,
                       o_ref, knew, vnew, iknew, s_scr, iq3_scr, kbuf, vbuf, sem,
                       *, past, t, ck, n_top, n_heads, n_idx_heads, idx_dim):
    b = pl.program_id(0)

    def head_copies(h):
        return (pltpu.make_async_copy(ck_hbm.at[b, :, h, :], kbuf.at[h], sem.at[0, h]),
                pltpu.make_async_copy(cv_hbm.at[b, :, h, :], vbuf.at[h], sem.at[1, h]))

    for h in range(n_heads):
        for cp in head_copies(h):
            cp.start()
    tn = knew.shape[0]
    _pad_rows(kn_ref, knew)
    _pad_rows(vn_ref, vnew)
    iknew[0:t, :] = ikw_ref[:, 0:idx_dim]
    iknew[t:, :] = jnp.zeros((ck - t, idx_dim), F32)
    limit = jnp.full((t, 1), past + t, jnp.int32)
    iw = ikw_ref[:, idx_dim:idx_dim + n_idx_heads]
    n_main = past // ck

    def ik3_main(c):
        return _ik3(cik_ref[pl.ds(pl.multiple_of(c * ck, ck), ck), :])

    _dsa_mask(iq_ref[...], iw, ik3_main, lambda: _ik3(iknew[...]), tri_ref, s_scr, iq3_scr,
              limit, n_main, ck, n_top, n_idx_heads, idx_dim)

    bias_past = jnp.concatenate([s_scr[c] for c in range(n_main)], axis=1)
    bias_new = s_scr[n_main][:, 0:tn]
    for h in range(n_heads):
        hs = slice(h * HEAD_DIM, (h + 1) * HEAD_DIM)
        q_h = q_ref[:, hs]
        for cp in head_copies(h):
            cp.wait()
        lg_p = lax.dot_general(q_h, kbuf[h].astype(BF16), NT_DIMS, preferred_element_type=F32) + bias_past
        lg_n = lax.dot_general(q_h, knew[:, hs], NT_DIMS, preferred_element_type=F32) + bias_new
        m = jnp.maximum(jnp.max(_fold_lanes(lg_p, jnp.maximum), axis=1, keepdims=True),
                        jnp.max(lg_n, axis=1, keepdims=True))
        p_p = jnp.exp(lg_p - m)
        p_n = jnp.exp(lg_n - m)
        l = jnp.sum(_fold_lanes(p_p, jnp.add), axis=1, keepdims=True) + jnp.sum(p_n, axis=1, keepdims=True)
        o = jnp.dot(p_p.astype(BF16), vbuf[h].astype(BF16), preferred_element_type=F32)
        o = o + jnp.dot(p_n.astype(BF16), vnew[:, hs], preferred_element_type=F32)
        o_ref[:, hs] = (o / l).astype(o_ref.dtype)


def _dsa_sample(qa, iq, ikw, ka, va, cache_k, cache_v, cache_ik, n_idx_heads):
    b, t, wa = qa.shape
    past, n_heads = cache_k.shape[1], cache_k.shape[2]
    idx_dim = cache_ik.shape[2]
    assert (past // CHUNK + 1) * CHUNK >= past + t, "new frames must sit in one open chunk"
    ck = DSA_CHUNK
    tn = LANES
    assert past % ck == 0 and t <= tn <= ck
    n_top = min(TOPK_MAX, (past + t) // 4)
    row = lambda bb: (bb, 0, 0)
    hbm = pl.BlockSpec(memory_space=pl.ANY)
    return pl.pallas_call(
        functools.partial(_dsa_sample_kernel, past=past, t=t, ck=ck, n_top=n_top, n_heads=n_heads,
                          n_idx_heads=n_idx_heads, idx_dim=idx_dim),
        grid=(b,),
        in_specs=[
            pl.BlockSpec((None, t, wa), row),
            pl.BlockSpec((None, t, iq.shape[2]), row),
            pl.BlockSpec((None, t, LANES), row),
            pl.BlockSpec((None, t, wa), row),
            pl.BlockSpec((None, t, wa), row),
            pl.BlockSpec((None, past, idx_dim), row),
            pl.BlockSpec((ck, ck), lambda bb: (0, 0)),
            hbm, hbm,
        ],
        out_specs=pl.BlockSpec((None, t, wa), row),
        out_shape=jax.ShapeDtypeStruct((b, t, wa), BF16),
        scratch_shapes=[
            pltpu.VMEM((tn, wa), BF16), pltpu.VMEM((tn, wa), BF16), pltpu.VMEM((ck, idx_dim), F32),
            pltpu.VMEM((past // ck + 1, t, ck), F32), pltpu.VMEM((n_idx_heads * t, IDX_SPLIT * idx_dim), BF16),
            pltpu.VMEM((n_heads, past, HEAD_DIM), F32), pltpu.VMEM((n_heads, past, HEAD_DIM), F32),
            pltpu.SemaphoreType.DMA((2, n_heads)),
        ],
        compiler_params=_params("arbitrary"),
        name="dsa_sample",
    )(qa, iq, ikw, ka, va, cache_ik, _tri_incl(ck), cache_k, cache_v)


def _sb_sample_kernel(q_ref, kn_ref, vn_ref, ck_ref, cv_ref, m2_ref, o_ref, knew, vnew, acc_scr,
                      *, past, t, tk, n_heads):
    for new_ref, buf in ((kn_ref, knew), (vn_ref, vnew)):
        buf[0:t, :] = new_ref[...]
        buf[t:, :] = jnp.zeros((tk - t, buf.shape[1]), BF16)

    def new_kv(h):
        cols = slice(h * HEAD_DIM, (h + 1) * HEAD_DIM)
        return knew[:, cols], vnew[:, cols]

    def cache_kv(j, h):
        rows = pl.ds(pl.multiple_of(j * tk, tk), tk)
        return ck_ref[rows, h, :].astype(BF16), cv_ref[rows, h, :].astype(BF16)

    _sb_core(q_ref, new_kv, cache_kv, m2_ref, o_ref, acc_scr, past // tk, n_heads, tk)


def _sb_sample(qb, kb, vb, cache_k, cache_v):
    b, t, wb = qb.shape
    past, n_heads = cache_k.shape[1], cache_k.shape[2]
    tk = 128
    assert past % tk == 0 and t <= tk
    row = lambda bb: (bb, 0, 0)
    cache = lambda bb: (bb, 0, 0, 0)
    return pl.pallas_call(
        functools.partial(_sb_sample_kernel, past=past, t=t, tk=tk, n_heads=n_heads),
        grid=(b,),
        in_specs=[
            pl.BlockSpec((None, t, wb), row),
            pl.BlockSpec((None, t, wb), row),
            pl.BlockSpec((None, t, wb), row),
            pl.BlockSpec((None, past, n_heads, HEAD_DIM), cache),
            pl.BlockSpec((None, past, n_heads, HEAD_DIM), cache),
            pl.BlockSpec((2 * tk, tk), lambda bb: (0, 0)),
        ],
        out_specs=pl.BlockSpec((None, t, wb), row),
        out_shape=jax.ShapeDtypeStruct((b, t, wb), BF16),
        scratch_shapes=[pltpu.VMEM((tk, wb), BF16), pltpu.VMEM((tk, wb), BF16), pltpu.VMEM((t, wb), F32)],
        compiler_params=_params("parallel"),
        name="sb_sample",
    )(qb, kb, vb, cache_k, cache_v, _tri_after(tk))


def _out_proj_kernel(y_ref, oa_ref, ob_ref, wa_ref, wb_ref, o_ref):
    acc = jnp.dot(oa_ref[...], wa_ref[...], preferred_element_type=F32)
    acc = acc + jnp.dot(ob_ref[...], wb_ref[...], preferred_element_type=F32)
    o_ref[...] = y_ref[...] + acc


def _out_proj(y, oa, ob, w_a, w_b):
    m, d = y.shape
    tm = min(512, m)
    row = lambda i: (i, 0)
    const = lambda i: (0, 0)
    return pl.pallas_call(
        _out_proj_kernel,
        grid=(m // tm,),
        in_specs=[
            pl.BlockSpec((tm, d), row),
            pl.BlockSpec((tm, oa.shape[1]), row),
            pl.BlockSpec((tm, ob.shape[1]), row),
            pl.BlockSpec(w_a.shape, const),
            pl.BlockSpec(w_b.shape, const),
        ],
        out_specs=pl.BlockSpec((tm, d), row),
        out_shape=jax.ShapeDtypeStruct((m, d), F32),
        compiler_params=_params("parallel"),
        name="out_proj",
    )(y, oa, ob, w_a, w_b)


def _gelu(x):
    return 0.5 * x * (1.0 + jnp.tanh(math.sqrt(2.0 / math.pi) * (x + 0.044715 * (x * x * x))))


def _c_v_kernel(x_ref, g_ref, w_ref, vg_ref, *outs):
    h = _rms(x_ref[...], g_ref[...]).astype(BF16)
    v = _gelu(jnp.dot(h, w_ref[...], preferred_element_type=F32))
    vn = _rms(v, vg_ref[...])
    outs[0][...] = vn.astype(BF16)
    if len(outs) > 1:
        outs[1][...] = vn


def _c_v(x, g, w_v, v_gain, want_f32):
    m, d = x.shape
    cw = w_v.shape[1]
    tm = min(512, m)
    row = lambda i: (i, 0)
    const = lambda i: (0, 0)
    out_specs = [pl.BlockSpec((tm, cw), row)]
    out_shape = [jax.ShapeDtypeStruct((m, cw), BF16)]
    if want_f32:
        out_specs.append(pl.BlockSpec((tm, cw), row))
        out_shape.append(jax.ShapeDtypeStruct((m, cw), F32))
    return pl.pallas_call(
        _c_v_kernel,
        grid=(m // tm,),
        in_specs=[
            pl.BlockSpec((tm, d), row),
            pl.BlockSpec((1, d), const),
            pl.BlockSpec((d, cw), const),
            pl.BlockSpec((1, cw), const),
        ],
        out_specs=out_specs,
        out_shape=out_shape,
        compiler_params=_params("parallel"),
        name="c_v",
    )(x, g.reshape(1, d), w_v, v_gain.reshape(1, cw))


def _c_mix_kernel(x_ref, g_ref, wu_ref, vn_ref, wm_ref, bias_ref, wo_ref, o_ref, h_scr, acc_scr, p_scr,
                  *, nj, gs):
    j = pl.program_id(1)

    @pl.when(j == 0)
    def _():
        h_scr[...] = _rms(x_ref[...], g_ref[...]).astype(BF16)
        acc_scr[...] = jnp.zeros_like(acc_scr)

    tm = h_scr.shape[0]
    tiles = [(slice(r * C_CHUNK, (r + 1) * C_CHUNK), slice(gg * LANES, (gg + 1) * LANES), gg)
             for gg in range(gs) for r in range(tm // C_CHUNK)]
    mixes = [jnp.dot(wm_ref[j * gs + gg], vn_ref[rs, cs], preferred_element_type=F32) for rs, cs, gg in tiles]
    u = _gelu(jnp.dot(h_scr[...], wu_ref[...], preferred_element_type=F32))
    for (rs, cs, _), mix in zip(tiles, mixes):
        p_scr[rs, cs] = (u[rs, cs] * (mix + bias_ref[:, cs])).astype(BF16)
    acc_scr[...] += jnp.dot(p_scr[...], wo_ref[...], preferred_element_type=F32)

    @pl.when(j == nj - 1)
    def _():
        o_ref[...] = x_ref[...] + acc_scr[...]


def _c_mix(x, g, w_u, vn, w_m, bias, w_o, gs=4):
    m, d = x.shape
    cw = w_u.shape[1]
    ng = cw // LANES
    gs = min(gs, ng)
    nj = ng // gs
    tm = min(512, m)
    row = lambda i, j: (i, 0)
    return pl.pallas_call(
        functools.partial(_c_mix_kernel, nj=nj, gs=gs),
        grid=(m // tm, nj),
        in_specs=[
            pl.BlockSpec((tm, d), row),
            pl.BlockSpec((1, d), lambda i, j: (0, 0)),
            pl.BlockSpec((d, gs * LANES), lambda i, j: (0, j)),
            pl.BlockSpec((tm, gs * LANES), lambda i, j: (i, j)),
            pl.BlockSpec(w_m.shape, lambda i, j: (0, 0, 0)),
            pl.BlockSpec((C_CHUNK, gs * LANES), lambda i, j: (0, j)),
            pl.BlockSpec((gs * LANES, d), lambda i, j: (j, 0)),
        ],
        out_specs=pl.BlockSpec((tm, d), row),
        out_shape=jax.ShapeDtypeStruct((m, d), F32),
        scratch_shapes=[pltpu.VMEM((tm, d), BF16), pltpu.VMEM((tm, d), F32),
                        pltpu.VMEM((tm, gs * LANES), BF16)],
        compiler_params=_params("parallel", "arbitrary"),
        name="c_mix",
    )(x, g.reshape(1, d), w_u, vn, w_m, bias, w_o)


def _rope_tables(pos, reps):
    pos = pos.astype(F32)[:, None]

    def tab(half, copies):
        inv = ROPE_THETA ** (-jnp.arange(half, dtype=F32) / half)
        ang = pos * inv[None, :]
        c, s = jnp.cos(ang), jnp.sin(ang)
        return jnp.tile(jnp.concatenate([c, c], axis=1), (reps, copies)), \
            jnp.tile(jnp.concatenate([-s, s], axis=1), (reps, copies))

    c128, s128 = tab(HEAD_DIM // 2, 1)
    c64, s64 = tab(32, 2)
    return c128, s128, c64, s64


def kernel(x_prompt, x_sample, cache_a_k, cache_a_v, cache_a_ik, cache_b_k, cache_b_v, norm_ff1, ff1_w1, ff1_w3, ff1_w2, norm_mix, norm_ff2, ff2_w1, ff2_w3, ff2_w2, ab_w_in, ab_w_out, c_w_in, c_v_norm, c_w_s, c_b_s, c_w_out, final_norm):
    bp, seq, d = x_prompt.shape
    bs, t, _ = x_sample.shape
    past, ha = cache_a_k.shape[2], cache_a_k.shape[3]
    hb = cache_b_k.shape[3]
    idx_dim = cache_a_ik.shape[3]
    wa, wb = ha * HEAD_DIM, hb * HEAD_DIM
    n_idx_heads = (ab_w_in.shape[2] - 3 * wa - 3 * wb - idx_dim) // (idx_dim + 1)
    wi = n_idx_heads * idx_dim
    assert idx_dim == 64 and wi % LANES == 0 and wi <= wa and n_idx_heads <= LANES - idx_dim
    depth = norm_ff1.shape[0]
    mp, ms = bp * seq, bs * t

    yp = x_prompt.reshape(mp, d)
    ys = x_sample.reshape(ms, d)
    bf = lambda w: w.astype(BF16)

    tabs_p = _rope_tables(jnp.arange(seq), 1)
    tabs_s = _rope_tables(past + jnp.arange(t), min(512, ms) // t)

    outs_p, outs_s, s_cv = [], [], []
    for layer in range(depth):
        j = layer // 2
        ys, w1, w3, w2 = _half_ffn(ys, norm_ff1[layer], ff1_w1, ff1_w3, ff1_w2, emit_bf16=True, layer=layer)
        yp = _half_ffn(yp, norm_ff1[layer], w1, w3, w2)
        if layer % 2 == 0:
            w_in = ab_w_in[j]
            o = 0
            cols = []
            for width in (wa, wa, wa, wi, idx_dim, n_idx_heads, wb, wb, wb):
                cols.append(w_in[:, o:o + width])
                o += width
            w_qa, w_ka, w_va, w_iq, w_ik, w_iw, w_qb, w_kb, w_vb = cols
            pad = lambda w, width: jnp.pad(w, ((0, 0), (0, width - w.shape[1])))
            w_pa = bf(jnp.concatenate([w_qa, w_ka, w_va, pad(w_iq, wa)], axis=1))
            w_kw = bf(pad(jnp.concatenate([w_ik, w_iw], axis=1), LANES))
            w_pb = bf(jnp.concatenate([w_qb, w_kb, w_vb], axis=1))
            w_oa, w_ob = bf(ab_w_out[j][:wa]), bf(ab_w_out[j][wa:])

            tq = DSA_QUERIES
            qt, kaf, kab, vaf, vt, iq3t, ikw, ik3, iwt = _proj_a_t(
                yp, norm_mix[layer], w_pa, w_kw, tabs_p, seq, ha, n_idx_heads, idx_dim, tq, DSA_CHUNK)
            qb, kbf, kbb, vbf, vbb = _proj_b(yp, norm_mix[layer], w_pb, heads_layout=True)
            r3 = lambda a: a.reshape(bp, seq, a.shape[-1])
            ik = ikw[:, :idx_dim]
            o_a = _dsa_prompt_t(qt, iq3t, iwt, ik3, kab, vt, bp, tq, DSA_CHUNK)
            o_b = _sb_prompt(r3(qb), r3(kbb), r3(vbb), hb)
            yp = _out_proj(yp, o_a.reshape(mp, wa), o_b.reshape(mp, wb), w_oa, w_ob)
            outs_p.append((kaf.reshape(bp, seq, ha, HEAD_DIM), vaf.reshape(bp, seq, ha, HEAD_DIM),
                           ik.reshape(bp, seq, idx_dim),
                           kbf.reshape(bp, seq, hb, HEAD_DIM), vbf.reshape(bp, seq, hb, HEAD_DIM)))

            qa, kaf, kab, vaf, vab, iq, ikw = _proj_a(
                ys, norm_mix[layer], w_pa, w_kw, tabs_s, min(512, ms), ha, n_idx_heads, idx_dim)
            qb, kbf, kbb, vbf, vbb = _proj_b(ys, norm_mix[layer], w_pb)
            r3 = lambda a: a.reshape(bs, t, a.shape[-1])
            o_a = _dsa_sample(r3(qa), r3(iq), r3(ikw), r3(kab), r3(vab),
                              cache_a_k[j], cache_a_v[j], cache_a_ik[j], n_idx_heads)
            o_b = _sb_sample(r3(qb), r3(kbb), r3(vbb), cache_b_k[j], cache_b_v[j])
            ys = _out_proj(ys, o_a.reshape(ms, wa), o_b.reshape(ms, wb), w_oa, w_ob)
            outs_s.append((kaf.reshape(bs, t, ha, HEAD_DIM), vaf.reshape(bs, t, ha, HEAD_DIM),
                           ikw[:, :idx_dim].reshape(bs, t, idx_dim),
                           kbf.reshape(bs, t, hb, HEAD_DIM), vbf.reshape(bs, t, hb, HEAD_DIM)))
        else:
            cw = c_w_in.shape[2] // 2
            ng = c_w_s.shape[1]
            w_u, w_v = bf(c_w_in[j][:, :cw]), bf(c_w_in[j][:, cw:])
            w_o = bf(c_w_out[j])
            i = jnp.arange(C_CHUNK)
            mask = (i[None, :] // CHUNK) <= (i[:, None] // CHUNK)
            w_m = jnp.where(mask[None], c_w_s[j], 0.0)
            bias_p = jnp.repeat(c_b_s[j].T, cw // ng, axis=1)
            per = C_CHUNK // t
            w_ms = jnp.einsum('ab,gij->gaibj', jnp.eye(per, dtype=F32), w_m[:, :t, :t]).reshape(ng, C_CHUNK, C_CHUNK)
            bias_s = jnp.tile(bias_p[:t], (per, 1))

            vn = _c_v(yp, norm_mix[layer], w_v, c_v_norm[j], False)[0]
            yp = _c_mix(yp, norm_mix[layer], w_u, vn, bf(w_m), bias_p, w_o)
            vn, vn_f32 = _c_v(ys, norm_mix[layer], w_v, c_v_norm[j], True)
            ys = _c_mix(ys, norm_mix[layer], w_u, vn, bf(w_ms), bias_s, w_o)
            s_cv.append(vn_f32.reshape(bs, t, cw))
        last = layer == depth - 1
        fg = final_norm if last else None
        ys, w1, w3, w2 = _half_ffn(ys, norm_ff2[layer], ff2_w1, ff2_w3, ff2_w2, fg, emit_bf16=True, layer=layer)
        yp = _half_ffn(yp, norm_ff2[layer], w1, w3, w2, fg)

    stack = lambda outs, k: jnp.stack([o[k] for o in outs])
    return (yp.reshape(bp, seq, d), ys.reshape(bs, t, d),
            stack(outs_p, 0), stack(outs_p, 1), stack(outs_p, 2), stack(outs_p, 3), stack(outs_p, 4),
            stack(outs_s, 0), stack(outs_s, 1), stack(outs_s, 2), stack(outs_s, 3), stack(outs_s, 4),
            jnp.stack(s_cv))
```
